```python
import jax, jax.numpy as jnp
from jax import lax
import numpy as np

D_MODEL = 1024
BATCH = 8
SEQ = 8192
DEPTH = 1
DEC_BATCH = 32
DEC_SEQ = 64
PAST_LEN = 1024

CHUNK = 64
Q_BLOCK = 128
N_MEM = 256
EPS = 1e-6
NEG_INF = -1e30

GLA_HEADS = 4
GLA_DK = 64
GLA_DV = 128
GLA_GATE_RANK = 16
GLA_GATE_NORM = 16.0
GLA_WIDTH = GLA_HEADS * GLA_DV

MLA_HEADS = 4
MLA_Q_RANK = 256
MLA_KV_RANK = 128
MLA_NOPE = 64
MLA_ROPE = 32
MLA_DV = 128
MLA_WIDTH = MLA_HEADS * MLA_DV
MLA_SCALE = (MLA_NOPE + MLA_ROPE) ** -0.5
ROPE_THETA = 10000.0

D_MIX = GLA_WIDTH + MLA_WIDTH

MEM_HEADS = 4
MEM_DH = D_MODEL // MEM_HEADS

IN_SPLITS = (GLA_HEADS * GLA_DK, GLA_HEADS * GLA_DK, GLA_WIDTH, GLA_GATE_RANK, GLA_WIDTH,
             MLA_Q_RANK, MLA_KV_RANK, MLA_ROPE, MLA_WIDTH)
D_IN = (GLA_HEADS * GLA_DK * 2 + GLA_WIDTH * 2 + GLA_GATE_RANK
        + MLA_Q_RANK + MLA_KV_RANK + MLA_ROPE + MLA_WIDTH)

kernel_name = 'hybrid_gla_mla_streaming_step'


def rmsnorm(x, g):
    xf = x.astype(jnp.float32)
    y = xf * lax.rsqrt(jnp.mean(xf * xf, axis=-1, keepdims=True) + EPS)
    return (y * g.astype(jnp.float32)).astype(x.dtype)


def rope(x, pos):
    half = MLA_ROPE // 2
    inv = ROPE_THETA ** (-jnp.arange(half, dtype=jnp.float32) / half)
    ang = pos.astype(jnp.float32)[:, None] * inv[None, :]
    cos = jnp.cos(ang)[None, :, None, :]
    sin = jnp.sin(ang)[None, :, None, :]
    xf = x.astype(jnp.float32)
    x1, x2 = xf[..., :half], xf[..., half:]
    return jnp.concatenate([x1 * cos - x2 * sin, x2 * cos + x1 * sin], axis=-1).astype(x.dtype)


def gla_chunked(q, k, v, log_a, s0, chunk):
    bsz, t, h, _ = q.shape
    dv = v.shape[-1]
    n = t // chunk

    def to_chunks(a):
        return a.astype(jnp.float32).reshape(bsz, n, chunk, h, a.shape[-1]).transpose(1, 0, 3, 2, 4)

    qc, kc, vc, gc = to_chunks(q), to_chunks(k), to_chunks(v), to_chunks(log_a)
    bc = jnp.cumsum(gc, axis=3)
    causal = jnp.tril(jnp.ones((chunk, chunk), dtype=bool))

    def step(s, inp):
        qi, ki, vi, bi = inp
        b_last = bi[:, :, -1:, :]
        q_dec = qi * jnp.exp(bi)
        k_dec = ki * jnp.exp(-bi)
        a = jnp.where(causal, jnp.einsum('bhid,bhjd->bhij', q_dec, k_dec), 0.0)
        o = jnp.einsum('bhid,bhde->bhie', q_dec, s) + jnp.einsum('bhij,bhje->bhie', a, vi)
        s_new = (jnp.exp(b_last[:, :, 0, :])[..., None] * s
                 + jnp.einsum('bhjd,bhje->bhde', ki * jnp.exp(b_last - bi), vi))
        return s_new, o

    s_fin, o = lax.scan(step, s0.astype(jnp.float32), (qc, kc, vc, bc))
    o = o.transpose(1, 0, 3, 2, 4).reshape(bsz, t, h, dv)
    return o, s_fin


def mla_attend_rows(qn, qr, qpos, kn, kr, v, kpos):
    s = (jnp.einsum('bqhd,bkhd->bhqk', qn, kn).astype(jnp.float32)
         + jnp.einsum('bqhr,bkr->bhqk', qr, kr).astype(jnp.float32)) * MLA_SCALE
    allowed = (kpos[None, :] // CHUNK) <= (qpos[:, None] // CHUNK)
    s = jnp.where(allowed, s, NEG_INF)
    p = jax.nn.softmax(s, axis=-1).astype(v.dtype)
    return jnp.einsum('bhqk,bkhd->bqhd', p, v)


def mixer(h, pos, s0, past_ckv, past_krope, past_pos, w_in, w_gla_g2, b_gla_g, g_gla_out,
          g_qa, w_qb, g_kva, w_kvb, w_out, gla_chunk, blocked):
    bsz, t, _ = h.shape
    offsets = np.cumsum(np.array(IN_SPLITS))[:-1].tolist()
    gq, gk, gv, glr, gz, cq, ckv_raw, kr_raw, mz = jnp.split(h @ w_in, offsets, axis=-1)

    q = gq.reshape(bsz, t, GLA_HEADS, GLA_DK) * (GLA_DK ** -0.5)
    k = gk.reshape(bsz, t, GLA_HEADS, GLA_DK)
    v = gv.reshape(bsz, t, GLA_HEADS, GLA_DV)
    log_a = (jax.nn.log_sigmoid((glr @ w_gla_g2 + b_gla_g).astype(jnp.float32))
             / GLA_GATE_NORM).reshape(bsz, t, GLA_HEADS, GLA_DK)
    o_gla, s_fin = gla_chunked(q, k, v, log_a, s0, gla_chunk)
    o_gla = rmsnorm(o_gla.astype(h.dtype), g_gla_out).reshape(bsz, t, GLA_WIDTH) * jax.nn.silu(gz)

    qf = (rmsnorm(cq, g_qa) @ w_qb).reshape(bsz, t, MLA_HEADS, MLA_NOPE + MLA_ROPE)
    qn = qf[..., :MLA_NOPE]
    qr = rope(qf[..., MLA_NOPE:], pos)
    ckv = rmsnorm(ckv_raw, g_kva)
    krope = rope(kr_raw[:, :, None, :], pos)[:, :, 0, :]
    if past_ckv is None:
        all_ckv, all_kr, kpos = ckv, krope, pos
    else:
        all_ckv = jnp.concatenate([past_ckv.astype(ckv.dtype), ckv], axis=1)
        all_kr = jnp.concatenate([past_krope.astype(krope.dtype), krope], axis=1)
        kpos = jnp.concatenate([past_pos, pos])
    kv = (all_ckv @ w_kvb).reshape(bsz, all_ckv.shape[1], MLA_HEADS, MLA_NOPE + MLA_DV)
    kn, vv = kv[..., :MLA_NOPE], kv[..., MLA_NOPE:]
    if blocked:
        nb = t // Q_BLOCK

        def blk(a):
            return a.reshape(bsz, nb, Q_BLOCK, *a.shape[2:]).swapaxes(0, 1)

        o_mla = lax.map(lambda xs: mla_attend_rows(xs[0], xs[1], xs[2], kn, all_kr, vv, kpos),
                        (blk(qn), blk(qr), pos.reshape(nb, Q_BLOCK)))
        o_mla = o_mla.swapaxes(0, 1).reshape(bsz, t, MLA_HEADS, MLA_DV)
    else:
        o_mla = mla_attend_rows(qn, qr, pos, kn, all_kr, vv, kpos)
    o_mla = o_mla.reshape(bsz, t, MLA_WIDTH) * jax.nn.silu(mz)

    y = jnp.concatenate([o_gla, o_mla], axis=-1) @ w_out
    return y, s_fin, ckv, krope


def mem_keys_values(mem, g_mem_kv, w_mk, w_mv):
    bsz, n, _ = mem.shape
    m = rmsnorm(mem, g_mem_kv)
    return ((m @ w_mk).reshape(bsz, n, MEM_HEADS, MEM_DH),
            (m @ w_mv).reshape(bsz, n, MEM_HEADS, MEM_DH))


def mem_cross_attn(h, mk, mv, w_mq, w_mo):
    bsz, t, _ = h.shape
    q = (h @ w_mq).reshape(bsz, t, MEM_HEADS, MEM_DH)
    s = jnp.einsum('bqhd,bkhd->bhqk', q, mk.astype(q.dtype)).astype(jnp.float32) * (MEM_DH ** -0.5)
    p = jax.nn.softmax(s, axis=-1).astype(q.dtype)
    o = jnp.einsum('bhqk,bkhd->bqhd', p, mv.astype(q.dtype)).reshape(bsz, t, MEM_HEADS * MEM_DH)
    return o @ w_mo


def setup_inputs(seed: int = 0) -> dict:
    key = jax.random.key(seed)
    ks = jax.random.split(key, 32)
    f32 = jnp.float32

    def nrm(k, shape, scale):
        return jax.random.normal(k, shape, f32) * scale

    def gain(k, shape):
        return 1.0 + 0.01 * jax.random.normal(k, shape, f32)

    return {
        'x_prompt': nrm(ks[0], (BATCH, SEQ, D_MODEL), 1.0),
        'x_sample': nrm(ks[1], (DEC_BATCH, DEC_SEQ, D_MODEL), 1.0),
        'mem_prompt': nrm(ks[2], (BATCH, N_MEM, D_MODEL), 1.0),
        'cache_mla_ckv': nrm(ks[3], (DEPTH, DEC_BATCH, PAST_LEN, MLA_KV_RANK), 1.0),
        'cache_mla_krope': nrm(ks[4], (DEPTH, DEC_BATCH, PAST_LEN, MLA_ROPE), 1.0),
        'state_gla': nrm(ks[5], (DEPTH, DEC_BATCH, GLA_HEADS, GLA_DK, GLA_DV), 0.5),
        'cache_mem_k': nrm(ks[6], (DEPTH, DEC_BATCH, N_MEM, MEM_HEADS, MEM_DH), 1.0),
        'cache_mem_v': nrm(ks[7], (DEPTH, DEC_BATCH, N_MEM, MEM_HEADS, MEM_DH), 1.0),
        'g_mix': gain(ks[8], (DEPTH, D_MODEL)),
        'w_in': nrm(ks[9], (DEPTH, D_MODEL, D_IN), D_MODEL ** -0.5),
        'w_gla_g2': nrm(ks[10], (DEPTH, GLA_GATE_RANK, GLA_HEADS * GLA_DK), GLA_GATE_RANK ** -0.5),
        'b_gla_g': nrm(ks[11], (DEPTH, GLA_HEADS * GLA_DK), 0.1),
        'g_gla_out': gain(ks[12], (DEPTH, GLA_DV)),
        'g_qa': gain(ks[13], (DEPTH, MLA_Q_RANK)),
        'w_qb': nrm(ks[14], (DEPTH, MLA_Q_RANK, MLA_HEADS * (MLA_NOPE + MLA_ROPE)), MLA_Q_RANK ** -0.5),
        'g_kva': gain(ks[15], (DEPTH, MLA_KV_RANK)),
        'w_kvb': nrm(ks[16], (DEPTH, MLA_KV_RANK, MLA_HEADS * (MLA_NOPE + MLA_DV)), MLA_KV_RANK ** -0.5),
        'w_out': nrm(ks[17], (DEPTH, D_MIX, D_MODEL), D_MIX ** -0.5),
        'g_mem_q': gain(ks[18], (DEPTH, D_MODEL)),
        'g_mem_kv': gain(ks[19], (DEPTH, D_MODEL)),
        'w_mq': nrm(ks[20], (DEPTH, D_MODEL, MEM_HEADS * MEM_DH), D_MODEL ** -0.5),
        'w_mk': nrm(ks[21], (DEPTH, D_MODEL, MEM_HEADS * MEM_DH), D_MODEL ** -0.5),
        'w_mv': nrm(ks[22], (DEPTH, D_MODEL, MEM_HEADS * MEM_DH), D_MODEL ** -0.5),
        'w_mo': nrm(ks[23], (DEPTH, MEM_HEADS * MEM_DH, D_MODEL), (MEM_HEADS * MEM_DH) ** -0.5),
        'g_final': gain(ks[24], (D_MODEL,)),
    }


def reference(x_prompt, x_sample, mem_prompt, cache_mla_ckv, cache_mla_krope, state_gla,
              cache_mem_k, cache_mem_v, g_mix, w_in, w_gla_g2, b_gla_g, g_gla_out, g_qa, w_qb,
              g_kva, w_kvb, w_out, g_mem_q, g_mem_kv, w_mq, w_mk, w_mv, w_mo, g_final):
    bp, tp, _ = x_prompt.shape
    ts = x_sample.shape[1]
    pos_p = jnp.arange(tp, dtype=jnp.int32)
    past = cache_mla_ckv.shape[2]
    past_pos = jnp.arange(past, dtype=jnp.int32)
    pos_s = past + jnp.arange(ts, dtype=jnp.int32)

    xp, xs = x_prompt, x_sample
    p_ckv, p_kr, p_st, p_mk, p_mv, s_ckv, s_kr, s_st = [], [], [], [], [], [], [], []
    for l in range(DEPTH):
        s0 = jnp.zeros((bp, GLA_HEADS, GLA_DK, GLA_DV), jnp.float32)
        yp, stp, ckvp, krp = mixer(rmsnorm(xp, g_mix[l]), pos_p, s0, None, None, None,
                                   w_in[l], w_gla_g2[l], b_gla_g[l], g_gla_out[l], g_qa[l], w_qb[l],
                                   g_kva[l], w_kvb[l], w_out[l], CHUNK, True)
        xp = xp + yp
        ys, sts, ckvs, krs = mixer(rmsnorm(xs, g_mix[l]), pos_s, state_gla[l], cache_mla_ckv[l],
                                   cache_mla_krope[l], past_pos,
                                   w_in[l], w_gla_g2[l], b_gla_g[l], g_gla_out[l], g_qa[l], w_qb[l],
                                   g_kva[l], w_kvb[l], w_out[l], ts, False)
        xs = xs + ys
        mkp, mvp = mem_keys_values(mem_prompt, g_mem_kv[l], w_mk[l], w_mv[l])
        xp = xp + mem_cross_attn(rmsnorm(xp, g_mem_q[l]), mkp, mvp, w_mq[l], w_mo[l])
        xs = xs + mem_cross_attn(rmsnorm(xs, g_mem_q[l]), cache_mem_k[l], cache_mem_v[l], w_mq[l], w_mo[l])

        p_ckv.append(ckvp)
        p_kr.append(krp)
        p_st.append(stp.astype(x_prompt.dtype))
        p_mk.append(mkp)
        p_mv.append(mvp)
        s_ckv.append(ckvs)
        s_kr.append(krs)
        s_st.append(sts.astype(x_sample.dtype))

    y_prompt = rmsnorm(xp, g_final)
    y_sample = rmsnorm(xs, g_final)
    return (y_prompt, y_sample, jnp.stack(p_ckv), jnp.stack(p_kr), jnp.stack(p_st),
            jnp.stack(p_mk), jnp.stack(p_mv), jnp.stack(s_ckv), jnp.stack(s_kr), jnp.stack(s_st))
```

```python
import functools

import jax
import jax.numpy as jnp
import numpy as np
from jax import lax
from jax.experimental import pallas as pl
from jax.experimental.pallas import tpu as pltpu

D_MODEL = 1024
CHUNK = 64
EPS = 1e-6
NEG_INF = -1e30

GLA_HEADS = 4
GLA_DK = 64
GLA_DV = 128
GLA_GATE_RANK = 16
GLA_GATE_NORM = 16.0
GLA_KW = GLA_HEADS * GLA_DK
GLA_WIDTH = GLA_HEADS * GLA_DV

MLA_HEADS = 4
MLA_Q_RANK = 256
MLA_KV_RANK = 128
MLA_NOPE = 64
MLA_ROPE = 32
MLA_DV = 128
MLA_WIDTH = MLA_HEADS * MLA_DV
MLA_SCALE = (MLA_NOPE + MLA_ROPE) ** -0.5
ROPE_THETA = 10000.0
HALF_ROPE = MLA_ROPE // 2

MEM_HEADS = 4
MEM_DH = D_MODEL // MEM_HEADS
N_MEM = 256

LANES = 128
HEAD_PAD = 128
ROPE_LANE0 = MLA_NOPE

COL_Q = 0
COL_K = COL_Q + GLA_KW
COL_V = COL_K + GLA_KW
COL_Z = COL_V + GLA_WIDTH
COL_CQ = COL_Z + GLA_WIDTH
COL_CKV = COL_CQ + MLA_Q_RANK
COL_MZ = COL_CKV + MLA_KV_RANK
COL_MISC = COL_MZ + MLA_WIDTH
D_IN_PAD = COL_MISC + LANES

VMEM_LIMIT = 52 * 1024 * 1024

BF16 = jnp.bfloat16
F32 = jnp.float32


def _rms(x, g):
    ms = jnp.mean(x * x, axis=-1, keepdims=True)
    return x * lax.rsqrt(ms + EPS) * g


def _dot(a, b):
    return jnp.dot(a, b, preferred_element_type=F32)


def _dot_nt(a, b):
    return lax.dot_general(a, b, (((1,), (1,)), ((), ())), preferred_element_type=F32)


def _dot_tn(a, b):
    return lax.dot_general(a, b, (((0,), (0,)), ((), ())), preferred_element_type=F32)


def _rope_group(x, cos, sin):
    lane = lax.broadcasted_iota(jnp.int32, x.shape, 1)
    partner = jnp.where(lane < ROPE_LANE0 + HALF_ROPE,
                        pltpu.roll(x, LANES - HALF_ROPE, 1),
                        pltpu.roll(x, HALF_ROPE, 1))
    return x * cos + partner * sin


def _mem_kv_body(mem_ref, g_ref, wk_ref, wv_ref, mk_ref, mv_ref):
    m = _rms(mem_ref[...], g_ref[...]).astype(BF16)
    mk_ref[...] = _dot(m, wk_ref[...])
    mv_ref[...] = _dot(m, wv_ref[...])


def _mem_kv(mem2d, g, wk, wv, tm):
    rows = mem2d.shape[0]
    wspec = pl.BlockSpec((D_MODEL, D_MODEL), lambda i: (0, 0))
    ospec = pl.BlockSpec((tm, D_MODEL), lambda i: (i, 0))
    return pl.pallas_call(
        _mem_kv_body,
        grid=(rows // tm,),
        in_specs=[pl.BlockSpec((tm, D_MODEL), lambda i: (i, 0)),
                  pl.BlockSpec((1, D_MODEL), lambda i: (0, 0)), wspec, wspec],
        out_specs=[ospec, ospec],
        out_shape=[jax.ShapeDtypeStruct((rows, D_MODEL), F32)] * 2,
        compiler_params=pltpu.CompilerParams(dimension_semantics=("arbitrary",),
                                             vmem_limit_bytes=VMEM_LIMIT),
        name="mem_kv",
    )(mem2d, g, wk, wv)


def _in_gla_body(x_ref, s0_ref, cos_ref, sin_ref, amask_ref, gmix_ref, win_ref, wg2_ref, bg_ref,
                 ggla_ref, gqa_ref, wqb_ref, gkva_ref, wk_ref, wv_ref,
                 ogla_ref, mz_ref, q_ref, k_ref, v_ref, ckv_ref, kr_ref, sfin_ref,
                 proj_ref, la_ref, st_ref, *, carry):
    tm = x_ref.shape[0]
    t = pl.program_id(1)

    h = _rms(x_ref[...], gmix_ref[...]).astype(BF16)
    proj_ref[...] = _dot(h, win_ref[...])

    cos = cos_ref[...]
    sin = sin_ref[...]
    lane = lax.broadcasted_iota(jnp.int32, (tm, LANES), 1)

    cqn = _rms(proj_ref[:, COL_CQ:COL_CQ + MLA_Q_RANK], gqa_ref[...]).astype(BF16)
    qf = _dot(cqn, wqb_ref[...])
    for hh in range(MLA_HEADS):
        qh = _rope_group(qf[:, hh * HEAD_PAD:(hh + 1) * HEAD_PAD], cos, sin)
        q_ref[hh] = (qh * MLA_SCALE).astype(BF16)

    ckv = _rms(proj_ref[:, COL_CKV:COL_CKV + MLA_KV_RANK], gkva_ref[...])
    ckv_ref[...] = ckv
    ckv_b = ckv.astype(BF16)
    misc = proj_ref[:, COL_MISC:COL_MISC + LANES]
    kr = jnp.where(lane >= ROPE_LANE0, _rope_group(misc, cos, sin), 0.0)
    kr_ref[...] = kr[:, ROPE_LANE0:ROPE_LANE0 + MLA_ROPE]
    kn = _dot(ckv_b, wk_ref[...])
    vv = _dot(ckv_b, wv_ref[...])
    for hh in range(MLA_HEADS):
        k_ref[hh] = (kn[:, hh * HEAD_PAD:(hh + 1) * HEAD_PAD] + kr).astype(BF16)
        v_ref[hh] = vv[:, hh * MLA_DV:(hh + 1) * MLA_DV].astype(BF16)

    mz = proj_ref[:, COL_MZ:COL_MZ + MLA_WIDTH]
    mz_ref[...] = (mz * jax.nn.sigmoid(mz)).astype(BF16)

    z = _dot(misc.astype(BF16), wg2_ref[...]) + bg_ref[...]
    la_ref[...] = (jnp.minimum(z, 0.0) - jnp.log1p(jnp.exp(-jnp.abs(z)))) * (1.0 / GLA_GATE_NORM)

    if carry:
        @pl.when(t == 0)
        def _():
            st_ref[...] = s0_ref[0].T

    klane = lax.broadcasted_iota(jnp.int32, (CHUNK, GLA_KW), 1)
    head_of_lane = klane // GLA_DK
    ri = lax.broadcasted_iota(jnp.int32, (CHUNK, CHUNK), 0)
    ci = lax.broadcasted_iota(jnp.int32, (CHUNK, CHUNK), 1)
    tril = (ci <= ri).astype(BF16)
    tril3 = jnp.concatenate([tril, tril, tril], axis=1)
    amask = amask_ref[...] != 0.0
    g_out = ggla_ref[...]

    def chunk(c, _):
        r0 = pl.multiple_of(c * CHUNK, CHUNK)
        rows = pl.ds(r0, CHUNK)
        if not carry:
            st_ref[...] = s0_ref[c].T
        q = proj_ref[rows, COL_Q:COL_Q + GLA_KW] * (GLA_DK ** -0.5)
        k = proj_ref[rows, COL_K:COL_K + GLA_KW]
        v = proj_ref[rows, COL_V:COL_V + GLA_WIDTH]
        la = la_ref[rows, :]
        la_hi = la.astype(BF16)
        r1 = la - la_hi.astype(F32)
        la_mid = r1.astype(BF16)
        la_lo = (r1 - la_mid.astype(F32)).astype(BF16)
        b = _dot(tril3, jnp.concatenate([la_hi, la_mid, la_lo], axis=0))
        b_last = b[CHUNK - 1:CHUNK, :]
        q_dec = q * jnp.exp(b)
        k_dec = k * jnp.exp(-b)
        kd2 = k * jnp.exp(b_last - b)
        sdec = jnp.exp(b_last)

        def per_head(a):
            return jnp.concatenate(
                [jnp.where(head_of_lane == hh, a, 0.0) for hh in range(GLA_HEADS)], axis=0)

        qs = per_head(q_dec).astype(BF16)
        kt = jnp.concatenate([k_dec.astype(BF16)] * GLA_HEADS, axis=0)
        a = jnp.where(amask, _dot_nt(qs, kt), 0.0).astype(BF16)
        vs = jnp.concatenate([v[:, hh * GLA_DV:(hh + 1) * GLA_DV] for hh in range(GLA_HEADS)],
                             axis=0).astype(BF16)
        st = st_ref[...]
        o = _dot_nt(qs, st.astype(BF16)) + _dot(a, vs)
        st_ref[...] = st * sdec + _dot_tn(vs, per_head(kd2).astype(BF16))
        gz = proj_ref[rows, COL_Z:COL_Z + GLA_WIDTH]
        gate = gz * jax.nn.sigmoid(gz)
        on = _rms(o, g_out)
        for hh in range(GLA_HEADS):
            cols = slice(hh * GLA_DV, (hh + 1) * GLA_DV)
            ogla_ref[rows, cols] = (on[hh * CHUNK:(hh + 1) * CHUNK, :] * gate[:, cols]).astype(BF16)
        if not carry:
            sfin_ref[c] = st_ref[...].T
        return 0

    lax.fori_loop(0, tm // CHUNK, chunk, 0)

    if carry:
        @pl.when(t == pl.num_programs(1) - 1)
        def _():
            sfin_ref[0] = st_ref[...].T


def _in_gla(x, s0, cos, sin, amask, wts, *, tm, carry):
    g, tlen, _ = x.shape
    nt = tlen // tm
    ns = 1 if carry else tm // CHUNK
    smap = (lambda b, t: (b, 0, 0)) if carry else (lambda b, t: (t, 0, 0))
    tab_tiles = cos.shape[0] // tm
    tmap = (lambda b, t: (t, 0)) if tab_tiles > 1 else (lambda b, t: (0, 0))

    def const(shape):
        return pl.BlockSpec(shape, lambda b, t: (0,) * len(shape))

    row = lambda w: pl.BlockSpec((None, tm, w), lambda b, t: (b, t, 0))
    heads = pl.BlockSpec((None, MLA_HEADS, tm, HEAD_PAD), lambda b, t: (b, 0, t, 0))
    in_specs = [
        row(D_MODEL),
        pl.BlockSpec((ns, GLA_KW, GLA_DV), smap),
        pl.BlockSpec((tm, LANES), tmap), pl.BlockSpec((tm, LANES), tmap),
        const((GLA_HEADS * CHUNK, GLA_HEADS * CHUNK)),
        const((1, D_MODEL)), const((D_MODEL, D_IN_PAD)), const((LANES, GLA_KW)), const((1, GLA_KW)),
        const((1, GLA_DV)), const((1, MLA_Q_RANK)), const((MLA_Q_RANK, MLA_HEADS * HEAD_PAD)),
        const((1, MLA_KV_RANK)), const((MLA_KV_RANK, MLA_HEADS * HEAD_PAD)),
        const((MLA_KV_RANK, MLA_WIDTH)),
    ]
    out_specs = [row(GLA_WIDTH), row(MLA_WIDTH), heads, heads, heads, row(MLA_KV_RANK),
                 row(MLA_ROPE), pl.BlockSpec((ns, GLA_KW, GLA_DV), smap)]
    hshape = jax.ShapeDtypeStruct((g, MLA_HEADS, tlen, HEAD_PAD), BF16)
    out_shape = [
        jax.ShapeDtypeStruct((g, tlen, GLA_WIDTH), BF16),
        jax.ShapeDtypeStruct((g, tlen, MLA_WIDTH), BF16),
        hshape, hshape, hshape,
        jax.ShapeDtypeStruct((g, tlen, MLA_KV_RANK), F32),
        jax.ShapeDtypeStruct((g, tlen, MLA_ROPE), F32),
        jax.ShapeDtypeStruct(s0.shape, F32),
    ]
    return pl.pallas_call(
        functools.partial(_in_gla_body, carry=carry),
        grid=(g, nt),
        in_specs=in_specs,
        out_specs=out_specs,
        out_shape=out_shape,
        scratch_shapes=[pltpu.VMEM((tm, D_IN_PAD), F32), pltpu.VMEM((tm, GLA_KW), F32),
                        pltpu.VMEM((GLA_DV, GLA_KW), F32)],
        compiler_params=pltpu.CompilerParams(dimension_semantics=("arbitrary", "arbitrary"),
                                             vmem_limit_bytes=VMEM_LIMIT),
        name="in_gla_carry" if carry else "in_gla_batched",
    )(x, s0, cos, sin, amask, *wts)


def _past_kv_body(ckv_ref, kr_ref, wk_ref, wv_ref, sel_ref, k_ref, v_ref):
    ckv_b = ckv_ref[...].astype(BF16)
    kr = _dot(kr_ref[...].astype(BF16), sel_ref[...])
    kn = _dot(ckv_b, wk_ref[...])
    vv = _dot(ckv_b, wv_ref[...])
    for hh in range(MLA_HEADS):
        k_ref[hh] = (kn[:, hh * HEAD_PAD:(hh + 1) * HEAD_PAD] + kr).astype(BF16)
        v_ref[hh] = vv[:, hh * MLA_DV:(hh + 1) * MLA_DV].astype(BF16)


def _past_kv(ckv, kr, wk, wv, sel):
    bsz, past, _ = ckv.shape
    hspec = pl.BlockSpec((MLA_HEADS, None, past, HEAD_PAD), lambda b: (0, b, 0, 0))
    hshape = jax.ShapeDtypeStruct((MLA_HEADS, bsz, past, HEAD_PAD), BF16)
    return pl.pallas_call(
        _past_kv_body,
        grid=(bsz,),
        in_specs=[pl.BlockSpec((None, past, MLA_KV_RANK), lambda b: (b, 0, 0)),
                  pl.BlockSpec((None, past, MLA_ROPE), lambda b: (b, 0, 0)),
                  pl.BlockSpec(wk.shape, lambda b: (0, 0)),
                  pl.BlockSpec(wv.shape, lambda b: (0, 0)),
                  pl.BlockSpec(sel.shape, lambda b: (0, 0))],
        out_specs=[hspec, hspec],
        out_shape=[hshape, hshape],
        compiler_params=pltpu.CompilerParams(dimension_semantics=("arbitrary",),
                                             vmem_limit_bytes=VMEM_LIMIT),
        name="past_kv",
    )(ckv, kr, wk, wv, sel)


def _attn_body(q_ref, k_ref, v_ref, mz_ref, o_ref, m_ref, l_ref, acc_ref, *, tk, q0):
    tq = q_ref.shape[0]
    i = pl.program_id(2)
    qstart = q0 + i * tq
    n_full = qstart // tk
    n_kv = (qstart + tq + tk - 1) // tk
    q = q_ref[...]
    m_ref[...] = jnp.full(m_ref.shape, NEG_INF, F32)
    l_ref[...] = jnp.zeros(l_ref.shape, F32)
    acc_ref[...] = jnp.zeros(acc_ref.shape, F32)

    def make_step(masked):
        def step(j, _):
            k0 = pl.multiple_of(j * tk, tk)
            s = _dot_nt(q, k_ref[pl.ds(k0, tk), :])
            if masked:
                qchunk = (qstart + lax.broadcasted_iota(jnp.int32, (tq, tk), 0)) // CHUNK
                kchunk = (k0 + lax.broadcasted_iota(jnp.int32, (tq, tk), 1)) // CHUNK
                s = jnp.where(kchunk <= qchunk, s, NEG_INF)
            m_old = m_ref[...]
            m_new = jnp.maximum(m_old, jnp.max(s, axis=1, keepdims=True))
            alpha = jnp.exp(m_old - m_new)
            p = jnp.exp(s - m_new)
            l_ref[...] = alpha * l_ref[...] + jnp.sum(p, axis=1, keepdims=True)
            acc_ref[...] = alpha * acc_ref[...] + _dot(p.astype(BF16), v_ref[pl.ds(k0, tk), :])
            m_ref[...] = m_new
            return 0
        return step

    lax.fori_loop(0, n_full, make_step(False), 0)
    lax.fori_loop(n_full, n_kv, make_step(True), 0)
    o = acc_ref[...] / l_ref[...]
    o_ref[...] = (o * mz_ref[...].astype(F32)).astype(BF16)


def _attention(q, k, v, mz, *, grid, qmap, kmap, omap, tq, tk, q0):
    tkeys = k.shape[2]
    kspec = pl.BlockSpec((None, None, tkeys, HEAD_PAD), kmap)
    return pl.pallas_call(
        functools.partial(_attn_body, tk=tk, q0=q0),
        grid=grid,
        in_specs=[pl.BlockSpec((None, None, tq, HEAD_PAD), qmap), kspec, kspec,
                  pl.BlockSpec((None, tq, MLA_DV), omap)],
        out_specs=pl.BlockSpec((None, tq, MLA_DV), omap),
        out_shape=jax.ShapeDtypeStruct(mz.shape, BF16),
        scratch_shapes=[pltpu.VMEM((tq, 1), F32), pltpu.VMEM((tq, 1), F32),
                        pltpu.VMEM((tq, MLA_DV), F32)],
        compiler_params=pltpu.CompilerParams(
            dimension_semantics=("arbitrary", "arbitrary", "arbitrary"),
            vmem_limit_bytes=VMEM_LIMIT),
        name="mla_attn",
    )(q, k, v, mz)


def _out_mem_body(x_ref, ogla_ref, omla_ref, mk_ref, mv_ref, wout_ref, gq_ref, wmq_ref, wmo_ref,
                  gfin_ref, y_ref, o_scr):
    nb = mk_ref.shape[0]
    rb = x_ref.shape[0] // nb
    x1 = (x_ref[...] + _dot(ogla_ref[...], wout_ref[0:GLA_WIDTH, :])
          + _dot(omla_ref[...], wout_ref[GLA_WIDTH:GLA_WIDTH + MLA_WIDTH, :]))
    h = _rms(x1, gq_ref[...]).astype(BF16)
    qm = (_dot(h, wmq_ref[...]) * (MEM_DH ** -0.5)).astype(BF16)
    for bb in range(nb):
        rows = slice(bb * rb, (bb + 1) * rb)
        for hh in range(MEM_HEADS):
            cols = slice(hh * MEM_DH, (hh + 1) * MEM_DH)
            s = _dot_nt(qm[rows, cols], mk_ref[bb, :, cols].astype(BF16))
            p = jnp.exp(s - jnp.max(s, axis=1, keepdims=True))
            p = p / jnp.sum(p, axis=1, keepdims=True)
            o_scr[rows, cols] = _dot(p.astype(BF16), mv_ref[bb, :, cols].astype(BF16)).astype(BF16)
    x2 = x1 + _dot(o_scr[...], wmo_ref[...])
    y_ref[...] = _rms(x2, gfin_ref[...])


def _out_mem(x, ogla, omla, mk, mv, wts, *, tm, nb):
    g, tlen, _ = x.shape
    nt = tlen // tm
    mmap = (lambda b, t: (b, 0, 0)) if nb == 1 else (lambda b, t: (t, 0, 0))
    row = lambda w: pl.BlockSpec((None, tm, w), lambda b, t: (b, t, 0))
    mspec = pl.BlockSpec((nb, N_MEM, D_MODEL), mmap)
    sq = pl.BlockSpec((D_MODEL, D_MODEL), lambda b, t: (0, 0))
    vec = pl.BlockSpec((1, D_MODEL), lambda b, t: (0, 0))
    return pl.pallas_call(
        _out_mem_body,
        grid=(g, nt),
        in_specs=[row(D_MODEL), row(GLA_WIDTH), row(MLA_WIDTH), mspec, mspec, sq, vec, sq, sq, vec],
        out_specs=row(D_MODEL),
        out_shape=jax.ShapeDtypeStruct(x.shape, F32),
        scratch_shapes=[pltpu.VMEM((tm, D_MODEL), BF16)],
        compiler_params=pltpu.CompilerParams(dimension_semantics=("arbitrary", "arbitrary"),
                                             vmem_limit_bytes=VMEM_LIMIT),
        name="out_mem",
    )(x, ogla, omla, mk, mv, *wts)


def _prep_w_in(w_in):
    splits = np.cumsum([GLA_KW, GLA_KW, GLA_WIDTH, GLA_GATE_RANK, GLA_WIDTH, MLA_Q_RANK,
                        MLA_KV_RANK, MLA_ROPE])
    gq, gk, gv, glr, gz, cq, ckv, kr, mz = jnp.split(w_in, splits.tolist(), axis=1)
    zeros = lambda n: jnp.zeros((D_MODEL, n), w_in.dtype)
    misc = jnp.concatenate([glr, zeros(ROPE_LANE0 - GLA_GATE_RANK), kr,
                            zeros(LANES - ROPE_LANE0 - MLA_ROPE)], axis=1)
    return jnp.concatenate([gq, gk, gv, gz, cq, ckv, mz, misc], axis=1).astype(BF16)


def _pad_heads(w, width):
    kdim = w.shape[0]
    w = w.reshape(kdim, MLA_HEADS, width)
    w = jnp.pad(w, ((0, 0), (0, 0), (0, HEAD_PAD - width)))
    return w.reshape(kdim, MLA_HEADS * HEAD_PAD)


def _rope_tables(pos):
    inv = ROPE_THETA ** (-jnp.arange(HALF_ROPE, dtype=F32) / HALF_ROPE)
    ang = pos.astype(F32)[:, None] * inv[None, :]
    c, s = jnp.cos(ang), jnp.sin(ang)
    n = pos.shape[0]
    cos = jnp.concatenate([jnp.ones((n, ROPE_LANE0), F32), c, c,
                           jnp.zeros((n, LANES - ROPE_LANE0 - MLA_ROPE), F32)], axis=1)
    sin = jnp.concatenate([jnp.zeros((n, ROPE_LANE0), F32), -s, s,
                           jnp.zeros((n, LANES - ROPE_LANE0 - MLA_ROPE), F32)], axis=1)
    return cos, sin


def _gla_mask():
    r = np.arange(GLA_HEADS * CHUNK)
    same_head = (r[:, None] // CHUNK) == (r[None, :] // CHUNK)
    causal = (r[None, :] % CHUNK) <= (r[:, None] % CHUNK)
    return jnp.asarray((same_head & causal).astype(np.float32))


def kernel(x_prompt, x_sample, mem_prompt, cache_mla_ckv, cache_mla_krope, state_gla, cache_mem_k, cache_mem_v, g_mix, w_in, w_gla_g2, b_gla_g, g_gla_out, g_qa, w_qb, g_kva, w_kvb, w_out, g_mem_q, g_mem_kv, w_mq, w_mk, w_mv, w_mo, g_final):
    bp, tp, _ = x_prompt.shape
    bs, ts, _ = x_sample.shape
    past = cache_mla_ckv.shape[2]
    assert ts == CHUNK and g_mix.shape[0] == 1
    l = 0
    row = lambda g: g.reshape(1, -1)

    w_in_p = _prep_w_in(w_in[l])
    wg2_p = jnp.pad(w_gla_g2[l], ((0, LANES - GLA_GATE_RANK), (0, 0))).astype(BF16)
    wqb_p = _pad_heads(w_qb[l], MLA_NOPE + MLA_ROPE).astype(BF16)
    wkv = w_kvb[l].reshape(MLA_KV_RANK, MLA_HEADS, MLA_NOPE + MLA_DV)
    wk_p = _pad_heads(wkv[:, :, :MLA_NOPE].reshape(MLA_KV_RANK, -1), MLA_NOPE).astype(BF16)
    wv_p = wkv[:, :, MLA_NOPE:].reshape(MLA_KV_RANK, MLA_WIDTH).astype(BF16)
    in_wts = (row(g_mix[l]), w_in_p, wg2_p, row(b_gla_g[l]), row(g_gla_out[l]), row(g_qa[l]),
              wqb_p, row(g_kva[l]), wk_p, wv_p)
    out_wts = (w_out[l].astype(BF16), row(g_mem_q[l]), w_mq[l].astype(BF16),
               w_mo[l].astype(BF16), row(g_final))
    amask = _gla_mask()
    sel = jnp.zeros((MLA_ROPE, HEAD_PAD), BF16).at[
        jnp.arange(MLA_ROPE), ROPE_LANE0 + jnp.arange(MLA_ROPE)].set(1.0)

    tm = 512
    sample_rows = bs * ts
    bpt = tm // ts

    mk_p, mv_p = _mem_kv(mem_prompt.reshape(bp * N_MEM, D_MODEL), row(g_mem_kv[l]),
                         w_mk[l].astype(BF16), w_mv[l].astype(BF16), tm)
    cos_p, sin_p = _rope_tables(jnp.arange(tp, dtype=jnp.int32))
    s0_p = jnp.zeros((bp, GLA_KW, GLA_DV), F32)
    ogla_p, mz_p, q_p, k_p, v_p, ckv_p, kr_p, st_p = _in_gla(
        x_prompt, s0_p, cos_p, sin_p, amask, in_wts, tm=tm, carry=True)
    omla_p = _attention(
        q_p, k_p, v_p, mz_p, grid=(bp, MLA_HEADS, tp // tm),
        qmap=lambda b, h, i: (b, h, i, 0), kmap=lambda b, h, i: (b, h, 0, 0),
        omap=lambda b, h, i: (b, i, h), tq=tm, tk=tm, q0=0)
    y_prompt = _out_mem(x_prompt, ogla_p, omla_p, mk_p.reshape(bp, N_MEM, D_MODEL),
                        mv_p.reshape(bp, N_MEM, D_MODEL), out_wts, tm=tm, nb=1)

    xs = x_sample.reshape(1, sample_rows, D_MODEL)
    cos_s, sin_s = _rope_tables(past + jnp.arange(ts, dtype=jnp.int32))
    cos_s, sin_s = jnp.tile(cos_s, (bpt, 1)), jnp.tile(sin_s, (bpt, 1))
    s0_s = state_gla[l].reshape(bs, GLA_KW, GLA_DV)
    ogla_s, mz_s, q_s, k_s, v_s, ckv_s, kr_s, st_s = _in_gla(
        xs, s0_s, cos_s, sin_s, amask, in_wts, tm=tm, carry=False)
    k_past, v_past = _past_kv(cache_mla_ckv[l], cache_mla_krope[l], wk_p, wv_p, sel)
    nkeys = past + ts
    kpad = -nkeys % LANES
    per_batch = lambda a: a.reshape(MLA_HEADS, bs, ts, HEAD_PAD)
    tail = jnp.zeros((MLA_HEADS, bs, kpad, HEAD_PAD), BF16)
    k_all = jnp.concatenate([k_past, per_batch(k_s), tail], axis=2)
    v_all = jnp.concatenate([v_past, per_batch(v_s), tail], axis=2)
    omla_s = _attention(
        per_batch(q_s), k_all, v_all, mz_s, grid=(bs, MLA_HEADS, 1),
        qmap=lambda b, h, i: (h, b, 0, 0), kmap=lambda b, h, i: (h, b, 0, 0),
        omap=lambda b, h, i: (0, b, h), tq=ts, tk=nkeys + kpad, q0=past)
    nb_s = 4
    y_sample = _out_mem(xs, ogla_s, omla_s, cache_mem_k[l].reshape(bs, N_MEM, D_MODEL),
                        cache_mem_v[l].reshape(bs, N_MEM, D_MODEL), out_wts,
                        tm=nb_s * ts, nb=nb_s)

    mem_shape = (1, bp, N_MEM, MEM_HEADS, MEM_DH)
    st_shape = (GLA_HEADS, GLA_DK, GLA_DV)
    return (y_prompt, y_sample.reshape(bs, ts, D_MODEL),
            ckv_p[None], kr_p[None], st_p.reshape(1, bp, *st_shape),
            mk_p.reshape(mem_shape), mv_p.reshape(mem_shape),
            ckv_s.reshape(1, bs, ts, MLA_KV_RANK), kr_s.reshape(1, bs, ts, MLA_ROPE),
            st_s.reshape(1, bs, *st_shape))
```

```python
import functools

import jax
import jax.numpy as jnp
import numpy as np
from jax import lax
from jax.experimental import pallas as pl
from jax.experimental.pallas import tpu as pltpu

D_MODEL = 1024
CHUNK = 64
EPS = 1e-6
NEG_INF = -1e30

GLA_HEADS = 4
GLA_DK = 64
GLA_DV = 128
GLA_GATE_RANK = 16
GLA_GATE_NORM = 16.0
GLA_KW = GLA_HEADS * GLA_DK
GLA_WIDTH = GLA_HEADS * GLA_DV

MLA_HEADS = 4
MLA_Q_RANK = 256
MLA_KV_RANK = 128
MLA_NOPE = 64
MLA_ROPE = 32
MLA_DV = 128
MLA_WIDTH = MLA_HEADS * MLA_DV
MLA_SCALE = (MLA_NOPE + MLA_ROPE) ** -0.5
ROPE_THETA = 10000.0
HALF_ROPE = MLA_ROPE // 2
Q_PRESCALE = MLA_SCALE * float(np.log2(np.e))
SUBQ = 256

MEM_HEADS = 4
MEM_DH = D_MODEL // MEM_HEADS
N_MEM = 256

LANES = 128
HEAD_PAD = 128
ROPE_LANE0 = MLA_NOPE

COL_Q = 0
COL_K = COL_Q + GLA_KW
COL_V = COL_K + GLA_KW
COL_Z = COL_V + GLA_WIDTH
COL_CQ = COL_Z + GLA_WIDTH
COL_CKV = COL_CQ + MLA_Q_RANK
COL_MZ = COL_CKV + MLA_KV_RANK
COL_MISC = COL_MZ + MLA_WIDTH
D_IN_PAD = COL_MISC + LANES

VMEM_LIMIT = 52 * 1024 * 1024

BF16 = jnp.bfloat16
F32 = jnp.float32


def _rms(x, g):
    ms = jnp.mean(x * x, axis=-1, keepdims=True)
    return x * lax.rsqrt(ms + EPS) * g


def _dot(a, b):
    return jnp.dot(a, b, preferred_element_type=F32)


def _dot_nt(a, b):
    return lax.dot_general(a, b, (((1,), (1,)), ((), ())), preferred_element_type=F32)


def _dot_tn(a, b):
    return lax.dot_general(a, b, (((0,), (0,)), ((), ())), preferred_element_type=F32)


def _rope_group(x, cos, sin):
    lane = lax.broadcasted_iota(jnp.int32, x.shape, 1)
    partner = jnp.where(lane < ROPE_LANE0 + HALF_ROPE,
                        pltpu.roll(x, LANES - HALF_ROPE, 1),
                        pltpu.roll(x, HALF_ROPE, 1))
    return x * cos + partner * sin


def _mem_kv_body(mem_ref, g_ref, wk_ref, wv_ref, mk_ref, mv_ref):
    m = _rms(mem_ref[...], g_ref[...]).astype(BF16)
    mk_ref[...] = _dot(m, wk_ref[...])
    mv_ref[...] = _dot(m, wv_ref[...])


def _mem_kv(mem2d, g, wk, wv, tm):
    rows = mem2d.shape[0]
    wspec = pl.BlockSpec((D_MODEL, D_MODEL), lambda i: (0, 0))
    ospec = pl.BlockSpec((tm, D_MODEL), lambda i: (i, 0))
    return pl.pallas_call(
        _mem_kv_body,
        grid=(rows // tm,),
        in_specs=[pl.BlockSpec((tm, D_MODEL), lambda i: (i, 0)),
                  pl.BlockSpec((1, D_MODEL), lambda i: (0, 0)), wspec, wspec],
        out_specs=[ospec, ospec],
        out_shape=[jax.ShapeDtypeStruct((rows, D_MODEL), F32)] * 2,
        compiler_params=pltpu.CompilerParams(dimension_semantics=("arbitrary",),
                                             vmem_limit_bytes=VMEM_LIMIT),
        name="mem_kv",
    )(mem2d, g, wk, wv)


def _in_gla_body(x_ref, s0_ref, cos_ref, sin_ref, amask_ref, gmix_ref, win_ref, wg2_ref, bg_ref,
                 ggla_ref, gqa_ref, wqb_ref, gkva_ref, wk_ref, wvt_ref,
                 ogla_ref, mz_ref, q_ref, k_ref, vt_ref, ckv_ref, kr_ref, sfin_ref,
                 proj_ref, la_ref, st_ref, *, carry):
    tm = x_ref.shape[0]
    t = pl.program_id(1)

    h = _rms(x_ref[...], gmix_ref[...]).astype(BF16)
    proj_ref[...] = _dot(h, win_ref[...])

    cos = cos_ref[...]
    sin = sin_ref[...]
    lane = lax.broadcasted_iota(jnp.int32, (tm, LANES), 1)

    cqn = _rms(proj_ref[:, COL_CQ:COL_CQ + MLA_Q_RANK], gqa_ref[...]).astype(BF16)
    qf = _dot(cqn, wqb_ref[...])
    for hh in range(MLA_HEADS):
        qh = _rope_group(qf[:, hh * HEAD_PAD:(hh + 1) * HEAD_PAD], cos, sin)
        q_ref[hh] = (qh * Q_PRESCALE).astype(BF16)

    ckv = _rms(proj_ref[:, COL_CKV:COL_CKV + MLA_KV_RANK], gkva_ref[...])
    ckv_ref[...] = ckv
    ckv_b = ckv.astype(BF16)
    misc = proj_ref[:, COL_MISC:COL_MISC + LANES]
    kr = jnp.where(lane >= ROPE_LANE0, _rope_group(misc, cos, sin), 0.0)
    kr_ref[...] = kr[:, ROPE_LANE0:ROPE_LANE0 + MLA_ROPE]
    kn = _dot(ckv_b, wk_ref[...])
    vt = _dot_nt(wvt_ref[...], ckv_b)
    for hh in range(MLA_HEADS):
        k_ref[hh] = (kn[:, hh * HEAD_PAD:(hh + 1) * HEAD_PAD] + kr).astype(BF16)
        vt_ref[hh] = vt[hh * MLA_DV:(hh + 1) * MLA_DV, :].astype(BF16)

    mz = proj_ref[:, COL_MZ:COL_MZ + MLA_WIDTH]
    mz_ref[...] = (mz * jax.nn.sigmoid(mz)).astype(BF16)

    z = _dot(misc.astype(BF16), wg2_ref[...]) + bg_ref[...]
    la_ref[...] = (jnp.minimum(z, 0.0) - jnp.log1p(jnp.exp(-jnp.abs(z)))) * (1.0 / GLA_GATE_NORM)

    if carry:
        @pl.when(t == 0)
        def _():
            st_ref[...] = s0_ref[0].T

    klane = lax.broadcasted_iota(jnp.int32, (CHUNK, GLA_KW), 1)
    head_of_lane = klane // GLA_DK
    ri = lax.broadcasted_iota(jnp.int32, (CHUNK, CHUNK), 0)
    ci = lax.broadcasted_iota(jnp.int32, (CHUNK, CHUNK), 1)
    tril = (ci <= ri).astype(BF16)
    tril3 = jnp.concatenate([tril, tril, tril], axis=1)
    g_out = ggla_ref[...]

    def chunk(c, _):
        r0 = pl.multiple_of(c * CHUNK, CHUNK)
        rows = pl.ds(r0, CHUNK)
        if not carry:
            st_ref[...] = s0_ref[c].T
        q = proj_ref[rows, COL_Q:COL_Q + GLA_KW] * (GLA_DK ** -0.5)
        k = proj_ref[rows, COL_K:COL_K + GLA_KW]
        v = proj_ref[rows, COL_V:COL_V + GLA_WIDTH]
        la = la_ref[rows, :]
        la_hi = la.astype(BF16)
        r1 = la - la_hi.astype(F32)
        la_mid = r1.astype(BF16)
        la_lo = (r1 - la_mid.astype(F32)).astype(BF16)
        b = _dot(tril3, jnp.concatenate([la_hi, la_mid, la_lo], axis=0))
        b_last = b[CHUNK - 1:CHUNK, :]
        q_dec = q * jnp.exp(b)
        k_dec = k * jnp.exp(-b)
        kd2 = k * jnp.exp(b_last - b)
        sdec = jnp.exp(b_last)

        def per_head(a):
            return jnp.concatenate(
                [jnp.where(head_of_lane == hh, a, 0.0) for hh in range(GLA_HEADS)], axis=0)

        qs = per_head(q_dec).astype(BF16)
        kt = jnp.concatenate([k_dec.astype(BF16)] * GLA_HEADS, axis=0)
        a = jnp.where(amask_ref[...] != 0.0, _dot_nt(qs, kt), 0.0).astype(BF16)
        vs = jnp.concatenate([v[:, hh * GLA_DV:(hh + 1) * GLA_DV] for hh in range(GLA_HEADS)],
                             axis=0).astype(BF16)
        st = st_ref[...]
        o = _dot_nt(qs, st.astype(BF16)) + _dot(a, vs)
        st_ref[...] = st * sdec + _dot_tn(vs, per_head(kd2).astype(BF16))
        gz = proj_ref[rows, COL_Z:COL_Z + GLA_WIDTH]
        gate = gz * jax.nn.sigmoid(gz)
        on = _rms(o, g_out)
        for hh in range(GLA_HEADS):
            cols = slice(hh * GLA_DV, (hh + 1) * GLA_DV)
            ogla_ref[rows, cols] = (on[hh * CHUNK:(hh + 1) * CHUNK, :] * gate[:, cols]).astype(BF16)
        if not carry:
            sfin_ref[c] = st_ref[...].T
        return 0

    lax.fori_loop(0, tm // CHUNK, chunk, 0)

    if carry:
        @pl.when(t == pl.num_programs(1) - 1)
        def _():
            sfin_ref[0] = st_ref[...].T


def _in_gla(x, s0, cos, sin, amask, wts, *, tm, carry):
    g, tlen, _ = x.shape
    nt = tlen // tm
    ns = 1 if carry else tm // CHUNK
    smap = (lambda b, t: (b, 0, 0)) if carry else (lambda b, t: (t, 0, 0))
    tab_tiles = cos.shape[0] // tm
    tmap = (lambda b, t: (t, 0)) if tab_tiles > 1 else (lambda b, t: (0, 0))

    def const(shape):
        return pl.BlockSpec(shape, lambda b, t: (0,) * len(shape))

    row = lambda w: pl.BlockSpec((None, tm, w), lambda b, t: (b, t, 0))
    heads = pl.BlockSpec((None, MLA_HEADS, tm, HEAD_PAD), lambda b, t: (b, 0, t, 0))
    in_specs = [
        row(D_MODEL),
        pl.BlockSpec((ns, GLA_KW, GLA_DV), smap),
        pl.BlockSpec((tm, LANES), tmap), pl.BlockSpec((tm, LANES), tmap),
        const((GLA_HEADS * CHUNK, GLA_HEADS * CHUNK)),
        const((1, D_MODEL)), const((D_MODEL, D_IN_PAD)), const((LANES, GLA_KW)), const((1, GLA_KW)),
        const((1, GLA_DV)), const((1, MLA_Q_RANK)), const((MLA_Q_RANK, MLA_HEADS * HEAD_PAD)),
        const((1, MLA_KV_RANK)), const((MLA_KV_RANK, MLA_HEADS * HEAD_PAD)),
        const((MLA_WIDTH, MLA_KV_RANK)),
    ]
    vt_spec = pl.BlockSpec((None, MLA_HEADS, None, MLA_DV, tm), lambda b, t: (b, 0, t, 0, 0))
    out_specs = [row(GLA_WIDTH), row(MLA_WIDTH), heads, heads, vt_spec, row(MLA_KV_RANK),
                 row(MLA_ROPE), pl.BlockSpec((ns, GLA_KW, GLA_DV), smap)]
    hshape = jax.ShapeDtypeStruct((g, MLA_HEADS, tlen, HEAD_PAD), BF16)
    out_shape = [
        jax.ShapeDtypeStruct((g, tlen, GLA_WIDTH), BF16),
        jax.ShapeDtypeStruct((g, tlen, MLA_WIDTH), BF16),
        hshape, hshape, jax.ShapeDtypeStruct((g, MLA_HEADS, nt, MLA_DV, tm), BF16),
        jax.ShapeDtypeStruct((g, tlen, MLA_KV_RANK), F32),
        jax.ShapeDtypeStruct((g, tlen, MLA_ROPE), F32),
        jax.ShapeDtypeStruct(s0.shape, F32),
    ]
    return pl.pallas_call(
        functools.partial(_in_gla_body, carry=carry),
        grid=(g, nt),
        in_specs=in_specs,
        out_specs=out_specs,
        out_shape=out_shape,
        scratch_shapes=[pltpu.VMEM((tm, D_IN_PAD), F32), pltpu.VMEM((tm, GLA_KW), F32),
                        pltpu.VMEM((GLA_DV, GLA_KW), F32)],
        compiler_params=pltpu.CompilerParams(dimension_semantics=("arbitrary", "arbitrary"),
                                             vmem_limit_bytes=VMEM_LIMIT),
        name="in_gla_carry" if carry else "in_gla_batched",
    )(x, s0, cos, sin, amask, *wts)


def _past_kv_body(ckv_ref, kr_ref, wk_ref, wvt_ref, sel_ref, k_ref, vt_ref):
    ckv_b = ckv_ref[...].astype(BF16)
    kr = _dot(kr_ref[...].astype(BF16), sel_ref[...])
    kn = _dot(ckv_b, wk_ref[...])
    vt = _dot_nt(wvt_ref[...], ckv_b)
    for hh in range(MLA_HEADS):
        k_ref[hh] = (kn[:, hh * HEAD_PAD:(hh + 1) * HEAD_PAD] + kr).astype(BF16)
        vt_ref[hh] = vt[hh * MLA_DV:(hh + 1) * MLA_DV, :].astype(BF16)


def _past_kv(ckv, kr, wk, wvt, sel):
    bsz, past, _ = ckv.shape
    return pl.pallas_call(
        _past_kv_body,
        grid=(bsz,),
        in_specs=[pl.BlockSpec((None, past, MLA_KV_RANK), lambda b: (b, 0, 0)),
                  pl.BlockSpec((None, past, MLA_ROPE), lambda b: (b, 0, 0)),
                  pl.BlockSpec(wk.shape, lambda b: (0, 0)),
                  pl.BlockSpec(wvt.shape, lambda b: (0, 0)),
                  pl.BlockSpec(sel.shape, lambda b: (0, 0))],
        out_specs=[pl.BlockSpec((MLA_HEADS, None, past, HEAD_PAD), lambda b: (0, b, 0, 0)),
                   pl.BlockSpec((MLA_HEADS, None, MLA_DV, past), lambda b: (0, b, 0, 0))],
        out_shape=[jax.ShapeDtypeStruct((MLA_HEADS, bsz, past, HEAD_PAD), BF16),
                   jax.ShapeDtypeStruct((MLA_HEADS, bsz, MLA_DV, past), BF16)],
        compiler_params=pltpu.CompilerParams(dimension_semantics=("arbitrary",),
                                             vmem_limit_bytes=VMEM_LIMIT),
        name="past_kv",
    )(ckv, kr, wk, wvt, sel)


def _attn_body(q_ref, k_ref, vt_ref, mz_ref, o_ref, m_ref, l_ref, acc_ref, *, q0, nsub):
    tq = q_ref.shape[0]
    tk = k_ref.shape[1]
    sq = tq // nsub
    i = pl.program_id(2)
    qstart = q0 + i * tq
    n_full = qstart // tk
    n_kv = (qstart + tq + tk - 1) // tk
    m_ref[...] = jnp.full(m_ref.shape, NEG_INF, F32)
    l_ref[...] = jnp.zeros(l_ref.shape, F32)
    acc_ref[...] = jnp.zeros(acc_ref.shape, F32)

    def make_step(masked):
        def step(j, _):
            kt = k_ref[j]
            vt = vt_ref[j]
            for u in range(nsub):
                cols = slice(u * sq, (u + 1) * sq)
                s = _dot_nt(kt, q_ref[cols, :])
                if masked:
                    kchunk = (j * tk + lax.broadcasted_iota(jnp.int32, (tk, 1), 0)) // CHUNK
                    qchunk = (qstart + u * sq
                              + lax.broadcasted_iota(jnp.int32, (1, sq), 1)) // CHUNK
                    s = jnp.where(kchunk <= qchunk, s, NEG_INF)
                m_old = m_ref[:, cols]
                m_new = jnp.maximum(m_old, jnp.max(s, axis=0, keepdims=True))
                alpha = jnp.exp2(m_old - m_new)
                p = jnp.exp2(s - m_new)
                l_ref[:, cols] = alpha * l_ref[:, cols] + jnp.sum(p, axis=0, keepdims=True)
                acc_ref[:, cols] = alpha * acc_ref[:, cols] + _dot(vt, p.astype(BF16))
                m_ref[:, cols] = m_new
            return 0
        return step

    lax.fori_loop(0, n_full, make_step(False), 0)
    lax.fori_loop(n_full, n_kv, make_step(True), 0)
    o = (acc_ref[...] / l_ref[...]).T
    o_ref[...] = (o * mz_ref[...].astype(F32)).astype(BF16)


def _attn_pipe_body(q_ref, k_ref, vt_ref, mz_ref, o_ref, s0_ref, s1_ref, c0_ref, c1_ref,
                    p0_ref, p1_ref, a0_ref, a1_ref, m_ref, l_ref, acc_ref):
    tq = q_ref.shape[0]
    tk = k_ref.shape[1]
    nsub = tq // SUBQ
    i = pl.program_id(2)
    all_subs = tuple(range(nsub))
    late_subs = tuple(range(nsub // 2, nsub))

    def cols(u):
        return slice(u * SUBQ, (u + 1) * SUBQ)

    def scores(j, s_ref, c_ref, subs, masked):
        kt = k_ref[j]
        for u in subs:
            s = _dot_nt(kt, q_ref[cols(u), :])
            if masked:
                kchunk = (j * tk + lax.broadcasted_iota(jnp.int32, (tk, 1), 0)) // CHUNK
                qchunk = (i * tq + u * SUBQ
                          + lax.broadcasted_iota(jnp.int32, (1, SUBQ), 1)) // CHUNK
                s = jnp.where(kchunk <= qchunk, s, NEG_INF)
            s_ref[:, cols(u)] = s
            c_ref[:, cols(u)] = jnp.max(s, axis=0, keepdims=True)

    def softmax(s_ref, c_ref, p_ref, a_ref, subs):
        for u in subs:
            m_old = m_ref[:, cols(u)]
            m_new = jnp.maximum(m_old, c_ref[:, cols(u)])
            alpha = jnp.exp2(m_old - m_new)
            p = jnp.exp2(s_ref[:, cols(u)] - m_new)
            l_ref[:, cols(u)] = alpha * l_ref[:, cols(u)] + jnp.sum(p, axis=0, keepdims=True)
            p_ref[:, cols(u)] = p.astype(BF16)
            a_ref[:, cols(u)] = alpha
            m_ref[:, cols(u)] = m_new

    def values(j, p_ref, a_ref, subs):
        vt = vt_ref[j]
        for u in subs:
            acc_ref[:, cols(u)] = (a_ref[:, cols(u)] * acc_ref[:, cols(u)]
                                   + _dot(vt, p_ref[:, cols(u)]))

    m_ref[...] = jnp.full(m_ref.shape, NEG_INF, F32)
    l_ref[...] = jnp.zeros(l_ref.shape, F32)
    acc_ref[...] = jnp.zeros(acc_ref.shape, F32)
    p1_ref[...] = jnp.zeros(p1_ref.shape, BF16)
    a1_ref[...] = jnp.ones(a1_ref.shape, F32)
    scores(0, s0_ref, c0_ref, all_subs, True)

    def pair(u, _):
        scores(2 * u + 1, s1_ref, c1_ref, all_subs, False)
        softmax(s0_ref, c0_ref, p0_ref, a0_ref, all_subs)
        values(jnp.maximum(2 * u - 1, 0), p1_ref, a1_ref, all_subs)
        scores(2 * u + 2, s0_ref, c0_ref, all_subs, True)
        softmax(s1_ref, c1_ref, p1_ref, a1_ref, all_subs)
        values(2 * u, p0_ref, a0_ref, all_subs)
        return 0

    lax.fori_loop(0, i, pair, 0)
    scores(2 * i + 1, s1_ref, c1_ref, late_subs, True)
    softmax(s0_ref, c0_ref, p0_ref, a0_ref, all_subs)
    values(jnp.maximum(2 * i - 1, 0), p1_ref, a1_ref, all_subs)
    softmax(s1_ref, c1_ref, p1_ref, a1_ref, late_subs)
    values(2 * i, p0_ref, a0_ref, all_subs)
    values(2 * i + 1, p1_ref, a1_ref, late_subs)
    o = (acc_ref[...] / l_ref[...]).T
    o_ref[...] = (o * mz_ref[...].astype(F32)).astype(BF16)


def _attention_pipelined(q, k, vt, mz):
    bsz, nh, tlen, _ = q.shape
    nkt, tk = k.shape[2], k.shape[3]
    tq = 2 * tk
    kmap = lambda b, h, i: (b, h, 0, 0, 0)
    omap = lambda b, h, i: (b, i, h)
    stat = pltpu.VMEM((1, tq), F32)
    return pl.pallas_call(
        _attn_pipe_body,
        grid=(bsz, nh, tlen // tq),
        in_specs=[pl.BlockSpec((None, None, tq, HEAD_PAD), lambda b, h, i: (b, h, i, 0)),
                  pl.BlockSpec((None, None, nkt, tk, HEAD_PAD), kmap),
                  pl.BlockSpec((None, None, nkt, MLA_DV, tk), kmap),
                  pl.BlockSpec((None, tq, MLA_DV), omap)],
        out_specs=pl.BlockSpec((None, tq, MLA_DV), omap),
        out_shape=jax.ShapeDtypeStruct(mz.shape, BF16),
        scratch_shapes=[pltpu.VMEM((tk, tq), F32), pltpu.VMEM((tk, tq), F32), stat, stat,
                        pltpu.VMEM((tk, tq), BF16), pltpu.VMEM((tk, tq), BF16), stat, stat,
                        stat, stat, pltpu.VMEM((MLA_DV, tq), F32)],
        compiler_params=pltpu.CompilerParams(
            dimension_semantics=("arbitrary", "arbitrary", "arbitrary"),
            vmem_limit_bytes=VMEM_LIMIT),
        name="mla_attn_pipe",
    )(q, k, vt, mz)


def _attention(q, k, vt, mz, *, grid, qmap, kmap, omap, tq, q0, nsub):
    nkt, tk = k.shape[2], k.shape[3]
    return pl.pallas_call(
        functools.partial(_attn_body, q0=q0, nsub=nsub),
        grid=grid,
        in_specs=[pl.BlockSpec((None, None, tq, HEAD_PAD), qmap),
                  pl.BlockSpec((None, None, nkt, tk, HEAD_PAD), kmap),
                  pl.BlockSpec((None, None, nkt, MLA_DV, tk), kmap),
                  pl.BlockSpec((None, tq, MLA_DV), omap)],
        out_specs=pl.BlockSpec((None, tq, MLA_DV), omap),
        out_shape=jax.ShapeDtypeStruct(mz.shape, BF16),
        scratch_shapes=[pltpu.VMEM((1, tq), F32), pltpu.VMEM((1, tq), F32),
                        pltpu.VMEM((MLA_DV, tq), F32)],
        compiler_params=pltpu.CompilerParams(
            dimension_semantics=("arbitrary", "arbitrary", "arbitrary"),
            vmem_limit_bytes=VMEM_LIMIT),
        name="mla_attn",
    )(q, k, vt, mz)


def _out_mem_body(x_ref, ogla_ref, omla_ref, mk_ref, mv_ref, wout_ref, gq_ref, wmq_ref, wmo_ref,
                  gfin_ref, y_ref, o_scr):
    nb = mk_ref.shape[0]
    rb = x_ref.shape[0] // nb
    x1 = (x_ref[...] + _dot(ogla_ref[...], wout_ref[0:GLA_WIDTH, :])
          + _dot(omla_ref[...], wout_ref[GLA_WIDTH:GLA_WIDTH + MLA_WIDTH, :]))
    h = _rms(x1, gq_ref[...]).astype(BF16)
    qm = (_dot(h, wmq_ref[...]) * (MEM_DH ** -0.5)).astype(BF16)
    for bb in range(nb):
        rows = slice(bb * rb, (bb + 1) * rb)
        for hh in range(MEM_HEADS):
            cols = slice(hh * MEM_DH, (hh + 1) * MEM_DH)
            s = _dot_nt(qm[rows, cols], mk_ref[bb, :, cols].astype(BF16))
            p = jnp.exp(s - jnp.max(s, axis=1, keepdims=True))
            p = p / jnp.sum(p, axis=1, keepdims=True)
            o_scr[rows, cols] = _dot(p.astype(BF16), mv_ref[bb, :, cols].astype(BF16)).astype(BF16)
    x2 = x1 + _dot(o_scr[...], wmo_ref[...])
    y_ref[...] = _rms(x2, gfin_ref[...])


def _out_mem(x, ogla, omla, mk, mv, wts, *, tm, nb):
    g, tlen, _ = x.shape
    nt = tlen // tm
    mmap = (lambda b, t: (b, 0, 0)) if nb == 1 else (lambda b, t: (t, 0, 0))
    row = lambda w: pl.BlockSpec((None, tm, w), lambda b, t: (b, t, 0))
    mspec = pl.BlockSpec((nb, N_MEM, D_MODEL), mmap)
    sq = pl.BlockSpec((D_MODEL, D_MODEL), lambda b, t: (0, 0))
    vec = pl.BlockSpec((1, D_MODEL), lambda b, t: (0, 0))
    return pl.pallas_call(
        _out_mem_body,
        grid=(g, nt),
        in_specs=[row(D_MODEL), row(GLA_WIDTH), row(MLA_WIDTH), mspec, mspec, sq, vec, sq, sq, vec],
        out_specs=row(D_MODEL),
        out_shape=jax.ShapeDtypeStruct(x.shape, F32),
        scratch_shapes=[pltpu.VMEM((tm, D_MODEL), BF16)],
        compiler_params=pltpu.CompilerParams(dimension_semantics=("arbitrary", "arbitrary"),
                                             vmem_limit_bytes=VMEM_LIMIT),
        name="out_mem",
    )(x, ogla, omla, mk, mv, *wts)


def _prep_w_in(w_in):
    splits = np.cumsum([GLA_KW, GLA_KW, GLA_WIDTH, GLA_GATE_RANK, GLA_WIDTH, MLA_Q_RANK,
                        MLA_KV_RANK, MLA_ROPE])
    gq, gk, gv, glr, gz, cq, ckv, kr, mz = jnp.split(w_in, splits.tolist(), axis=1)
    zeros = lambda n: jnp.zeros((D_MODEL, n), w_in.dtype)
    misc = jnp.concatenate([glr, zeros(ROPE_LANE0 - GLA_GATE_RANK), kr,
                            zeros(LANES - ROPE_LANE0 - MLA_ROPE)], axis=1)
    return jnp.concatenate([gq, gk, gv, gz, cq, ckv, mz, misc], axis=1).astype(BF16)


def _pad_heads(w, width):
    kdim = w.shape[0]
    w = w.reshape(kdim, MLA_HEADS, width)
    w = jnp.pad(w, ((0, 0), (0, 0), (0, HEAD_PAD - width)))
    return w.reshape(kdim, MLA_HEADS * HEAD_PAD)


def _rope_tables(pos):
    inv = ROPE_THETA ** (-jnp.arange(HALF_ROPE, dtype=F32) / HALF_ROPE)
    ang = pos.astype(F32)[:, None] * inv[None, :]
    c, s = jnp.cos(ang), jnp.sin(ang)
    n = pos.shape[0]
    cos = jnp.concatenate([jnp.ones((n, ROPE_LANE0), F32), c, c,
                           jnp.zeros((n, LANES - ROPE_LANE0 - MLA_ROPE), F32)], axis=1)
    sin = jnp.concatenate([jnp.zeros((n, ROPE_LANE0), F32), -s, s,
                           jnp.zeros((n, LANES - ROPE_LANE0 - MLA_ROPE), F32)], axis=1)
    return cos, sin


def _gla_mask():
    r = np.arange(GLA_HEADS * CHUNK)
    same_head = (r[:, None] // CHUNK) == (r[None, :] // CHUNK)
    causal = (r[None, :] % CHUNK) <= (r[:, None] % CHUNK)
    return jnp.asarray((same_head & causal).astype(np.float32))


def kernel(x_prompt, x_sample, mem_prompt, cache_mla_ckv, cache_mla_krope, state_gla, cache_mem_k, cache_mem_v, g_mix, w_in, w_gla_g2, b_gla_g, g_gla_out, g_qa, w_qb, g_kva, w_kvb, w_out, g_mem_q, g_mem_kv, w_mq, w_mk, w_mv, w_mo, g_final):
    bp, tp, _ = x_prompt.shape
    bs, ts, _ = x_sample.shape
    past = cache_mla_ckv.shape[2]
    assert ts == CHUNK and g_mix.shape[0] == 1
    l = 0
    row = lambda g: g.reshape(1, -1)

    w_in_p = _prep_w_in(w_in[l])
    wg2_p = jnp.pad(w_gla_g2[l], ((0, LANES - GLA_GATE_RANK), (0, 0))).astype(BF16)
    wqb_p = _pad_heads(w_qb[l], MLA_NOPE + MLA_ROPE).astype(BF16)
    wkv = w_kvb[l].reshape(MLA_KV_RANK, MLA_HEADS, MLA_NOPE + MLA_DV)
    wk_p = _pad_heads(wkv[:, :, :MLA_NOPE].reshape(MLA_KV_RANK, -1), MLA_NOPE).astype(BF16)
    wvt_p = wkv[:, :, MLA_NOPE:].reshape(MLA_KV_RANK, MLA_WIDTH).T.astype(BF16)
    in_wts = (row(g_mix[l]), w_in_p, wg2_p, row(b_gla_g[l]), row(g_gla_out[l]), row(g_qa[l]),
              wqb_p, row(g_kva[l]), wk_p, wvt_p)
    out_wts = (w_out[l].astype(BF16), row(g_mem_q[l]), w_mq[l].astype(BF16),
               w_mo[l].astype(BF16), row(g_final))
    amask = _gla_mask()
    sel = jnp.zeros((MLA_ROPE, HEAD_PAD), BF16).at[
        jnp.arange(MLA_ROPE), ROPE_LANE0 + jnp.arange(MLA_ROPE)].set(1.0)

    tm = 512
    sample_rows = bs * ts
    bpt = tm // ts

    mk_p, mv_p = _mem_kv(mem_prompt.reshape(bp * N_MEM, D_MODEL), row(g_mem_kv[l]),
                         w_mk[l].astype(BF16), w_mv[l].astype(BF16), tm)
    cos_p, sin_p = _rope_tables(jnp.arange(tp, dtype=jnp.int32))
    s0_p = jnp.zeros((bp, GLA_KW, GLA_DV), F32)
    ogla_p, mz_p, q_p, k_p, vt_p, ckv_p, kr_p, st_p = _in_gla(
        x_prompt, s0_p, cos_p, sin_p, amask, in_wts, tm=tm, carry=True)
    omla_p = _attention_pipelined(
        q_p, k_p.reshape(bp, MLA_HEADS, tp // tm, tm, HEAD_PAD), vt_p, mz_p)
    y_prompt = _out_mem(x_prompt, ogla_p, omla_p, mk_p.reshape(bp, N_MEM, D_MODEL),
                        mv_p.reshape(bp, N_MEM, D_MODEL), out_wts, tm=tm, nb=1)

    xs = x_sample.reshape(1, sample_rows, D_MODEL)
    cos_s, sin_s = _rope_tables(past + jnp.arange(ts, dtype=jnp.int32))
    cos_s, sin_s = jnp.tile(cos_s, (bpt, 1)), jnp.tile(sin_s, (bpt, 1))
    s0_s = state_gla[l].reshape(bs, GLA_KW, GLA_DV)
    ogla_s, mz_s, q_s, k_s, vt_s, ckv_s, kr_s, st_s = _in_gla(
        xs, s0_s, cos_s, sin_s, amask, in_wts, tm=tm, carry=False)
    k_past, vt_past = _past_kv(cache_mla_ckv[l], cache_mla_krope[l], wk_p, wvt_p, sel)
    nkeys = past + ts
    kpad = -nkeys % LANES
    per_batch = lambda a: a.reshape(MLA_HEADS, bs, ts, HEAD_PAD)
    k_all = jnp.concatenate(
        [k_past, per_batch(k_s), jnp.zeros((MLA_HEADS, bs, kpad, HEAD_PAD), BF16)], axis=2)
    vt_new = vt_s.reshape(MLA_HEADS, sample_rows // tm, MLA_DV, bpt, ts)
    vt_new = vt_new.transpose(0, 1, 3, 2, 4).reshape(MLA_HEADS, bs, MLA_DV, ts)
    vt_all = jnp.concatenate(
        [vt_past, vt_new, jnp.zeros((MLA_HEADS, bs, MLA_DV, kpad), BF16)], axis=3)
    omla_s = _attention(
        per_batch(q_s), k_all[:, :, None], vt_all[:, :, None], mz_s, grid=(bs, MLA_HEADS, 1),
        qmap=lambda b, h, i: (h, b, 0, 0), kmap=lambda b, h, i: (h, b, 0, 0, 0),
        omap=lambda b, h, i: (0, b, h), tq=ts, q0=past, nsub=1)
    nb_s = 4
    y_sample = _out_mem(xs, ogla_s, omla_s, cache_mem_k[l].reshape(bs, N_MEM, D_MODEL),
                        cache_mem_v[l].reshape(bs, N_MEM, D_MODEL), out_wts,
                        tm=nb_s * ts, nb=nb_s)

    mem_shape = (1, bp, N_MEM, MEM_HEADS, MEM_DH)
    st_shape = (GLA_HEADS, GLA_DK, GLA_DV)
    return (y_prompt, y_sample.reshape(bs, ts, D_MODEL),
            ckv_p[None], kr_p[None], st_p.reshape(1, bp, *st_shape),
            mk_p.reshape(mem_shape), mv_p.reshape(mem_shape),
            ckv_s.reshape(1, bs, ts, MLA_KV_RANK), kr_s.reshape(1, bs, ts, MLA_ROPE),
            st_s.reshape(1, bs, *st_shape))
```

```python
import functools

import jax
import jax.numpy as jnp
import numpy as np
from jax import lax
from jax.experimental import pallas as pl
from jax.experimental.pallas import tpu as pltpu

D_MODEL = 1024
CHUNK = 64
EPS = 1e-6
NEG_INF = -1e30

GLA_HEADS = 4
GLA_DK = 64
GLA_DV = 128
GLA_GATE_RANK = 16
GLA_GATE_NORM = 16.0
GLA_KW = GLA_HEADS * GLA_DK
GLA_WIDTH = GLA_HEADS * GLA_DV

MLA_HEADS = 4
MLA_Q_RANK = 256
MLA_KV_RANK = 128
MLA_NOPE = 64
MLA_ROPE = 32
MLA_DV = 128
MLA_WIDTH = MLA_HEADS * MLA_DV
MLA_SCALE = (MLA_NOPE + MLA_ROPE) ** -0.5
ROPE_THETA = 10000.0
HALF_ROPE = MLA_ROPE // 2
Q_PRESCALE = MLA_SCALE * float(np.log2(np.e))
SUBQ = 256
MASK_LANE0 = MLA_NOPE + MLA_ROPE
Q_TILE_CHUNKS = 16
ROW_SPLITS = 2

MEM_HEADS = 4
MEM_DH = D_MODEL // MEM_HEADS
N_MEM = 256

LANES = 128
HEAD_PAD = 128
ROPE_LANE0 = MLA_NOPE

COL_Q = 0
COL_K = COL_Q + GLA_KW
COL_V = COL_K + GLA_KW
COL_Z = COL_V + GLA_WIDTH
COL_CQ = COL_Z + GLA_WIDTH
COL_CKV = COL_CQ + MLA_Q_RANK
COL_MZ = COL_CKV + MLA_KV_RANK
COL_MISC = COL_MZ + MLA_WIDTH
D_IN_PAD = COL_MISC + LANES

VMEM_LIMIT = 52 * 1024 * 1024

BF16 = jnp.bfloat16
F32 = jnp.float32


def _rms(x, g):
    ms = jnp.mean(x * x, axis=-1, keepdims=True)
    return x * lax.rsqrt(ms + EPS) * g


def _dot(a, b):
    return jnp.dot(a, b, preferred_element_type=F32)


def _dot_nt(a, b):
    return lax.dot_general(a, b, (((1,), (1,)), ((), ())), preferred_element_type=F32)


def _dot_tn(a, b):
    return lax.dot_general(a, b, (((0,), (0,)), ((), ())), preferred_element_type=F32)


def _rope_group(x, cos, sin):
    lane = lax.broadcasted_iota(jnp.int32, x.shape, 1)
    partner = jnp.where(lane < ROPE_LANE0 + HALF_ROPE,
                        pltpu.roll(x, LANES - HALF_ROPE, 1),
                        pltpu.roll(x, HALF_ROPE, 1))
    return x * cos + partner * sin


def _mem_kv_body(mem_ref, g_ref, wk_ref, wv_ref, mk_ref, mv_ref):
    m = _rms(mem_ref[...], g_ref[...]).astype(BF16)
    mk_ref[...] = _dot(m, wk_ref[...])
    mv_ref[...] = _dot(m, wv_ref[...])


def _mem_kv(mem2d, g, wk, wv, tm):
    rows = mem2d.shape[0]
    wspec = pl.BlockSpec((D_MODEL, D_MODEL), lambda i: (0, 0))
    ospec = pl.BlockSpec((tm, D_MODEL), lambda i: (i, 0))
    return pl.pallas_call(
        _mem_kv_body,
        grid=(rows // tm,),
        in_specs=[pl.BlockSpec((tm, D_MODEL), lambda i: (i, 0)),
                  pl.BlockSpec((1, D_MODEL), lambda i: (0, 0)), wspec, wspec],
        out_specs=[ospec, ospec],
        out_shape=[jax.ShapeDtypeStruct((rows, D_MODEL), F32)] * 2,
        compiler_params=pltpu.CompilerParams(dimension_semantics=("arbitrary",),
                                             vmem_limit_bytes=VMEM_LIMIT),
        name="mem_kv",
    )(mem2d, g, wk, wv)


def _in_gla_body(x_ref, s0_ref, cos_ref, sin_ref, kx_ref, amask_ref, gmix_ref, win_ref, wg2_ref, bg_ref,
                 ggla_ref, gqa_ref, wqb_ref, gkva_ref, wk_ref, wvt_ref,
                 ogla_ref, mz_ref, q_ref, k_ref, vt_ref, ckv_ref, kr_ref, sfin_ref,
                 proj_ref, la_ref, st_ref, *, carry):
    tm = x_ref.shape[0]
    t = pl.program_id(1)

    if carry:
        @pl.when(t == 0)
        def _():
            st_ref[...] = s0_ref[0].T

    def prep(rows):
        h = _rms(x_ref[rows, :], gmix_ref[...]).astype(BF16)
        proj_ref[rows, :] = _dot(h, win_ref[...])
        cos = cos_ref[rows, :]
        sin = sin_ref[rows, :]

        cqn = _rms(proj_ref[rows, COL_CQ:COL_CQ + MLA_Q_RANK], gqa_ref[...]).astype(BF16)
        qf = _dot(cqn, wqb_ref[...])
        for hh in range(MLA_HEADS):
            qh = _rope_group(qf[:, hh * HEAD_PAD:(hh + 1) * HEAD_PAD], cos, sin)
            q_ref[hh, rows, :] = (qh * Q_PRESCALE).astype(BF16)

        ckv = _rms(proj_ref[rows, COL_CKV:COL_CKV + MLA_KV_RANK], gkva_ref[...])
        ckv_ref[rows, :] = ckv
        ckv_b = ckv.astype(BF16)
        misc = proj_ref[rows, COL_MISC:COL_MISC + LANES]
        lane = lax.broadcasted_iota(jnp.int32, misc.shape, 1)
        kr = jnp.where(lane >= ROPE_LANE0, _rope_group(misc, cos, sin), 0.0)
        kr_ref[rows, :] = kr[:, ROPE_LANE0:ROPE_LANE0 + MLA_ROPE]
        kr = kr + kx_ref[rows, :]
        kn = _dot(ckv_b, wk_ref[...])
        vt = _dot_nt(wvt_ref[...], ckv_b)
        for hh in range(MLA_HEADS):
            k_ref[hh, rows, :] = (kn[:, hh * HEAD_PAD:(hh + 1) * HEAD_PAD] + kr).astype(BF16)
            vt_ref[hh, :, rows] = vt[hh * MLA_DV:(hh + 1) * MLA_DV, :].astype(BF16)

        mz = proj_ref[rows, COL_MZ:COL_MZ + MLA_WIDTH]
        mz_ref[rows, :] = (mz * jax.nn.sigmoid(mz)).astype(BF16)

        z = _dot(misc.astype(BF16), wg2_ref[...]) + bg_ref[...]
        la_ref[rows, :] = ((jnp.minimum(z, 0.0) - jnp.log1p(jnp.exp(-jnp.abs(z))))
                           * (1.0 / GLA_GATE_NORM))

    for r in range(ROW_SPLITS):
        prep(slice(r * tm // ROW_SPLITS, (r + 1) * tm // ROW_SPLITS))

    klane = lax.broadcasted_iota(jnp.int32, (CHUNK, GLA_KW), 1)
    head_of_lane = klane // GLA_DK
    ri = lax.broadcasted_iota(jnp.int32, (CHUNK, CHUNK), 0)
    ci = lax.broadcasted_iota(jnp.int32, (CHUNK, CHUNK), 1)
    tril = (ci <= ri).astype(BF16)
    tril3 = jnp.concatenate([tril, tril, tril], axis=1)
    g_out = ggla_ref[...]

    def chunk(c, st):
        rows = slice(c * CHUNK, (c + 1) * CHUNK)
        if not carry:
            st = s0_ref[c].T
        q = proj_ref[rows, COL_Q:COL_Q + GLA_KW] * (GLA_DK ** -0.5)
        k = proj_ref[rows, COL_K:COL_K + GLA_KW]
        v = proj_ref[rows, COL_V:COL_V + GLA_WIDTH]
        la = la_ref[rows, :]
        la_hi = la.astype(BF16)
        r1 = la - la_hi.astype(F32)
        la_mid = r1.astype(BF16)
        la_lo = (r1 - la_mid.astype(F32)).astype(BF16)
        b = _dot(tril3, jnp.concatenate([la_hi, la_mid, la_lo], axis=0))
        b_last = b[CHUNK - 1:CHUNK, :]
        q_dec = q * jnp.exp(b)
        k_dec = k * jnp.exp(-b)
        kd2 = k * jnp.exp(b_last - b)
        sdec = jnp.exp(b_last)

        def per_head(a):
            return jnp.concatenate(
                [jnp.where(head_of_lane == hh, a, 0.0) for hh in range(GLA_HEADS)], axis=0)

        qs = per_head(q_dec).astype(BF16)
        kt = jnp.concatenate([k_dec.astype(BF16)] * GLA_HEADS, axis=0)
        a = jnp.where(amask_ref[...] != 0.0, _dot_nt(qs, kt), 0.0).astype(BF16)
        vs = jnp.concatenate([v[:, hh * GLA_DV:(hh + 1) * GLA_DV] for hh in range(GLA_HEADS)],
                             axis=0).astype(BF16)
        o = _dot_nt(qs, st.astype(BF16)) + _dot(a, vs)
        st = st * sdec + _dot_tn(vs, per_head(kd2).astype(BF16))
        gz = proj_ref[rows, COL_Z:COL_Z + GLA_WIDTH]
        gate = gz * jax.nn.sigmoid(gz)
        on = _rms(o, g_out)
        for hh in range(GLA_HEADS):
            cols = slice(hh * GLA_DV, (hh + 1) * GLA_DV)
            ogla_ref[rows, cols] = (on[hh * CHUNK:(hh + 1) * CHUNK, :] * gate[:, cols]).astype(BF16)
        if not carry:
            sfin_ref[c] = st.T
        return st

    st = st_ref[...] if carry else None
    for c in range(tm // CHUNK):
        st = chunk(c, st)

    if carry:
        st_ref[...] = st

        @pl.when(t == pl.num_programs(1) - 1)
        def _():
            sfin_ref[0] = st.T


def _in_gla(x, s0, cos, sin, kx, amask, wts, *, tm, carry):
    g, tlen, _ = x.shape
    nt = tlen // tm
    ns = 1 if carry else tm // CHUNK
    smap = (lambda b, t: (b, 0, 0)) if carry else (lambda b, t: (t, 0, 0))
    tab_tiles = cos.shape[0] // tm
    tmap = (lambda b, t: (t, 0)) if tab_tiles > 1 else (lambda b, t: (0, 0))

    def const(shape):
        return pl.BlockSpec(shape, lambda b, t: (0,) * len(shape))

    row = lambda w: pl.BlockSpec((None, tm, w), lambda b, t: (b, t, 0))
    heads = pl.BlockSpec((None, MLA_HEADS, tm, HEAD_PAD), lambda b, t: (b, 0, t, 0))
    in_specs = [
        row(D_MODEL),
        pl.BlockSpec((ns, GLA_KW, GLA_DV), smap),
        pl.BlockSpec((tm, LANES), tmap), pl.BlockSpec((tm, LANES), tmap),
        pl.BlockSpec((tm, LANES), tmap),
        const((GLA_HEADS * CHUNK, GLA_HEADS * CHUNK)),
        const((1, D_MODEL)), const((D_MODEL, D_IN_PAD)), const((LANES, GLA_KW)), const((1, GLA_KW)),
        const((1, GLA_DV)), const((1, MLA_Q_RANK)), const((MLA_Q_RANK, MLA_HEADS * HEAD_PAD)),
        const((1, MLA_KV_RANK)), const((MLA_KV_RANK, MLA_HEADS * HEAD_PAD)),
        const((MLA_WIDTH, MLA_KV_RANK)),
    ]
    vt_spec = pl.BlockSpec((None, MLA_HEADS, None, MLA_DV, tm), lambda b, t: (b, 0, t, 0, 0))
    out_specs = [row(GLA_WIDTH), row(MLA_WIDTH), heads, heads, vt_spec, row(MLA_KV_RANK),
                 row(MLA_ROPE), pl.BlockSpec((ns, GLA_KW, GLA_DV), smap)]
    hshape = jax.ShapeDtypeStruct((g, MLA_HEADS, tlen, HEAD_PAD), BF16)
    out_shape = [
        jax.ShapeDtypeStruct((g, tlen, GLA_WIDTH), BF16),
        jax.ShapeDtypeStruct((g, tlen, MLA_WIDTH), BF16),
        hshape, hshape, jax.ShapeDtypeStruct((g, MLA_HEADS, nt, MLA_DV, tm), BF16),
        jax.ShapeDtypeStruct((g, tlen, MLA_KV_RANK), F32),
        jax.ShapeDtypeStruct((g, tlen, MLA_ROPE), F32),
        jax.ShapeDtypeStruct(s0.shape, F32),
    ]
    return pl.pallas_call(
        functools.partial(_in_gla_body, carry=carry),
        grid=(g, nt),
        in_specs=in_specs,
        out_specs=out_specs,
        out_shape=out_shape,
        scratch_shapes=[pltpu.VMEM((tm, D_IN_PAD), F32), pltpu.VMEM((tm, GLA_KW), F32),
                        pltpu.VMEM((GLA_DV, GLA_KW), F32)],
        compiler_params=pltpu.CompilerParams(dimension_semantics=("arbitrary", "arbitrary"),
                                             vmem_limit_bytes=VMEM_LIMIT),
        name="in_gla_carry" if carry else "in_gla_batched",
    )(x, s0, cos, sin, kx, amask, *wts)


def _past_kv_body(ckv_ref, kr_ref, wk_ref, wvt_ref, sel_ref, k_ref, vt_ref):
    ckv_b = ckv_ref[...].astype(BF16)
    kr = _dot(kr_ref[...].astype(BF16), sel_ref[...])
    kn = _dot(ckv_b, wk_ref[...])
    vt = _dot_nt(wvt_ref[...], ckv_b)
    for hh in range(MLA_HEADS):
        k_ref[hh] = (kn[:, hh * HEAD_PAD:(hh + 1) * HEAD_PAD] + kr).astype(BF16)
        vt_ref[hh] = vt[hh * MLA_DV:(hh + 1) * MLA_DV, :].astype(BF16)


def _past_kv(ckv, kr, wk, wvt, sel):
    bsz, past, _ = ckv.shape
    return pl.pallas_call(
        _past_kv_body,
        grid=(bsz,),
        in_specs=[pl.BlockSpec((None, past, MLA_KV_RANK), lambda b: (b, 0, 0)),
                  pl.BlockSpec((None, past, MLA_ROPE), lambda b: (b, 0, 0)),
                  pl.BlockSpec(wk.shape, lambda b: (0, 0)),
                  pl.BlockSpec(wvt.shape, lambda b: (0, 0)),
                  pl.BlockSpec(sel.shape, lambda b: (0, 0))],
        out_specs=[pl.BlockSpec((MLA_HEADS, None, past, HEAD_PAD), lambda b: (0, b, 0, 0)),
                   pl.BlockSpec((MLA_HEADS, None, MLA_DV, past), lambda b: (0, b, 0, 0))],
        out_shape=[jax.ShapeDtypeStruct((MLA_HEADS, bsz, past, HEAD_PAD), BF16),
                   jax.ShapeDtypeStruct((MLA_HEADS, bsz, MLA_DV, past), BF16)],
        compiler_params=pltpu.CompilerParams(dimension_semantics=("arbitrary",),
                                             vmem_limit_bytes=VMEM_LIMIT),
        name="past_kv",
    )(ckv, kr, wk, wvt, sel)


def _attn_body(q_ref, k_ref, vt_ref, mz_ref, o_ref, m_ref, l_ref, acc_ref, *, q0, nsub):
    tq = q_ref.shape[0]
    tk = k_ref.shape[1]
    sq = tq // nsub
    i = pl.program_id(2)
    qstart = q0 + i * tq
    n_full = qstart // tk
    n_kv = (qstart + tq + tk - 1) // tk
    m_ref[...] = jnp.full(m_ref.shape, NEG_INF, F32)
    l_ref[...] = jnp.zeros(l_ref.shape, F32)
    acc_ref[...] = jnp.zeros(acc_ref.shape, F32)

    def make_step(masked):
        def step(j, _):
            kt = k_ref[j]
            vt = vt_ref[j]
            for u in range(nsub):
                cols = slice(u * sq, (u + 1) * sq)
                s = _dot_nt(kt, q_ref[cols, :])
                if masked:
                    kchunk = (j * tk + lax.broadcasted_iota(jnp.int32, (tk, 1), 0)) // CHUNK
                    qchunk = (qstart + u * sq
                              + lax.broadcasted_iota(jnp.int32, (1, sq), 1)) // CHUNK
                    s = jnp.where(kchunk <= qchunk, s, NEG_INF)
                m_old = m_ref[:, cols]
                m_new = jnp.maximum(m_old, jnp.max(s, axis=0, keepdims=True))
                alpha = jnp.exp2(m_old - m_new)
                p = jnp.exp2(s - m_new)
                l_ref[:, cols] = alpha * l_ref[:, cols] + jnp.sum(p, axis=0, keepdims=True)
                acc_ref[:, cols] = alpha * acc_ref[:, cols] + _dot(vt, p.astype(BF16))
                m_ref[:, cols] = m_new
            return 0
        return step

    lax.fori_loop(0, n_full, make_step(False), 0)
    lax.fori_loop(n_full, n_kv, make_step(True), 0)
    o = (acc_ref[...] / l_ref[...]).T
    o_ref[...] = (o * mz_ref[...].astype(F32)).astype(BF16)


def _attn_pipe_body(q_ref, k_ref, vt_ref, mz_ref, o_ref, q2_ref, s0_ref, s1_ref, c0_ref, c1_ref,
                    p0_ref, p1_ref, a0_ref, a1_ref, m_ref, l_ref, acc_ref):
    tq = q_ref.shape[0]
    tk = k_ref.shape[1]
    nsub = tq // SUBQ
    i = pl.program_id(2)
    all_subs = tuple(range(nsub))
    late_subs = tuple(range(nsub // 2, nsub))

    def cols(u):
        return slice(u * SUBQ, (u + 1) * SUBQ)

    def scores(j, s_ref, c_ref, subs, diagonal):
        kt = k_ref[j]
        for u in subs:
            s = _dot_nt(kt, q2_ref[diagonal, cols(u), :])
            s_ref[:, cols(u)] = s
            c_ref[:, cols(u)] = jnp.max(s, axis=0, keepdims=True)

    def softmax(s_ref, c_ref, p_ref, a_ref, subs):
        for u in subs:
            m_old = m_ref[:, cols(u)]
            m_new = jnp.maximum(m_old, c_ref[:, cols(u)])
            alpha = jnp.exp2(m_old - m_new)
            p = jnp.exp2(s_ref[:, cols(u)] - m_new)
            l_ref[:, cols(u)] = alpha * l_ref[:, cols(u)] + jnp.sum(p, axis=0, keepdims=True)
            p_ref[:, cols(u)] = p.astype(BF16)
            a_ref[:, cols(u)] = alpha
            m_ref[:, cols(u)] = m_new

    def values(j, p_ref, a_ref, subs):
        vt = vt_ref[j]
        for u in subs:
            acc_ref[:, cols(u)] = (a_ref[:, cols(u)] * acc_ref[:, cols(u)]
                                   + _dot(vt, p_ref[:, cols(u)]))

    m_ref[...] = jnp.full(m_ref.shape, NEG_INF, F32)
    l_ref[...] = jnp.zeros(l_ref.shape, F32)
    acc_ref[...] = jnp.zeros(acc_ref.shape, F32)
    p1_ref[...] = jnp.zeros(p1_ref.shape, BF16)
    a1_ref[...] = jnp.ones(a1_ref.shape, F32)
    q = q_ref[...]
    lane = lax.broadcasted_iota(jnp.int32, q.shape, 1) - MASK_LANE0
    qchunk = lax.broadcasted_iota(jnp.int32, q.shape, 0) // CHUNK
    hidden = (lane > qchunk) & (lane < Q_TILE_CHUNKS)
    q2_ref[0] = q
    q2_ref[1] = jnp.where(hidden, NEG_INF, q.astype(F32)).astype(BF16)
    scores(0, s0_ref, c0_ref, all_subs, (i == 0).astype(jnp.int32))

    def pair(u, _):
        scores(2 * u + 1, s1_ref, c1_ref, all_subs, 0)
        softmax(s0_ref, c0_ref, p0_ref, a0_ref, all_subs)
        values(jnp.maximum(2 * u - 1, 0), p1_ref, a1_ref, all_subs)
        scores(2 * u + 2, s0_ref, c0_ref, all_subs, (u == i - 1).astype(jnp.int32))
        softmax(s1_ref, c1_ref, p1_ref, a1_ref, all_subs)
        values(2 * u, p0_ref, a0_ref, all_subs)
        return 0

    lax.fori_loop(0, i, pair, 0)
    scores(2 * i + 1, s1_ref, c1_ref, late_subs, 1)
    softmax(s0_ref, c0_ref, p0_ref, a0_ref, all_subs)
    values(jnp.maximum(2 * i - 1, 0), p1_ref, a1_ref, all_subs)
    softmax(s1_ref, c1_ref, p1_ref, a1_ref, late_subs)
    values(2 * i, p0_ref, a0_ref, all_subs)
    values(2 * i + 1, p1_ref, a1_ref, late_subs)
    o = (acc_ref[...] / l_ref[...]).T
    o_ref[...] = (o * mz_ref[...].astype(F32)).astype(BF16)


def _attention_pipelined(q, k, vt, mz):
    bsz, nh, tlen, _ = q.shape
    nkt, tk = k.shape[2], k.shape[3]
    tq = 2 * tk
    assert tq == Q_TILE_CHUNKS * CHUNK and tlen % tq == 0
    kmap = lambda b, h, i: (b, h, 0, 0, 0)
    omap = lambda b, h, i: (b, i, h)
    stat = pltpu.VMEM((1, tq), F32)
    return pl.pallas_call(
        _attn_pipe_body,
        grid=(bsz, nh, tlen // tq),
        in_specs=[pl.BlockSpec((None, None, tq, HEAD_PAD), lambda b, h, i: (b, h, i, 0)),
                  pl.BlockSpec((None, None, nkt, tk, HEAD_PAD), kmap),
                  pl.BlockSpec((None, None, nkt, MLA_DV, tk), kmap),
                  pl.BlockSpec((None, tq, MLA_DV), omap)],
        out_specs=pl.BlockSpec((None, tq, MLA_DV), omap),
        out_shape=jax.ShapeDtypeStruct(mz.shape, BF16),
        scratch_shapes=[pltpu.VMEM((2, tq, HEAD_PAD), BF16),
                        pltpu.VMEM((tk, tq), F32), pltpu.VMEM((tk, tq), F32), stat, stat,
                        pltpu.VMEM((tk, tq), BF16), pltpu.VMEM((tk, tq), BF16), stat, stat,
                        stat, stat, pltpu.VMEM((MLA_DV, tq), F32)],
        compiler_params=pltpu.CompilerParams(
            dimension_semantics=("arbitrary", "arbitrary", "arbitrary"),
            vmem_limit_bytes=VMEM_LIMIT),
        name="mla_attn_pipe",
    )(q, k, vt, mz)


def _attention(q, k, vt, mz, *, grid, qmap, kmap, omap, tq, q0, nsub):
    nkt, tk = k.shape[2], k.shape[3]
    return pl.pallas_call(
        functools.partial(_attn_body, q0=q0, nsub=nsub),
        grid=grid,
        in_specs=[pl.BlockSpec((None, None, tq, HEAD_PAD), qmap),
                  pl.BlockSpec((None, None, nkt, tk, HEAD_PAD), kmap),
                  pl.BlockSpec((None, None, nkt, MLA_DV, tk), kmap),
                  pl.BlockSpec((None, tq, MLA_DV), omap)],
        out_specs=pl.BlockSpec((None, tq, MLA_DV), omap),
        out_shape=jax.ShapeDtypeStruct(mz.shape, BF16),
        scratch_shapes=[pltpu.VMEM((1, tq), F32), pltpu.VMEM((1, tq), F32),
                        pltpu.VMEM((MLA_DV, tq), F32)],
        compiler_params=pltpu.CompilerParams(
            dimension_semantics=("arbitrary", "arbitrary", "arbitrary"),
            vmem_limit_bytes=VMEM_LIMIT),
        name="mla_attn",
    )(q, k, vt, mz)


def _out_mem_body(x_ref, ogla_ref, omla_ref, mk_ref, mv_ref, wout_ref, gq_ref, wmq_ref, wmo_ref,
                  gfin_ref, y_ref, o_scr):
    nb = mk_ref.shape[0]
    rb = x_ref.shape[0] // nb
    x1 = (x_ref[...] + _dot(ogla_ref[...], wout_ref[0:GLA_WIDTH, :])
          + _dot(omla_ref[...], wout_ref[GLA_WIDTH:GLA_WIDTH + MLA_WIDTH, :]))
    h = _rms(x1, gq_ref[...]).astype(BF16)
    qm = (_dot(h, wmq_ref[...]) * (MEM_DH ** -0.5)).astype(BF16)
    for bb in range(nb):
        rows = slice(bb * rb, (bb + 1) * rb)
        for hh in range(MEM_HEADS):
            cols = slice(hh * MEM_DH, (hh + 1) * MEM_DH)
            s = _dot_nt(qm[rows, cols], mk_ref[bb, :, cols].astype(BF16))
            p = jnp.exp(s - jnp.max(s, axis=1, keepdims=True))
            p = p / jnp.sum(p, axis=1, keepdims=True)
            o_scr[rows, cols] = _dot(p.astype(BF16), mv_ref[bb, :, cols].astype(BF16)).astype(BF16)
    x2 = x1 + _dot(o_scr[...], wmo_ref[...])
    y_ref[...] = _rms(x2, gfin_ref[...])


def _out_mem(x, ogla, omla, mk, mv, wts, *, tm, nb):
    g, tlen, _ = x.shape
    nt = tlen // tm
    mmap = (lambda b, t: (b, 0, 0)) if nb == 1 else (lambda b, t: (t, 0, 0))
    row = lambda w: pl.BlockSpec((None, tm, w), lambda b, t: (b, t, 0))
    mspec = pl.BlockSpec((nb, N_MEM, D_MODEL), mmap)
    sq = pl.BlockSpec((D_MODEL, D_MODEL), lambda b, t: (0, 0))
    vec = pl.BlockSpec((1, D_MODEL), lambda b, t: (0, 0))
    return pl.pallas_call(
        _out_mem_body,
        grid=(g, nt),
        in_specs=[row(D_MODEL), row(GLA_WIDTH), row(MLA_WIDTH), mspec, mspec, sq, vec, sq, sq, vec],
        out_specs=row(D_MODEL),
        out_shape=jax.ShapeDtypeStruct(x.shape, F32),
        scratch_shapes=[pltpu.VMEM((tm, D_MODEL), BF16)],
        compiler_params=pltpu.CompilerParams(dimension_semantics=("arbitrary", "arbitrary"),
                                             vmem_limit_bytes=VMEM_LIMIT),
        name="out_mem",
    )(x, ogla, omla, mk, mv, *wts)


def _prep_w_in(w_in):
    splits = np.cumsum([GLA_KW, GLA_KW, GLA_WIDTH, GLA_GATE_RANK, GLA_WIDTH, MLA_Q_RANK,
                        MLA_KV_RANK, MLA_ROPE])
    gq, gk, gv, glr, gz, cq, ckv, kr, mz = jnp.split(w_in, splits.tolist(), axis=1)
    zeros = lambda n: jnp.zeros((D_MODEL, n), w_in.dtype)
    misc = jnp.concatenate([glr, zeros(ROPE_LANE0 - GLA_GATE_RANK), kr,
                            zeros(LANES - ROPE_LANE0 - MLA_ROPE)], axis=1)
    return jnp.concatenate([gq, gk, gv, gz, cq, ckv, mz, misc], axis=1).astype(BF16)


def _pad_heads(w, width):
    kdim = w.shape[0]
    w = w.reshape(kdim, MLA_HEADS, width)
    w = jnp.pad(w, ((0, 0), (0, 0), (0, HEAD_PAD - width)))
    return w.reshape(kdim, MLA_HEADS * HEAD_PAD)


def _rope_tables(pos):
    inv = ROPE_THETA ** (-jnp.arange(HALF_ROPE, dtype=F32) / HALF_ROPE)
    ang = pos.astype(F32)[:, None] * inv[None, :]
    c, s = jnp.cos(ang), jnp.sin(ang)
    n = pos.shape[0]
    cos = jnp.concatenate([jnp.ones((n, ROPE_LANE0), F32), c, c,
                           jnp.zeros((n, LANES - ROPE_LANE0 - MLA_ROPE), F32)], axis=1)
    sin = jnp.concatenate([jnp.zeros((n, ROPE_LANE0), F32), -s, s,
                           jnp.zeros((n, LANES - ROPE_LANE0 - MLA_ROPE), F32)], axis=1)
    return cos, sin


def _chunk_onehot(pos):
    cidx = (np.asarray(pos) // CHUNK) % Q_TILE_CHUNKS
    tab = np.zeros((len(cidx), LANES), np.float32)
    tab[np.arange(len(cidx)), MASK_LANE0 + cidx] = 1.0
    return jnp.asarray(tab)


def _gla_mask():
    r = np.arange(GLA_HEADS * CHUNK)
    same_head = (r[:, None] // CHUNK) == (r[None, :] // CHUNK)
    causal = (r[None, :] % CHUNK) <= (r[:, None] % CHUNK)
    return jnp.asarray((same_head & causal).astype(np.float32))


def kernel(x_prompt, x_sample, mem_prompt, cache_mla_ckv, cache_mla_krope, state_gla, cache_mem_k, cache_mem_v, g_mix, w_in, w_gla_g2, b_gla_g, g_gla_out, g_qa, w_qb, g_kva, w_kvb, w_out, g_mem_q, g_mem_kv, w_mq, w_mk, w_mv, w_mo, g_final):
    bp, tp, _ = x_prompt.shape
    bs, ts, _ = x_sample.shape
    past = cache_mla_ckv.shape[2]
    assert ts == CHUNK and g_mix.shape[0] == 1
    l = 0
    row = lambda g: g.reshape(1, -1)

    w_in_p = _prep_w_in(w_in[l])
    wg2_p = jnp.pad(w_gla_g2[l], ((0, LANES - GLA_GATE_RANK), (0, 0))).astype(BF16)
    wqb_p = _pad_heads(w_qb[l], MLA_NOPE + MLA_ROPE).astype(BF16)
    wkv = w_kvb[l].reshape(MLA_KV_RANK, MLA_HEADS, MLA_NOPE + MLA_DV)
    wk_p = _pad_heads(wkv[:, :, :MLA_NOPE].reshape(MLA_KV_RANK, -1), MLA_NOPE).astype(BF16)
    wvt_p = wkv[:, :, MLA_NOPE:].reshape(MLA_KV_RANK, MLA_WIDTH).T.astype(BF16)
    in_wts = (row(g_mix[l]), w_in_p, wg2_p, row(b_gla_g[l]), row(g_gla_out[l]), row(g_qa[l]),
              wqb_p, row(g_kva[l]), wk_p, wvt_p)
    out_wts = (w_out[l].astype(BF16), row(g_mem_q[l]), w_mq[l].astype(BF16),
               w_mo[l].astype(BF16), row(g_final))
    amask = _gla_mask()
    sel = jnp.zeros((MLA_ROPE, HEAD_PAD), BF16).at[
        jnp.arange(MLA_ROPE), ROPE_LANE0 + jnp.arange(MLA_ROPE)].set(1.0)

    tm = 512
    sample_rows = bs * ts
    bpt = tm // ts

    mk_p, mv_p = _mem_kv(mem_prompt.reshape(bp * N_MEM, D_MODEL), row(g_mem_kv[l]),
                         w_mk[l].astype(BF16), w_mv[l].astype(BF16), tm)
    cos_p, sin_p = _rope_tables(jnp.arange(tp, dtype=jnp.int32))
    s0_p = jnp.zeros((bp, GLA_KW, GLA_DV), F32)
    ogla_p, mz_p, q_p, k_p, vt_p, ckv_p, kr_p, st_p = _in_gla(
        x_prompt, s0_p, cos_p, sin_p, _chunk_onehot(np.arange(tp)), amask, in_wts, tm=tm, carry=True)
    omla_p = _attention_pipelined(
        q_p, k_p.reshape(bp, MLA_HEADS, tp // tm, tm, HEAD_PAD), vt_p, mz_p)
    y_prompt = _out_mem(x_prompt, ogla_p, omla_p, mk_p.reshape(bp, N_MEM, D_MODEL),
                        mv_p.reshape(bp, N_MEM, D_MODEL), out_wts, tm=tm, nb=1)

    xs = x_sample.reshape(1, sample_rows, D_MODEL)
    cos_s, sin_s = _rope_tables(past + jnp.arange(ts, dtype=jnp.int32))
    cos_s, sin_s = jnp.tile(cos_s, (bpt, 1)), jnp.tile(sin_s, (bpt, 1))
    s0_s = state_gla[l].reshape(bs, GLA_KW, GLA_DV)
    ogla_s, mz_s, q_s, k_s, vt_s, ckv_s, kr_s, st_s = _in_gla(
        xs, s0_s, cos_s, sin_s, jnp.zeros((tm, LANES), F32), amask, in_wts, tm=tm, carry=False)
    k_past, vt_past = _past_kv(cache_mla_ckv[l], cache_mla_krope[l], wk_p, wvt_p, sel)
    nkeys = past + ts
    kpad = -nkeys % LANES
    per_batch = lambda a: a.reshape(MLA_HEADS, bs, ts, HEAD_PAD)
    k_all = jnp.concatenate(
        [k_past, per_batch(k_s), jnp.zeros((MLA_HEADS, bs, kpad, HEAD_PAD), BF16)], axis=2)
    vt_new = vt_s.reshape(MLA_HEADS, sample_rows // tm, MLA_DV, bpt, ts)
    vt_new = vt_new.transpose(0, 1, 3, 2, 4).reshape(MLA_HEADS, bs, MLA_DV, ts)
    vt_all = jnp.concatenate(
        [vt_past, vt_new, jnp.zeros((MLA_HEADS, bs, MLA_DV, kpad), BF16)], axis=3)
    omla_s = _attention(
        per_batch(q_s), k_all[:, :, None], vt_all[:, :, None], mz_s, grid=(bs, MLA_HEADS, 1),
        qmap=lambda b, h, i: (h, b, 0, 0), kmap=lambda b, h, i: (h, b, 0, 0, 0),
        omap=lambda b, h, i: (0, b, h), tq=ts, q0=past, nsub=1)
    nb_s = 4
    y_sample = _out_mem(xs, ogla_s, omla_s, cache_mem_k[l].reshape(bs, N_MEM, D_MODEL),
                        cache_mem_v[l].reshape(bs, N_MEM, D_MODEL), out_wts,
                        tm=nb_s * ts, nb=nb_s)

    mem_shape = (1, bp, N_MEM, MEM_HEADS, MEM_DH)
    st_shape = (GLA_HEADS, GLA_DK, GLA_DV)
    return (y_prompt, y_sample.reshape(bs, ts, D_MODEL),
            ckv_p[None], kr_p[None], st_p.reshape(1, bp, *st_shape),
            mk_p.reshape(mem_shape), mv_p.reshape(mem_shape),
            ckv_s.reshape(1, bs, ts, MLA_KV_RANK), kr_s.reshape(1, bs, ts, MLA_ROPE),
            st_s.reshape(1, bs, *st_shape))
```

```python
import functools

import jax
import jax.numpy as jnp
import numpy as np
from jax import lax
from jax.experimental import pallas as pl
from jax.experimental.pallas import tpu as pltpu

D_MODEL = 1024
CHUNK = 64
EPS = 1e-6
NEG_INF = -1e30

GLA_HEADS = 4
GLA_DK = 64
GLA_DV = 128
GLA_GATE_RANK = 16
GLA_GATE_NORM = 16.0
GLA_KW = GLA_HEADS * GLA_DK
GLA_WIDTH = GLA_HEADS * GLA_DV

MLA_HEADS = 4
MLA_Q_RANK = 256
MLA_KV_RANK = 128
MLA_NOPE = 64
MLA_ROPE = 32
MLA_DV = 128
MLA_WIDTH = MLA_HEADS * MLA_DV
MLA_SCALE = (MLA_NOPE + MLA_ROPE) ** -0.5
ROPE_THETA = 10000.0
HALF_ROPE = MLA_ROPE // 2
Q_PRESCALE = MLA_SCALE * float(np.log2(np.e))
SUBQ = 256
MASK_LANE0 = MLA_NOPE + MLA_ROPE
Q_TILE_CHUNKS = 128 - MASK_LANE0
ROW_SPLITS = 2
ONES_ROWS = 16

MEM_HEADS = 4
MEM_DH = D_MODEL // MEM_HEADS
N_MEM = 256

LANES = 128
HEAD_PAD = 128
ROPE_LANE0 = MLA_NOPE

COL_Q = 0
COL_K = COL_Q + GLA_KW
COL_V = COL_K + GLA_KW
COL_Z = COL_V + GLA_WIDTH
COL_CQ = COL_Z + GLA_WIDTH
COL_CKV = COL_CQ + MLA_Q_RANK
COL_MZ = COL_CKV + MLA_KV_RANK
COL_MISC = COL_MZ + MLA_WIDTH
D_IN_PAD = COL_MISC + LANES

VMEM_LIMIT = 52 * 1024 * 1024

BF16 = jnp.bfloat16
F32 = jnp.float32


def _rms(x, g):
    ms = jnp.mean(x * x, axis=-1, keepdims=True)
    return x * lax.rsqrt(ms + EPS) * g


def _dot(a, b):
    return jnp.dot(a, b, preferred_element_type=F32)


def _dot_nt(a, b):
    return lax.dot_general(a, b, (((1,), (1,)), ((), ())), preferred_element_type=F32)


def _dot_tn(a, b):
    return lax.dot_general(a, b, (((0,), (0,)), ((), ())), preferred_element_type=F32)


def _rope_group(x, cos, sin):
    lane = lax.broadcasted_iota(jnp.int32, x.shape, 1)
    partner = jnp.where(lane < ROPE_LANE0 + HALF_ROPE,
                        pltpu.roll(x, LANES - HALF_ROPE, 1),
                        pltpu.roll(x, HALF_ROPE, 1))
    return x * cos + partner * sin


def _mem_kv_body(mem_ref, g_ref, wk_ref, wv_ref, mk_ref, mv_ref):
    m = _rms(mem_ref[...], g_ref[...]).astype(BF16)
    mk_ref[...] = _dot(m, wk_ref[...])
    mv_ref[...] = _dot(m, wv_ref[...])


def _mem_kv(mem2d, g, wk, wv, tm):
    rows = mem2d.shape[0]
    wspec = pl.BlockSpec((D_MODEL, D_MODEL), lambda i: (0, 0))
    ospec = pl.BlockSpec((tm, D_MODEL), lambda i: (i, 0))
    return pl.pallas_call(
        _mem_kv_body,
        grid=(rows // tm,),
        in_specs=[pl.BlockSpec((tm, D_MODEL), lambda i: (i, 0)),
                  pl.BlockSpec((1, D_MODEL), lambda i: (0, 0)), wspec, wspec],
        out_specs=[ospec, ospec],
        out_shape=[jax.ShapeDtypeStruct((rows, D_MODEL), F32)] * 2,
        compiler_params=pltpu.CompilerParams(dimension_semantics=("arbitrary",),
                                             vmem_limit_bytes=VMEM_LIMIT),
        name="mem_kv",
    )(mem2d, g, wk, wv)


def _in_gla_body(x_ref, s0_ref, cos_ref, sin_ref, kx_ref, amask_ref, gmix_ref, win_ref, wg2_ref, bg_ref,
                 ggla_ref, gqa_ref, wqb_ref, gkva_ref, wk_ref, wvt_ref,
                 ogla_ref, mz_ref, q_ref, k_ref, vt_ref, ckv_ref, kr_ref, sfin_ref,
                 proj_ref, la_ref, st_ref, *, carry):
    tm = x_ref.shape[0]
    t = pl.program_id(1)

    if carry:
        @pl.when(t == 0)
        def _():
            st_ref[...] = s0_ref[0].T

    def prep(rows):
        h = _rms(x_ref[rows, :], gmix_ref[...]).astype(BF16)
        proj_ref[rows, :] = _dot(h, win_ref[...])
        cos = cos_ref[rows, :]
        sin = sin_ref[rows, :]

        cqn = _rms(proj_ref[rows, COL_CQ:COL_CQ + MLA_Q_RANK], gqa_ref[...]).astype(BF16)
        qf = _dot(cqn, wqb_ref[...])
        for hh in range(MLA_HEADS):
            qh = _rope_group(qf[:, hh * HEAD_PAD:(hh + 1) * HEAD_PAD], cos, sin)
            q_ref[hh, rows, :] = (qh * Q_PRESCALE).astype(BF16)

        ckv = _rms(proj_ref[rows, COL_CKV:COL_CKV + MLA_KV_RANK], gkva_ref[...])
        ckv_ref[rows, :] = ckv
        ckv_b = ckv.astype(BF16)
        misc = proj_ref[rows, COL_MISC:COL_MISC + LANES]
        lane = lax.broadcasted_iota(jnp.int32, misc.shape, 1)
        kr = jnp.where(lane >= ROPE_LANE0, _rope_group(misc, cos, sin), 0.0)
        kr_ref[rows, :] = kr[:, ROPE_LANE0:ROPE_LANE0 + MLA_ROPE]
        kr = kr + kx_ref[rows, :]
        kn = _dot(ckv_b, wk_ref[...])
        vt = _dot_nt(wvt_ref[...], ckv_b)
        for hh in range(MLA_HEADS):
            k_ref[hh, rows, :] = (kn[:, hh * HEAD_PAD:(hh + 1) * HEAD_PAD] + kr).astype(BF16)
            vt_ref[hh, :, rows] = vt[hh * MLA_DV:(hh + 1) * MLA_DV, :].astype(BF16)

        mz = proj_ref[rows, COL_MZ:COL_MZ + MLA_WIDTH]
        mz_ref[rows, :] = (mz * jax.nn.sigmoid(mz)).astype(BF16)

        z = _dot(misc.astype(BF16), wg2_ref[...]) + bg_ref[...]
        la_ref[rows, :] = ((jnp.minimum(z, 0.0) - jnp.log1p(jnp.exp(-jnp.abs(z))))
                           * (1.0 / GLA_GATE_NORM))

    for r in range(ROW_SPLITS):
        prep(slice(r * tm // ROW_SPLITS, (r + 1) * tm // ROW_SPLITS))

    klane = lax.broadcasted_iota(jnp.int32, (CHUNK, GLA_KW), 1)
    head_of_lane = klane // GLA_DK
    ri = lax.broadcasted_iota(jnp.int32, (CHUNK, CHUNK), 0)
    ci = lax.broadcasted_iota(jnp.int32, (CHUNK, CHUNK), 1)
    tril = (ci <= ri).astype(BF16)
    tril3 = jnp.concatenate([tril, tril, tril], axis=1)
    g_out = ggla_ref[...]

    def chunk(c, st):
        rows = slice(c * CHUNK, (c + 1) * CHUNK)
        if not carry:
            st = s0_ref[c].T
        q = proj_ref[rows, COL_Q:COL_Q + GLA_KW] * (GLA_DK ** -0.5)
        k = proj_ref[rows, COL_K:COL_K + GLA_KW]
        v = proj_ref[rows, COL_V:COL_V + GLA_WIDTH]
        la = la_ref[rows, :]
        la_hi = la.astype(BF16)
        r1 = la - la_hi.astype(F32)
        la_mid = r1.astype(BF16)
        la_lo = (r1 - la_mid.astype(F32)).astype(BF16)
        b = _dot(tril3, jnp.concatenate([la_hi, la_mid, la_lo], axis=0))
        b_last = b[CHUNK - 1:CHUNK, :]
        q_dec = q * jnp.exp(b)
        k_dec = k * jnp.exp(-b)
        kd2 = k * jnp.exp(b_last - b)
        sdec = jnp.exp(b_last)

        def per_head(a):
            return jnp.concatenate(
                [jnp.where(head_of_lane == hh, a, 0.0) for hh in range(GLA_HEADS)], axis=0)

        qs = per_head(q_dec).astype(BF16)
        kt = jnp.concatenate([k_dec.astype(BF16)] * GLA_HEADS, axis=0)
        a = jnp.where(amask_ref[...] != 0.0, _dot_nt(qs, kt), 0.0).astype(BF16)
        vs = jnp.concatenate([v[:, hh * GLA_DV:(hh + 1) * GLA_DV] for hh in range(GLA_HEADS)],
                             axis=0).astype(BF16)
        o = _dot_nt(qs, st.astype(BF16)) + _dot(a, vs)
        st = st * sdec + _dot_tn(vs, per_head(kd2).astype(BF16))
        gz = proj_ref[rows, COL_Z:COL_Z + GLA_WIDTH]
        gate = gz * jax.nn.sigmoid(gz)
        on = _rms(o, g_out)
        for hh in range(GLA_HEADS):
            cols = slice(hh * GLA_DV, (hh + 1) * GLA_DV)
            ogla_ref[rows, cols] = (on[hh * CHUNK:(hh + 1) * CHUNK, :] * gate[:, cols]).astype(BF16)
        if not carry:
            sfin_ref[c] = st.T
        return st

    st = st_ref[...] if carry else None
    for c in range(tm // CHUNK):
        st = chunk(c, st)

    if carry:
        st_ref[...] = st

        @pl.when(t == pl.num_programs(1) - 1)
        def _():
            sfin_ref[0] = st.T


def _in_gla(x, s0, cos, sin, kx, amask, wts, *, tm, carry):
    g, tlen, _ = x.shape
    nt = tlen // tm
    ns = 1 if carry else tm // CHUNK
    smap = (lambda b, t: (b, 0, 0)) if carry else (lambda b, t: (t, 0, 0))
    tab_tiles = cos.shape[0] // tm
    tmap = (lambda b, t: (t, 0)) if tab_tiles > 1 else (lambda b, t: (0, 0))

    def const(shape):
        return pl.BlockSpec(shape, lambda b, t: (0,) * len(shape))

    row = lambda w: pl.BlockSpec((None, tm, w), lambda b, t: (b, t, 0))
    heads = pl.BlockSpec((None, MLA_HEADS, tm, HEAD_PAD), lambda b, t: (b, 0, t, 0))
    in_specs = [
        row(D_MODEL),
        pl.BlockSpec((ns, GLA_KW, GLA_DV), smap),
        pl.BlockSpec((tm, LANES), tmap), pl.BlockSpec((tm, LANES), tmap),
        pl.BlockSpec((tm, LANES), tmap),
        const((GLA_HEADS * CHUNK, GLA_HEADS * CHUNK)),
        const((1, D_MODEL)), const((D_MODEL, D_IN_PAD)), const((LANES, GLA_KW)), const((1, GLA_KW)),
        const((1, GLA_DV)), const((1, MLA_Q_RANK)), const((MLA_Q_RANK, MLA_HEADS * HEAD_PAD)),
        const((1, MLA_KV_RANK)), const((MLA_KV_RANK, MLA_HEADS * HEAD_PAD)),
        const((MLA_WIDTH, MLA_KV_RANK)),
    ]
    vt_spec = pl.BlockSpec((None, MLA_HEADS, None, MLA_DV, tm), lambda b, t: (b, 0, t, 0, 0))
    out_specs = [row(GLA_WIDTH), row(MLA_WIDTH), heads, heads, vt_spec, row(MLA_KV_RANK),
                 row(MLA_ROPE), pl.BlockSpec((ns, GLA_KW, GLA_DV), smap)]
    hshape = jax.ShapeDtypeStruct((g, MLA_HEADS, tlen, HEAD_PAD), BF16)
    out_shape = [
        jax.ShapeDtypeStruct((g, tlen, GLA_WIDTH), BF16),
        jax.ShapeDtypeStruct((g, tlen, MLA_WIDTH), BF16),
        hshape, hshape, jax.ShapeDtypeStruct((g, MLA_HEADS, nt, MLA_DV, tm), BF16),
        jax.ShapeDtypeStruct((g, tlen, MLA_KV_RANK), F32),
        jax.ShapeDtypeStruct((g, tlen, MLA_ROPE), F32),
        jax.ShapeDtypeStruct(s0.shape, F32),
    ]
    return pl.pallas_call(
        functools.partial(_in_gla_body, carry=carry),
        grid=(g, nt),
        in_specs=in_specs,
        out_specs=out_specs,
        out_shape=out_shape,
        scratch_shapes=[pltpu.VMEM((tm, D_IN_PAD), F32), pltpu.VMEM((tm, GLA_KW), F32),
                        pltpu.VMEM((GLA_DV, GLA_KW), F32)],
        compiler_params=pltpu.CompilerParams(dimension_semantics=("arbitrary", "arbitrary"),
                                             vmem_limit_bytes=VMEM_LIMIT),
        name="in_gla_carry" if carry else "in_gla_batched",
    )(x, s0, cos, sin, kx, amask, *wts)


def _past_kv_body(ckv_ref, kr_ref, wk_ref, wvt_ref, sel_ref, k_ref, vt_ref):
    ckv_b = ckv_ref[...].astype(BF16)
    kr = _dot(kr_ref[...].astype(BF16), sel_ref[...])
    kn = _dot(ckv_b, wk_ref[...])
    vt = _dot_nt(wvt_ref[...], ckv_b)
    for hh in range(MLA_HEADS):
        k_ref[hh] = (kn[:, hh * HEAD_PAD:(hh + 1) * HEAD_PAD] + kr).astype(BF16)
        vt_ref[hh] = vt[hh * MLA_DV:(hh + 1) * MLA_DV, :].astype(BF16)


def _past_kv(ckv, kr, wk, wvt, sel):
    bsz, past, _ = ckv.shape
    return pl.pallas_call(
        _past_kv_body,
        grid=(bsz,),
        in_specs=[pl.BlockSpec((None, past, MLA_KV_RANK), lambda b: (b, 0, 0)),
                  pl.BlockSpec((None, past, MLA_ROPE), lambda b: (b, 0, 0)),
                  pl.BlockSpec(wk.shape, lambda b: (0, 0)),
                  pl.BlockSpec(wvt.shape, lambda b: (0, 0)),
                  pl.BlockSpec(sel.shape, lambda b: (0, 0))],
        out_specs=[pl.BlockSpec((MLA_HEADS, None, past, HEAD_PAD), lambda b: (0, b, 0, 0)),
                   pl.BlockSpec((MLA_HEADS, None, MLA_DV, past), lambda b: (0, b, 0, 0))],
        out_shape=[jax.ShapeDtypeStruct((MLA_HEADS, bsz, past, HEAD_PAD), BF16),
                   jax.ShapeDtypeStruct((MLA_HEADS, bsz, MLA_DV, past), BF16)],
        compiler_params=pltpu.CompilerParams(dimension_semantics=("arbitrary",),
                                             vmem_limit_bytes=VMEM_LIMIT),
        name="past_kv",
    )(ckv, kr, wk, wvt, sel)


def _attn_body(q_ref, k_ref, vt_ref, mz_ref, o_ref, m_ref, l_ref, acc_ref, *, q0, nsub):
    tq = q_ref.shape[0]
    tk = k_ref.shape[1]
    sq = tq // nsub
    i = pl.program_id(2)
    qstart = q0 + i * tq
    n_full = qstart // tk
    n_kv = (qstart + tq + tk - 1) // tk
    m_ref[...] = jnp.full(m_ref.shape, NEG_INF, F32)
    l_ref[...] = jnp.zeros(l_ref.shape, F32)
    acc_ref[...] = jnp.zeros(acc_ref.shape, F32)

    def make_step(masked):
        def step(j, _):
            kt = k_ref[j]
            vt = vt_ref[j]
            for u in range(nsub):
                cols = slice(u * sq, (u + 1) * sq)
                s = _dot_nt(kt, q_ref[cols, :])
                if masked:
                    kchunk = (j * tk + lax.broadcasted_iota(jnp.int32, (tk, 1), 0)) // CHUNK
                    qchunk = (qstart + u * sq
                              + lax.broadcasted_iota(jnp.int32, (1, sq), 1)) // CHUNK
                    s = jnp.where(kchunk <= qchunk, s, NEG_INF)
                m_old = m_ref[:, cols]
                m_new = jnp.maximum(m_old, jnp.max(s, axis=0, keepdims=True))
                alpha = jnp.exp2(m_old - m_new)
                p = jnp.exp2(s - m_new)
                l_ref[:, cols] = alpha * l_ref[:, cols] + jnp.sum(p, axis=0, keepdims=True)
                acc_ref[:, cols] = alpha * acc_ref[:, cols] + _dot(vt, p.astype(BF16))
                m_ref[:, cols] = m_new
            return 0
        return step

    lax.fori_loop(0, n_full, make_step(False), 0)
    lax.fori_loop(n_full, n_kv, make_step(True), 0)
    o = (acc_ref[...] / l_ref[...]).T
    o_ref[...] = (o * mz_ref[...].astype(F32)).astype(BF16)


def _attn_pipe_body(q_ref, k_ref, vt_ref, mz_ref, o_ref, q2_ref, s0_ref, s1_ref, c0_ref, c1_ref,
                    p0_ref, p1_ref, a0_ref, a1_ref, m_ref, l_ref, acc_ref):
    tq = q_ref.shape[0]
    tk = k_ref.shape[1]
    nsub = tq // SUBQ
    ndiag = tq // tk
    i = pl.program_id(2)
    all_subs = tuple(range(nsub))

    def cols(u):
        return slice(u * SUBQ, (u + 1) * SUBQ)

    def scores(j, s_ref, c_ref, subs, diagonal):
        kt = k_ref[j]
        for u in subs:
            s = _dot_nt(kt, q2_ref[diagonal, cols(u), :])
            s_ref[:, cols(u)] = s
            c_ref[:, cols(u)] = jnp.max(s, axis=0, keepdims=True)

    def softmax(s_ref, c_ref, p_ref, a_ref, subs):
        for u in subs:
            m_old = m_ref[:, cols(u)]
            m_new = jnp.maximum(m_old, c_ref[:, cols(u)])
            a_ref[:, cols(u)] = jnp.exp2(m_old - m_new)
            p_ref[:, cols(u)] = jnp.exp2((s_ref[:, cols(u)] - m_new).astype(BF16))
            m_ref[:, cols(u)] = m_new

    ones_rows = jnp.ones((ONES_ROWS, tk), BF16)

    def values(j, p_ref, a_ref, subs):
        vt1 = jnp.concatenate([vt_ref[j], ones_rows], axis=0)
        for u in subs:
            pv = _dot(vt1, p_ref[:, cols(u)])
            alpha = a_ref[:, cols(u)]
            acc_ref[:, cols(u)] = alpha * acc_ref[:, cols(u)] + pv[:MLA_DV, :]
            l_ref[:, cols(u)] = alpha * l_ref[:, cols(u)] + pv[MLA_DV:MLA_DV + 1, :]

    m_ref[...] = jnp.full(m_ref.shape, NEG_INF, F32)
    l_ref[...] = jnp.zeros(l_ref.shape, F32)
    acc_ref[...] = jnp.zeros(acc_ref.shape, F32)
    p1_ref[...] = jnp.zeros(p1_ref.shape, BF16)
    a1_ref[...] = jnp.ones(a1_ref.shape, F32)
    q = q_ref[...]
    lane = lax.broadcasted_iota(jnp.int32, q.shape, 1) - MASK_LANE0
    qchunk = lax.broadcasted_iota(jnp.int32, q.shape, 0) // CHUNK
    hidden = (lane > qchunk) & (lane < Q_TILE_CHUNKS)
    q2_ref[0] = q
    q2_ref[1] = jnp.where(hidden, NEG_INF, q.astype(F32)).astype(BF16)
    slots = ((s0_ref, c0_ref, p0_ref, a0_ref), (s1_ref, c1_ref, p1_ref, a1_ref))
    n_plain = ndiag * i
    scores(0, s0_ref, c0_ref, all_subs, (i == 0).astype(jnp.int32))

    def pair(u, _):
        scores(2 * u + 1, s1_ref, c1_ref, all_subs, 0)
        softmax(s0_ref, c0_ref, p0_ref, a0_ref, all_subs)
        values(jnp.maximum(2 * u - 1, 0), p1_ref, a1_ref, all_subs)
        scores(2 * u + 2, s0_ref, c0_ref, all_subs, (2 * u + 2 == n_plain).astype(jnp.int32))
        softmax(s1_ref, c1_ref, p1_ref, a1_ref, all_subs)
        values(2 * u, p0_ref, a0_ref, all_subs)
        return 0

    lax.fori_loop(0, n_plain // 2, pair, 0)
    diag_subs = [tuple(u for u in all_subs if (u + 1) * SUBQ > d * tk) for d in range(ndiag)]
    for d in range(ndiag):
        s_ref, c_ref, p_ref, a_ref = slots[d % 2]
        s_nxt, c_nxt, p_prv, a_prv = slots[(d + 1) % 2]
        if d + 1 < ndiag:
            scores(n_plain + d + 1, s_nxt, c_nxt, diag_subs[d + 1], 1)
        softmax(s_ref, c_ref, p_ref, a_ref, diag_subs[d])
        values(jnp.maximum(n_plain + d - 1, 0), p_prv, a_prv,
               all_subs if d == 0 else diag_subs[d - 1])
    _, _, p_ref, a_ref = slots[(ndiag - 1) % 2]
    values(n_plain + ndiag - 1, p_ref, a_ref, diag_subs[ndiag - 1])
    o = (acc_ref[...] / l_ref[...]).T
    o_ref[...] = (o * mz_ref[...].astype(F32)).astype(BF16)


def _attention_pipelined(q, k, vt, mz):
    bsz, nh, tlen, _ = q.shape
    nkt, tk = k.shape[2], k.shape[3]
    tq = Q_TILE_CHUNKS * CHUNK
    assert tq % (2 * tk) == 0 and tlen % tq == 0
    kmap = lambda b, h, i: (b, h, 0, 0, 0)
    omap = lambda b, h, i: (b, i, h)
    stat = pltpu.VMEM((1, tq), F32)
    return pl.pallas_call(
        _attn_pipe_body,
        grid=(bsz, nh, tlen // tq),
        in_specs=[pl.BlockSpec((None, None, tq, HEAD_PAD), lambda b, h, i: (b, h, i, 0)),
                  pl.BlockSpec((None, None, nkt, tk, HEAD_PAD), kmap),
                  pl.BlockSpec((None, None, nkt, MLA_DV, tk), kmap),
                  pl.BlockSpec((None, tq, MLA_DV), omap)],
        out_specs=pl.BlockSpec((None, tq, MLA_DV), omap),
        out_shape=jax.ShapeDtypeStruct(mz.shape, BF16),
        scratch_shapes=[pltpu.VMEM((2, tq, HEAD_PAD), BF16),
                        pltpu.VMEM((tk, tq), F32), pltpu.VMEM((tk, tq), F32), stat, stat,
                        pltpu.VMEM((tk, tq), BF16), pltpu.VMEM((tk, tq), BF16), stat, stat,
                        stat, stat, pltpu.VMEM((MLA_DV, tq), F32)],
        compiler_params=pltpu.CompilerParams(
            dimension_semantics=("arbitrary", "arbitrary", "arbitrary"),
            vmem_limit_bytes=VMEM_LIMIT),
        name="mla_attn_pipe",
    )(q, k, vt, mz)


def _attention(q, k, vt, mz, *, grid, qmap, kmap, omap, tq, q0, nsub):
    nkt, tk = k.shape[2], k.shape[3]
    return pl.pallas_call(
        functools.partial(_attn_body, q0=q0, nsub=nsub),
        grid=grid,
        in_specs=[pl.BlockSpec((None, None, tq, HEAD_PAD), qmap),
                  pl.BlockSpec((None, None, nkt, tk, HEAD_PAD), kmap),
                  pl.BlockSpec((None, None, nkt, MLA_DV, tk), kmap),
                  pl.BlockSpec((None, tq, MLA_DV), omap)],
        out_specs=pl.BlockSpec((None, tq, MLA_DV), omap),
        out_shape=jax.ShapeDtypeStruct(mz.shape, BF16),
        scratch_shapes=[pltpu.VMEM((1, tq), F32), pltpu.VMEM((1, tq), F32),
                        pltpu.VMEM((MLA_DV, tq), F32)],
        compiler_params=pltpu.CompilerParams(
            dimension_semantics=("arbitrary", "arbitrary", "arbitrary"),
            vmem_limit_bytes=VMEM_LIMIT),
        name="mla_attn",
    )(q, k, vt, mz)


def _out_mem_body(x_ref, ogla_ref, omla_ref, mk_ref, mv_ref, wout_ref, gq_ref, wmq_ref, wmo_ref,
                  gfin_ref, y_ref, o_scr):
    nb = mk_ref.shape[0]
    rb = x_ref.shape[0] // nb
    x1 = (x_ref[...] + _dot(ogla_ref[...], wout_ref[0:GLA_WIDTH, :])
          + _dot(omla_ref[...], wout_ref[GLA_WIDTH:GLA_WIDTH + MLA_WIDTH, :]))
    h = _rms(x1, gq_ref[...]).astype(BF16)
    qm = (_dot(h, wmq_ref[...]) * (MEM_DH ** -0.5)).astype(BF16)
    for bb in range(nb):
        rows = slice(bb * rb, (bb + 1) * rb)
        for hh in range(MEM_HEADS):
            cols = slice(hh * MEM_DH, (hh + 1) * MEM_DH)
            s = _dot_nt(qm[rows, cols], mk_ref[bb, :, cols].astype(BF16))
            p = jnp.exp(s - jnp.max(s, axis=1, keepdims=True))
            p = p / jnp.sum(p, axis=1, keepdims=True)
            o_scr[rows, cols] = _dot(p.astype(BF16), mv_ref[bb, :, cols].astype(BF16)).astype(BF16)
    x2 = x1 + _dot(o_scr[...], wmo_ref[...])
    y_ref[...] = _rms(x2, gfin_ref[...])


def _out_mem(x, ogla, omla, mk, mv, wts, *, tm, nb):
    g, tlen, _ = x.shape
    nt = tlen // tm
    mmap = (lambda b, t: (b, 0, 0)) if nb == 1 else (lambda b, t: (t, 0, 0))
    row = lambda w: pl.BlockSpec((None, tm, w), lambda b, t: (b, t, 0))
    mspec = pl.BlockSpec((nb, N_MEM, D_MODEL), mmap)
    sq = pl.BlockSpec((D_MODEL, D_MODEL), lambda b, t: (0, 0))
    vec = pl.BlockSpec((1, D_MODEL), lambda b, t: (0, 0))
    return pl.pallas_call(
        _out_mem_body,
        grid=(g, nt),
        in_specs=[row(D_MODEL), row(GLA_WIDTH), row(MLA_WIDTH), mspec, mspec, sq, vec, sq, sq, vec],
        out_specs=row(D_MODEL),
        out_shape=jax.ShapeDtypeStruct(x.shape, F32),
        scratch_shapes=[pltpu.VMEM((tm, D_MODEL), BF16)],
        compiler_params=pltpu.CompilerParams(dimension_semantics=("arbitrary", "arbitrary"),
                                             vmem_limit_bytes=VMEM_LIMIT),
        name="out_mem",
    )(x, ogla, omla, mk, mv, *wts)


def _prep_w_in(w_in):
    splits = np.cumsum([GLA_KW, GLA_KW, GLA_WIDTH, GLA_GATE_RANK, GLA_WIDTH, MLA_Q_RANK,
                        MLA_KV_RANK, MLA_ROPE])
    gq, gk, gv, glr, gz, cq, ckv, kr, mz = jnp.split(w_in, splits.tolist(), axis=1)
    zeros = lambda n: jnp.zeros((D_MODEL, n), w_in.dtype)
    misc = jnp.concatenate([glr, zeros(ROPE_LANE0 - GLA_GATE_RANK), kr,
                            zeros(LANES - ROPE_LANE0 - MLA_ROPE)], axis=1)
    return jnp.concatenate([gq, gk, gv, gz, cq, ckv, mz, misc], axis=1).astype(BF16)


def _pad_heads(w, width):
    kdim = w.shape[0]
    w = w.reshape(kdim, MLA_HEADS, width)
    w = jnp.pad(w, ((0, 0), (0, 0), (0, HEAD_PAD - width)))
    return w.reshape(kdim, MLA_HEADS * HEAD_PAD)


def _rope_tables(pos):
    inv = ROPE_THETA ** (-jnp.arange(HALF_ROPE, dtype=F32) / HALF_ROPE)
    ang = pos.astype(F32)[:, None] * inv[None, :]
    c, s = jnp.cos(ang), jnp.sin(ang)
    n = pos.shape[0]
    cos = jnp.concatenate([jnp.ones((n, ROPE_LANE0), F32), c, c,
                           jnp.zeros((n, LANES - ROPE_LANE0 - MLA_ROPE), F32)], axis=1)
    sin = jnp.concatenate([jnp.zeros((n, ROPE_LANE0), F32), -s, s,
                           jnp.zeros((n, LANES - ROPE_LANE0 - MLA_ROPE), F32)], axis=1)
    return cos, sin


def _chunk_onehot(pos):
    cidx = (np.asarray(pos) // CHUNK) % Q_TILE_CHUNKS
    tab = np.zeros((len(cidx), LANES), np.float32)
    tab[np.arange(len(cidx)), MASK_LANE0 + cidx] = 1.0
    return jnp.asarray(tab)


def _gla_mask():
    r = np.arange(GLA_HEADS * CHUNK)
    same_head = (r[:, None] // CHUNK) == (r[None, :] // CHUNK)
    causal = (r[None, :] % CHUNK) <= (r[:, None] % CHUNK)
    return jnp.asarray((same_head & causal).astype(np.float32))


def kernel(x_prompt, x_sample, mem_prompt, cache_mla_ckv, cache_mla_krope, state_gla, cache_mem_k, cache_mem_v, g_mix, w_in, w_gla_g2, b_gla_g, g_gla_out, g_qa, w_qb, g_kva, w_kvb, w_out, g_mem_q, g_mem_kv, w_mq, w_mk, w_mv, w_mo, g_final):
    bp, tp, _ = x_prompt.shape
    bs, ts, _ = x_sample.shape
    past = cache_mla_ckv.shape[2]
    assert ts == CHUNK and g_mix.shape[0] == 1
    l = 0
    row = lambda g: g.reshape(1, -1)

    w_in_p = _prep_w_in(w_in[l])
    wg2_p = jnp.pad(w_gla_g2[l], ((0, LANES - GLA_GATE_RANK), (0, 0))).astype(BF16)
    wqb_p = _pad_heads(w_qb[l], MLA_NOPE + MLA_ROPE).astype(BF16)
    wkv = w_kvb[l].reshape(MLA_KV_RANK, MLA_HEADS, MLA_NOPE + MLA_DV)
    wk_p = _pad_heads(wkv[:, :, :MLA_NOPE].reshape(MLA_KV_RANK, -1), MLA_NOPE).astype(BF16)
    wvt_p = wkv[:, :, MLA_NOPE:].reshape(MLA_KV_RANK, MLA_WIDTH).T.astype(BF16)
    in_wts = (row(g_mix[l]), w_in_p, wg2_p, row(b_gla_g[l]), row(g_gla_out[l]), row(g_qa[l]),
              wqb_p, row(g_kva[l]), wk_p, wvt_p)
    out_wts = (w_out[l].astype(BF16), row(g_mem_q[l]), w_mq[l].astype(BF16),
               w_mo[l].astype(BF16), row(g_final))
    amask = _gla_mask()
    sel = jnp.zeros((MLA_ROPE, HEAD_PAD), BF16).at[
        jnp.arange(MLA_ROPE), ROPE_LANE0 + jnp.arange(MLA_ROPE)].set(1.0)

    tm = 512
    sample_rows = bs * ts
    bpt = tm // ts

    mk_p, mv_p = _mem_kv(mem_prompt.reshape(bp * N_MEM, D_MODEL), row(g_mem_kv[l]),
                         w_mk[l].astype(BF16), w_mv[l].astype(BF16), tm)
    cos_p, sin_p = _rope_tables(jnp.arange(tp, dtype=jnp.int32))
    s0_p = jnp.zeros((bp, GLA_KW, GLA_DV), F32)
    ogla_p, mz_p, q_p, k_p, vt_p, ckv_p, kr_p, st_p = _in_gla(
        x_prompt, s0_p, cos_p, sin_p, _chunk_onehot(np.arange(tp)), amask, in_wts, tm=tm, carry=True)
    omla_p = _attention_pipelined(
        q_p, k_p.reshape(bp, MLA_HEADS, tp // tm, tm, HEAD_PAD), vt_p, mz_p)
    y_prompt = _out_mem(x_prompt, ogla_p, omla_p, mk_p.reshape(bp, N_MEM, D_MODEL),
                        mv_p.reshape(bp, N_MEM, D_MODEL), out_wts, tm=tm, nb=1)

    xs = x_sample.reshape(1, sample_rows, D_MODEL)
    cos_s, sin_s = _rope_tables(past + jnp.arange(ts, dtype=jnp.int32))
    cos_s, sin_s = jnp.tile(cos_s, (bpt, 1)), jnp.tile(sin_s, (bpt, 1))
    s0_s = state_gla[l].reshape(bs, GLA_KW, GLA_DV)
    ogla_s, mz_s, q_s, k_s, vt_s, ckv_s, kr_s, st_s = _in_gla(
        xs, s0_s, cos_s, sin_s, jnp.zeros((tm, LANES), F32), amask, in_wts, tm=tm, carry=False)
    k_past, vt_past = _past_kv(cache_mla_ckv[l], cache_mla_krope[l], wk_p, wvt_p, sel)
    nkeys = past + ts
    kpad = -nkeys % LANES
    per_batch = lambda a: a.reshape(MLA_HEADS, bs, ts, HEAD_PAD)
    k_all = jnp.concatenate(
        [k_past, per_batch(k_s), jnp.zeros((MLA_HEADS, bs, kpad, HEAD_PAD), BF16)], axis=2)
    vt_new = vt_s.reshape(MLA_HEADS, sample_rows // tm, MLA_DV, bpt, ts)
    vt_new = vt_new.transpose(0, 1, 3, 2, 4).reshape(MLA_HEADS, bs, MLA_DV, ts)
    vt_all = jnp.concatenate(
        [vt_past, vt_new, jnp.zeros((MLA_HEADS, bs, MLA_DV, kpad), BF16)], axis=3)
    omla_s = _attention(
        per_batch(q_s), k_all[:, :, None], vt_all[:, :, None], mz_s, grid=(bs, MLA_HEADS, 1),
        qmap=lambda b, h, i: (h, b, 0, 0), kmap=lambda b, h, i: (h, b, 0, 0, 0),
        omap=lambda b, h, i: (0, b, h), tq=ts, q0=past, nsub=1)
    nb_s = 4
    y_sample = _out_mem(xs, ogla_s, omla_s, cache_mem_k[l].reshape(bs, N_MEM, D_MODEL),
                        cache_mem_v[l].reshape(bs, N_MEM, D_MODEL), out_wts,
                        tm=nb_s * ts, nb=nb_s)

    mem_shape = (1, bp, N_MEM, MEM_HEADS, MEM_DH)
    st_shape = (GLA_HEADS, GLA_DK, GLA_DV)
    return (y_prompt, y_sample.reshape(bs, ts, D_MODEL),
            ckv_p[None], kr_p[None], st_p.reshape(1, bp, *st_shape),
            mk_p.reshape(mem_shape), mv_p.reshape(mem_shape),
            ckv_s.reshape(1, bs, ts, MLA_KV_RANK), kr_s.reshape(1, bs, ts, MLA_ROPE),
            st_s.reshape(1, bs, *st_shape))
```

```python
import functools

import jax
import jax.numpy as jnp
import numpy as np
from jax import lax
from jax.experimental import pallas as pl
from jax.experimental.pallas import tpu as pltpu

D_MODEL = 1024
CHUNK = 64
EPS = 1e-6
NEG_INF = -1e30

GLA_HEADS = 4
GLA_DK = 64
GLA_DV = 128
GLA_GATE_RANK = 16
GLA_GATE_NORM = 16.0
GLA_KW = GLA_HEADS * GLA_DK
GLA_WIDTH = GLA_HEADS * GLA_DV

MLA_HEADS = 4
MLA_Q_RANK = 256
MLA_KV_RANK = 128
MLA_NOPE = 64
MLA_ROPE = 32
MLA_DV = 128
MLA_WIDTH = MLA_HEADS * MLA_DV
MLA_SCALE = (MLA_NOPE + MLA_ROPE) ** -0.5
ROPE_THETA = 10000.0
HALF_ROPE = MLA_ROPE // 2
Q_PRESCALE = MLA_SCALE * float(np.log2(np.e))
SUBQ = 256
MASK_LANE0 = MLA_NOPE + MLA_ROPE
Q_TILE_CHUNKS = 128 - MASK_LANE0
ROW_SPLITS = 2
ONES_ROWS = 16

MEM_HEADS = 4
MEM_DH = D_MODEL // MEM_HEADS
N_MEM = 256

LANES = 128
HEAD_PAD = 128
ROPE_LANE0 = MLA_NOPE

COL_Q = 0
COL_K = COL_Q + GLA_KW
COL_V = COL_K + GLA_KW
COL_Z = COL_V + GLA_WIDTH
COL_CQ = COL_Z + GLA_WIDTH
COL_CKV = COL_CQ + MLA_Q_RANK
COL_MZ = COL_CKV + MLA_KV_RANK
COL_MISC = COL_MZ + MLA_WIDTH
D_IN_PAD = COL_MISC + LANES

VMEM_LIMIT = 52 * 1024 * 1024

BF16 = jnp.bfloat16
F32 = jnp.float32


def _rms(x, g):
    ms = jnp.mean(x * x, axis=-1, keepdims=True)
    return x * lax.rsqrt(ms + EPS) * g


def _dot(a, b):
    return jnp.dot(a, b, preferred_element_type=F32)


def _dot_nt(a, b):
    return lax.dot_general(a, b, (((1,), (1,)), ((), ())), preferred_element_type=F32)


def _dot_tn(a, b):
    return lax.dot_general(a, b, (((0,), (0,)), ((), ())), preferred_element_type=F32)


def _rope_group(x, cos, sin):
    lane = lax.broadcasted_iota(jnp.int32, x.shape, 1)
    partner = jnp.where(lane < ROPE_LANE0 + HALF_ROPE,
                        pltpu.roll(x, LANES - HALF_ROPE, 1),
                        pltpu.roll(x, HALF_ROPE, 1))
    return x * cos + partner * sin


def _mem_kv_body(mem_ref, g_ref, wk_ref, wv_ref, mk_ref, mv_ref):
    m = _rms(mem_ref[...], g_ref[...]).astype(BF16)
    mk_ref[...] = _dot(m, wk_ref[...])
    mv_ref[...] = _dot(m, wv_ref[...])


def _mem_kv(mem2d, g, wk, wv, tm):
    rows = mem2d.shape[0]
    wspec = pl.BlockSpec((D_MODEL, D_MODEL), lambda i: (0, 0))
    ospec = pl.BlockSpec((tm, D_MODEL), lambda i: (i, 0))
    return pl.pallas_call(
        _mem_kv_body,
        grid=(rows // tm,),
        in_specs=[pl.BlockSpec((tm, D_MODEL), lambda i: (i, 0)),
                  pl.BlockSpec((1, D_MODEL), lambda i: (0, 0)), wspec, wspec],
        out_specs=[ospec, ospec],
        out_shape=[jax.ShapeDtypeStruct((rows, D_MODEL), F32)] * 2,
        compiler_params=pltpu.CompilerParams(dimension_semantics=("arbitrary",),
                                             vmem_limit_bytes=VMEM_LIMIT),
        name="mem_kv",
    )(mem2d, g, wk, wv)


def _in_gla_body(x_ref, s0_ref, cos_ref, sin_ref, kx_ref, amask_ref, gmix_ref, win_ref, wg2_ref, bg_ref,
                 ggla_ref, gqa_ref, wqb_ref, gkva_ref, wk_ref, wvt_ref,
                 ogla_ref, mz_ref, q_ref, k_ref, vt_ref, ckv_ref, kr_ref, sfin_ref,
                 proj_ref, la_ref, st_ref, *, carry):
    tm = x_ref.shape[0]
    t = pl.program_id(1)

    if carry:
        @pl.when(t == 0)
        def _():
            st_ref[...] = s0_ref[0].T

    def prep(rows):
        h = _rms(x_ref[rows, :], gmix_ref[...]).astype(BF16)
        proj_ref[rows, :] = _dot(h, win_ref[...])
        cos = cos_ref[rows, :]
        sin = sin_ref[rows, :]

        cqn = _rms(proj_ref[rows, COL_CQ:COL_CQ + MLA_Q_RANK], gqa_ref[...]).astype(BF16)
        qf = _dot(cqn, wqb_ref[...])
        for hh in range(MLA_HEADS):
            qh = _rope_group(qf[:, hh * HEAD_PAD:(hh + 1) * HEAD_PAD], cos, sin)
            q_ref[hh, rows, :] = (qh * Q_PRESCALE).astype(BF16)

        ckv = _rms(proj_ref[rows, COL_CKV:COL_CKV + MLA_KV_RANK], gkva_ref[...])
        ckv_ref[rows, :] = ckv
        ckv_b = ckv.astype(BF16)
        misc = proj_ref[rows, COL_MISC:COL_MISC + LANES]
        lane = lax.broadcasted_iota(jnp.int32, misc.shape, 1)
        kr = jnp.where(lane >= ROPE_LANE0, _rope_group(misc, cos, sin), 0.0)
        kr_ref[rows, :] = kr[:, ROPE_LANE0:ROPE_LANE0 + MLA_ROPE]
        kr = kr + kx_ref[rows, :]
        kn = _dot(ckv_b, wk_ref[...])
        vt = _dot_nt(wvt_ref[...], ckv_b)
        for hh in range(MLA_HEADS):
            k_ref[hh, rows, :] = (kn[:, hh * HEAD_PAD:(hh + 1) * HEAD_PAD] + kr).astype(BF16)
            vt_ref[hh, :, rows] = vt[hh * MLA_DV:(hh + 1) * MLA_DV, :].astype(BF16)

        mz = proj_ref[rows, COL_MZ:COL_MZ + MLA_WIDTH]
        mz_ref[rows, :] = (mz * jax.nn.sigmoid(mz)).astype(BF16)

        z = _dot(misc.astype(BF16), wg2_ref[...]) + bg_ref[...]
        la_ref[rows, :] = ((jnp.minimum(z, 0.0) - jnp.log1p(jnp.exp(-jnp.abs(z))))
                           * (1.0 / GLA_GATE_NORM))

    for r in range(ROW_SPLITS):
        prep(slice(r * tm // ROW_SPLITS, (r + 1) * tm // ROW_SPLITS))

    klane = lax.broadcasted_iota(jnp.int32, (CHUNK, GLA_KW), 1)
    head_of_lane = klane // GLA_DK
    ri = lax.broadcasted_iota(jnp.int32, (CHUNK, CHUNK), 0)
    ci = lax.broadcasted_iota(jnp.int32, (CHUNK, CHUNK), 1)
    tril = (ci <= ri).astype(BF16)
    tril3 = jnp.concatenate([tril, tril, tril], axis=1)
    g_out = ggla_ref[...]

    def chunk(c, st):
        rows = slice(c * CHUNK, (c + 1) * CHUNK)
        if not carry:
            st = s0_ref[c].T
        q = proj_ref[rows, COL_Q:COL_Q + GLA_KW] * (GLA_DK ** -0.5)
        k = proj_ref[rows, COL_K:COL_K + GLA_KW]
        v = proj_ref[rows, COL_V:COL_V + GLA_WIDTH]
        la = la_ref[rows, :]
        la_hi = la.astype(BF16)
        r1 = la - la_hi.astype(F32)
        la_mid = r1.astype(BF16)
        la_lo = (r1 - la_mid.astype(F32)).astype(BF16)
        b = _dot(tril3, jnp.concatenate([la_hi, la_mid, la_lo], axis=0))
        b_last = b[CHUNK - 1:CHUNK, :]
        q_dec = q * jnp.exp(b)
        k_dec = k * jnp.exp(-b)
        kd2 = k * jnp.exp(b_last - b)
        sdec = jnp.exp(b_last)

        def per_head(a):
            return jnp.concatenate(
                [jnp.where(head_of_lane == hh, a, 0.0) for hh in range(GLA_HEADS)], axis=0)

        qs = per_head(q_dec).astype(BF16)
        kt = jnp.concatenate([k_dec.astype(BF16)] * GLA_HEADS, axis=0)
        a = jnp.where(amask_ref[...] != 0.0, _dot_nt(qs, kt), 0.0).astype(BF16)
        vs = jnp.concatenate([v[:, hh * GLA_DV:(hh + 1) * GLA_DV] for hh in range(GLA_HEADS)],
                             axis=0).astype(BF16)
        o = _dot_nt(qs, st.astype(BF16)) + _dot(a, vs)
        st = st * sdec + _dot_tn(vs, per_head(kd2).astype(BF16))
        gz = proj_ref[rows, COL_Z:COL_Z + GLA_WIDTH]
        gate = gz * jax.nn.sigmoid(gz)
        on = _rms(o, g_out)
        for hh in range(GLA_HEADS):
            cols = slice(hh * GLA_DV, (hh + 1) * GLA_DV)
            ogla_ref[rows, cols] = (on[hh * CHUNK:(hh + 1) * CHUNK, :] * gate[:, cols]).astype(BF16)
        if not carry:
            sfin_ref[c] = st.T
        return st

    st = st_ref[...] if carry else None
    for c in range(tm // CHUNK):
        st = chunk(c, st)

    if carry:
        st_ref[...] = st

        @pl.when(t == pl.num_programs(1) - 1)
        def _():
            sfin_ref[0] = st.T


def _in_gla(x, s0, cos, sin, kx, amask, wts, *, tm, carry):
    g, tlen, _ = x.shape
    nt = tlen // tm
    ns = 1 if carry else tm // CHUNK
    smap = (lambda b, t: (b, 0, 0)) if carry else (lambda b, t: (t, 0, 0))
    tab_tiles = cos.shape[0] // tm
    tmap = (lambda b, t: (t, 0)) if tab_tiles > 1 else (lambda b, t: (0, 0))

    def const(shape):
        return pl.BlockSpec(shape, lambda b, t: (0,) * len(shape))

    row = lambda w: pl.BlockSpec((None, tm, w), lambda b, t: (b, t, 0))
    heads = pl.BlockSpec((None, MLA_HEADS, tm, HEAD_PAD), lambda b, t: (b, 0, t, 0))
    in_specs = [
        row(D_MODEL),
        pl.BlockSpec((ns, GLA_KW, GLA_DV), smap),
        pl.BlockSpec((tm, LANES), tmap), pl.BlockSpec((tm, LANES), tmap),
        pl.BlockSpec((tm, LANES), tmap),
        const((GLA_HEADS * CHUNK, GLA_HEADS * CHUNK)),
        const((1, D_MODEL)), const((D_MODEL, D_IN_PAD)), const((LANES, GLA_KW)), const((1, GLA_KW)),
        const((1, GLA_DV)), const((1, MLA_Q_RANK)), const((MLA_Q_RANK, MLA_HEADS * HEAD_PAD)),
        const((1, MLA_KV_RANK)), const((MLA_KV_RANK, MLA_HEADS * HEAD_PAD)),
        const((MLA_WIDTH, MLA_KV_RANK)),
    ]
    vt_spec = pl.BlockSpec((None, MLA_HEADS, None, MLA_DV, tm), lambda b, t: (b, 0, t, 0, 0))
    out_specs = [row(GLA_WIDTH), row(MLA_WIDTH), heads, heads, vt_spec, row(MLA_KV_RANK),
                 row(MLA_ROPE), pl.BlockSpec((ns, GLA_KW, GLA_DV), smap)]
    hshape = jax.ShapeDtypeStruct((g, MLA_HEADS, tlen, HEAD_PAD), BF16)
    out_shape = [
        jax.ShapeDtypeStruct((g, tlen, GLA_WIDTH), BF16),
        jax.ShapeDtypeStruct((g, tlen, MLA_WIDTH), BF16),
        hshape, hshape, jax.ShapeDtypeStruct((g, MLA_HEADS, nt, MLA_DV, tm), BF16),
        jax.ShapeDtypeStruct((g, tlen, MLA_KV_RANK), F32),
        jax.ShapeDtypeStruct((g, tlen, MLA_ROPE), F32),
        jax.ShapeDtypeStruct(s0.shape, F32),
    ]
    return pl.pallas_call(
        functools.partial(_in_gla_body, carry=carry),
        grid=(g, nt),
        in_specs=in_specs,
        out_specs=out_specs,
        out_shape=out_shape,
        scratch_shapes=[pltpu.VMEM((tm, D_IN_PAD), F32), pltpu.VMEM((tm, GLA_KW), F32),
                        pltpu.VMEM((GLA_DV, GLA_KW), F32)],
        compiler_params=pltpu.CompilerParams(dimension_semantics=("arbitrary", "arbitrary"),
                                             vmem_limit_bytes=VMEM_LIMIT),
        name="in_gla_carry" if carry else "in_gla_batched",
    )(x, s0, cos, sin, kx, amask, *wts)


def _latent_attn_body(q_ref, ckvn_ref, krn_ref, ckvc_ref, krc_ref, mz_ref, wk_ref, wvt_ref,
                      sel_ref, o_ref, *, past):
    ts = q_ref.shape[1]
    nk = past + ts
    ckv = jnp.concatenate([ckvc_ref[...], ckvn_ref[...]], axis=0).astype(BF16)
    kr = jnp.concatenate([krc_ref[...], krn_ref[...]], axis=0).astype(BF16)
    krp = _dot(kr, sel_ref[...]).astype(BF16)
    keys = jnp.concatenate([ckv, krp], axis=1)
    qcat = jnp.concatenate(
        [jnp.concatenate([_dot_nt(q_ref[hh], wk_ref[:, hh * HEAD_PAD:(hh + 1) * HEAD_PAD])
                          .astype(BF16), q_ref[hh]], axis=1) for hh in range(MLA_HEADS)],
        axis=0)
    s = _dot_nt(keys, qcat)
    kchunk = lax.broadcasted_iota(jnp.int32, (nk, 1), 0) // CHUNK
    qpos = past + lax.broadcasted_iota(jnp.int32, (1, MLA_HEADS * ts), 1) % ts
    s = jnp.where(kchunk <= qpos // CHUNK, s, NEG_INF)
    p = jnp.exp2(s - jnp.max(s, axis=0, keepdims=True)).astype(BF16)
    vals = jnp.concatenate([ckv, jnp.ones((nk, LANES), BF16)], axis=1)
    ol = _dot_tn(p, vals)
    olat = (ol[:, :MLA_KV_RANK] / ol[:, MLA_KV_RANK:]).astype(BF16)
    for hh in range(MLA_HEADS):
        cols = slice(hh * MLA_DV, (hh + 1) * MLA_DV)
        o = _dot_nt(olat[hh * ts:(hh + 1) * ts, :], wvt_ref[cols, :])
        o_ref[:, cols] = (o * mz_ref[:, cols].astype(F32)).astype(BF16)


def _latent_attention(q, ckv_new, kr_new, ckv_cache, kr_cache, mz, wk, wvt, sel):
    bsz, past, _ = ckv_cache.shape
    ts = q.shape[2]
    per_b = lambda r, w: pl.BlockSpec((None, r, w), lambda b: (b, 0, 0))
    const = lambda a: pl.BlockSpec(a.shape, lambda b: (0,) * a.ndim)
    return pl.pallas_call(
        functools.partial(_latent_attn_body, past=past),
        grid=(bsz,),
        in_specs=[pl.BlockSpec((MLA_HEADS, None, ts, HEAD_PAD), lambda b: (0, b, 0, 0)),
                  per_b(ts, MLA_KV_RANK), per_b(ts, MLA_ROPE),
                  per_b(past, MLA_KV_RANK), per_b(past, MLA_ROPE), per_b(ts, MLA_WIDTH),
                  const(wk), const(wvt), const(sel)],
        out_specs=per_b(ts, MLA_WIDTH),
        out_shape=jax.ShapeDtypeStruct(mz.shape, BF16),
        compiler_params=pltpu.CompilerParams(dimension_semantics=("arbitrary",),
                                             vmem_limit_bytes=VMEM_LIMIT),
        name="latent_attn",
    )(q, ckv_new, kr_new, ckv_cache, kr_cache, mz, wk, wvt, sel)


def _attn_pipe_body(q_ref, k_ref, vt_ref, mz_ref, o_ref, q2_ref, s0_ref, s1_ref, c0_ref, c1_ref,
                    p0_ref, p1_ref, a0_ref, a1_ref, m_ref, l_ref, acc_ref):
    tq = s0_ref.shape[1]
    tk = k_ref.shape[1]
    nsub = tq // SUBQ
    ndiag = tq // tk
    nq = q_ref.shape[0] // tq
    all_subs = tuple(range(nsub))
    diag_subs = [tuple(u for u in all_subs if (u + 1) * SUBQ > d * tk) for d in range(ndiag)]
    slots = ((s0_ref, c0_ref, p0_ref, a0_ref), (s1_ref, c1_ref, p1_ref, a1_ref))

    def cols(u):
        return slice(u * SUBQ, (u + 1) * SUBQ)

    def scores(j, slot, subs, diagonal):
        s_ref, c_ref, _, _ = slot
        kt = k_ref[j]
        for u in subs:
            s = _dot_nt(kt, q2_ref[diagonal, cols(u), :])
            s_ref[:, cols(u)] = s
            c_ref[:, cols(u)] = jnp.max(s, axis=0, keepdims=True)

    def softmax(slot, subs):
        s_ref, c_ref, p_ref, a_ref = slot
        for u in subs:
            m_old = m_ref[:, cols(u)]
            m_new = jnp.maximum(m_old, c_ref[:, cols(u)])
            a_ref[:, cols(u)] = jnp.exp2(m_old - m_new)
            p_ref[:, cols(u)] = jnp.exp2((s_ref[:, cols(u)] - m_new).astype(BF16))
            m_ref[:, cols(u)] = m_new

    ones_rows = jnp.ones((ONES_ROWS, tk), BF16)

    def values(j, slot, subs):
        _, _, p_ref, a_ref = slot
        vt1 = jnp.concatenate([vt_ref[j], ones_rows], axis=0)
        for u in subs:
            pv = _dot(vt1, p_ref[:, cols(u)])
            alpha = a_ref[:, cols(u)]
            acc_ref[:, cols(u)] = alpha * acc_ref[:, cols(u)] + pv[:MLA_DV, :]
            l_ref[:, cols(u)] = alpha * l_ref[:, cols(u)] + pv[MLA_DV:MLA_DV + 1, :]

    def start_tile(i):
        q = q_ref[i * tq:(i + 1) * tq, :]
        lane = lax.broadcasted_iota(jnp.int32, q.shape, 1) - MASK_LANE0
        qchunk = lax.broadcasted_iota(jnp.int32, q.shape, 0) // CHUNK
        q2_ref[0] = q
        q2_ref[1] = jnp.where(lane > qchunk, NEG_INF, q.astype(F32)).astype(BF16)
        scores(0, slots[0], all_subs, 1 if i == 0 else 0)

    def pair(u, n_plain):
        scores(2 * u + 1, slots[1], all_subs, 0)
        softmax(slots[0], all_subs)
        values(2 * u - 1, slots[1], all_subs)
        scores(2 * u + 2, slots[0], all_subs, (2 * u + 2 == n_plain).astype(jnp.int32))
        softmax(slots[1], all_subs)
        values(2 * u, slots[0], all_subs)

    start_tile(0)
    for i in range(nq):
        n_plain = ndiag * i
        m_ref[...] = jnp.full(m_ref.shape, NEG_INF, F32)
        l_ref[...] = jnp.zeros(l_ref.shape, F32)
        acc_ref[...] = jnp.zeros(acc_ref.shape, F32)
        if n_plain > 0:
            scores(1, slots[1], all_subs, 0)
            softmax(slots[0], all_subs)
            scores(2, slots[0], all_subs, 1 if n_plain == 2 else 0)
            softmax(slots[1], all_subs)
            values(0, slots[0], all_subs)
            lax.fori_loop(1, n_plain // 2, lambda u, c: (pair(u, n_plain), c)[1], 0)
        for d in range(ndiag):
            if d + 1 < ndiag:
                scores(n_plain + d + 1, slots[(d + 1) % 2], diag_subs[d + 1], 1)
            softmax(slots[d % 2], diag_subs[d])
            if n_plain + d > 0:
                values(n_plain + d - 1, slots[(d + 1) % 2],
                       all_subs if d == 0 else diag_subs[d - 1])
        values(n_plain + ndiag - 1, slots[(ndiag - 1) % 2], diag_subs[ndiag - 1])
        if i + 1 < nq:
            start_tile(i + 1)
        rows = slice(i * tq, (i + 1) * tq)
        o = (acc_ref[...] / l_ref[...]).T
        o_ref[rows, :] = (o * mz_ref[rows, :].astype(F32)).astype(BF16)


def _attention_pipelined(q, k, vt, mz):
    bsz, nh, tlen, _ = q.shape
    nkt, tk = k.shape[2], k.shape[3]
    tq = Q_TILE_CHUNKS * CHUNK
    assert tq % (2 * tk) == 0 and tlen % tq == 0
    kmap = lambda b, h: (b, h, 0, 0, 0)
    omap = lambda b, h: (b, 0, h)
    stat = pltpu.VMEM((1, tq), F32)
    return pl.pallas_call(
        _attn_pipe_body,
        grid=(bsz, nh),
        in_specs=[pl.BlockSpec((None, None, tlen, HEAD_PAD), lambda b, h: (b, h, 0, 0)),
                  pl.BlockSpec((None, None, nkt, tk, HEAD_PAD), kmap),
                  pl.BlockSpec((None, None, nkt, MLA_DV, tk), kmap),
                  pl.BlockSpec((None, tlen, MLA_DV), omap)],
        out_specs=pl.BlockSpec((None, tlen, MLA_DV), omap),
        out_shape=jax.ShapeDtypeStruct(mz.shape, BF16),
        scratch_shapes=[pltpu.VMEM((2, tq, HEAD_PAD), BF16),
                        pltpu.VMEM((tk, tq), F32), pltpu.VMEM((tk, tq), F32), stat, stat,
                        pltpu.VMEM((tk, tq), BF16), pltpu.VMEM((tk, tq), BF16), stat, stat,
                        stat, stat, pltpu.VMEM((MLA_DV, tq), F32)],
        compiler_params=pltpu.CompilerParams(
            dimension_semantics=("arbitrary", "arbitrary"),
            vmem_limit_bytes=VMEM_LIMIT),
        name="mla_attn_pipe",
    )(q, k, vt, mz)


def _out_mem_body(x_ref, ogla_ref, omla_ref, mk_ref, mv_ref, wout_ref, gq_ref, wmq_ref, wmo_ref,
                  gfin_ref, y_ref, o_scr):
    nb = mk_ref.shape[0]
    rb = x_ref.shape[0] // nb
    x1 = (x_ref[...] + _dot(ogla_ref[...], wout_ref[0:GLA_WIDTH, :])
          + _dot(omla_ref[...], wout_ref[GLA_WIDTH:GLA_WIDTH + MLA_WIDTH, :]))
    h = _rms(x1, gq_ref[...]).astype(BF16)
    qm = (_dot(h, wmq_ref[...]) * (MEM_DH ** -0.5)).astype(BF16)
    for bb in range(nb):
        rows = slice(bb * rb, (bb + 1) * rb)
        for hh in range(MEM_HEADS):
            cols = slice(hh * MEM_DH, (hh + 1) * MEM_DH)
            s = _dot_nt(qm[rows, cols], mk_ref[bb, :, cols].astype(BF16))
            p = jnp.exp(s - jnp.max(s, axis=1, keepdims=True))
            p = p / jnp.sum(p, axis=1, keepdims=True)
            o_scr[rows, cols] = _dot(p.astype(BF16), mv_ref[bb, :, cols].astype(BF16)).astype(BF16)
    x2 = x1 + _dot(o_scr[...], wmo_ref[...])
    y_ref[...] = _rms(x2, gfin_ref[...])


def _out_mem(x, ogla, omla, mk, mv, wts, *, tm, nb):
    g, tlen, _ = x.shape
    nt = tlen // tm
    mmap = (lambda b, t: (b, 0, 0)) if nb == 1 else (lambda b, t: (t, 0, 0))
    row = lambda w: pl.BlockSpec((None, tm, w), lambda b, t: (b, t, 0))
    mspec = pl.BlockSpec((nb, N_MEM, D_MODEL), mmap)
    sq = pl.BlockSpec((D_MODEL, D_MODEL), lambda b, t: (0, 0))
    vec = pl.BlockSpec((1, D_MODEL), lambda b, t: (0, 0))
    return pl.pallas_call(
        _out_mem_body,
        grid=(g, nt),
        in_specs=[row(D_MODEL), row(GLA_WIDTH), row(MLA_WIDTH), mspec, mspec, sq, vec, sq, sq, vec],
        out_specs=row(D_MODEL),
        out_shape=jax.ShapeDtypeStruct(x.shape, F32),
        scratch_shapes=[pltpu.VMEM((tm, D_MODEL), BF16)],
        compiler_params=pltpu.CompilerParams(dimension_semantics=("arbitrary", "arbitrary"),
                                             vmem_limit_bytes=VMEM_LIMIT),
        name="out_mem",
    )(x, ogla, omla, mk, mv, *wts)


def _prep_w_in(w_in):
    splits = np.cumsum([GLA_KW, GLA_KW, GLA_WIDTH, GLA_GATE_RANK, GLA_WIDTH, MLA_Q_RANK,
                        MLA_KV_RANK, MLA_ROPE])
    gq, gk, gv, glr, gz, cq, ckv, kr, mz = jnp.split(w_in, splits.tolist(), axis=1)
    zeros = lambda n: jnp.zeros((D_MODEL, n), w_in.dtype)
    misc = jnp.concatenate([glr, zeros(ROPE_LANE0 - GLA_GATE_RANK), kr,
                            zeros(LANES - ROPE_LANE0 - MLA_ROPE)], axis=1)
    return jnp.concatenate([gq, gk, gv, gz, cq, ckv, mz, misc], axis=1).astype(BF16)


def _pad_heads(w, width):
    kdim = w.shape[0]
    w = w.reshape(kdim, MLA_HEADS, width)
    w = jnp.pad(w, ((0, 0), (0, 0), (0, HEAD_PAD - width)))
    return w.reshape(kdim, MLA_HEADS * HEAD_PAD)


def _rope_tables(pos):
    inv = ROPE_THETA ** (-jnp.arange(HALF_ROPE, dtype=F32) / HALF_ROPE)
    ang = pos.astype(F32)[:, None] * inv[None, :]
    c, s = jnp.cos(ang), jnp.sin(ang)
    n = pos.shape[0]
    cos = jnp.concatenate([jnp.ones((n, ROPE_LANE0), F32), c, c,
                           jnp.zeros((n, LANES - ROPE_LANE0 - MLA_ROPE), F32)], axis=1)
    sin = jnp.concatenate([jnp.zeros((n, ROPE_LANE0), F32), -s, s,
                           jnp.zeros((n, LANES - ROPE_LANE0 - MLA_ROPE), F32)], axis=1)
    return cos, sin


def _chunk_onehot(pos):
    cidx = (np.asarray(pos) // CHUNK) % Q_TILE_CHUNKS
    tab = np.zeros((len(cidx), LANES), np.float32)
    tab[np.arange(len(cidx)), MASK_LANE0 + cidx] = 1.0
    return jnp.asarray(tab)


def _gla_mask():
    r = np.arange(GLA_HEADS * CHUNK)
    same_head = (r[:, None] // CHUNK) == (r[None, :] // CHUNK)
    causal = (r[None, :] % CHUNK) <= (r[:, None] % CHUNK)
    return jnp.asarray((same_head & causal).astype(np.float32))


def kernel(x_prompt, x_sample, mem_prompt, cache_mla_ckv, cache_mla_krope, state_gla, cache_mem_k, cache_mem_v, g_mix, w_in, w_gla_g2, b_gla_g, g_gla_out, g_qa, w_qb, g_kva, w_kvb, w_out, g_mem_q, g_mem_kv, w_mq, w_mk, w_mv, w_mo, g_final):
    bp, tp, _ = x_prompt.shape
    bs, ts, _ = x_sample.shape
    past = cache_mla_ckv.shape[2]
    assert ts == CHUNK and g_mix.shape[0] == 1
    l = 0
    row = lambda g: g.reshape(1, -1)

    w_in_p = _prep_w_in(w_in[l])
    wg2_p = jnp.pad(w_gla_g2[l], ((0, LANES - GLA_GATE_RANK), (0, 0))).astype(BF16)
    wqb_p = _pad_heads(w_qb[l], MLA_NOPE + MLA_ROPE).astype(BF16)
    wkv = w_kvb[l].reshape(MLA_KV_RANK, MLA_HEADS, MLA_NOPE + MLA_DV)
    wk_p = _pad_heads(wkv[:, :, :MLA_NOPE].reshape(MLA_KV_RANK, -1), MLA_NOPE).astype(BF16)
    wvt_p = wkv[:, :, MLA_NOPE:].reshape(MLA_KV_RANK, MLA_WIDTH).T.astype(BF16)
    in_wts = (row(g_mix[l]), w_in_p, wg2_p, row(b_gla_g[l]), row(g_gla_out[l]), row(g_qa[l]),
              wqb_p, row(g_kva[l]), wk_p, wvt_p)
    out_wts = (w_out[l].astype(BF16), row(g_mem_q[l]), w_mq[l].astype(BF16),
               w_mo[l].astype(BF16), row(g_final))
    amask = _gla_mask()
    sel = jnp.zeros((MLA_ROPE, HEAD_PAD), BF16).at[
        jnp.arange(MLA_ROPE), ROPE_LANE0 + jnp.arange(MLA_ROPE)].set(1.0)

    tm = 512
    sample_rows = bs * ts
    bpt = tm // ts

    mk_p, mv_p = _mem_kv(mem_prompt.reshape(bp * N_MEM, D_MODEL), row(g_mem_kv[l]),
                         w_mk[l].astype(BF16), w_mv[l].astype(BF16), tm)
    cos_p, sin_p = _rope_tables(jnp.arange(tp, dtype=jnp.int32))
    s0_p = jnp.zeros((bp, GLA_KW, GLA_DV), F32)
    ogla_p, mz_p, q_p, k_p, vt_p, ckv_p, kr_p, st_p = _in_gla(
        x_prompt, s0_p, cos_p, sin_p, _chunk_onehot(np.arange(tp)), amask, in_wts, tm=tm, carry=True)
    omla_p = _attention_pipelined(
        q_p, k_p.reshape(bp, MLA_HEADS, tp // tm, tm, HEAD_PAD), vt_p, mz_p)
    y_prompt = _out_mem(x_prompt, ogla_p, omla_p, mk_p.reshape(bp, N_MEM, D_MODEL),
                        mv_p.reshape(bp, N_MEM, D_MODEL), out_wts, tm=2 * tm, nb=1)

    xs = x_sample.reshape(1, sample_rows, D_MODEL)
    cos_s, sin_s = _rope_tables(past + jnp.arange(ts, dtype=jnp.int32))
    cos_s, sin_s = jnp.tile(cos_s, (bpt, 1)), jnp.tile(sin_s, (bpt, 1))
    s0_s = state_gla[l].reshape(bs, GLA_KW, GLA_DV)
    ogla_s, mz_s, q_s, _, _, ckv_s, kr_s, st_s = _in_gla(
        xs, s0_s, cos_s, sin_s, jnp.zeros((tm, LANES), F32), amask, in_wts, tm=tm, carry=False)
    omla_s = _latent_attention(
        q_s.reshape(MLA_HEADS, bs, ts, HEAD_PAD), ckv_s.reshape(bs, ts, MLA_KV_RANK),
        kr_s.reshape(bs, ts, MLA_ROPE), cache_mla_ckv[l], cache_mla_krope[l],
        mz_s.reshape(bs, ts, MLA_WIDTH), wk_p, wvt_p, sel)
    omla_s = omla_s.reshape(1, sample_rows, MLA_WIDTH)
    nb_s = tm // ts
    y_sample = _out_mem(xs, ogla_s, omla_s,
                        cache_mem_k[l].reshape(bs, N_MEM, D_MODEL).astype(BF16),
                        cache_mem_v[l].reshape(bs, N_MEM, D_MODEL).astype(BF16), out_wts,
                        tm=tm, nb=nb_s)

    mem_shape = (1, bp, N_MEM, MEM_HEADS, MEM_DH)
    st_shape = (GLA_HEADS, GLA_DK, GLA_DV)
    return (y_prompt, y_sample.reshape(bs, ts, D_MODEL),
            ckv_p[None], kr_p[None], st_p.reshape(1, bp, *st_shape),
            mk_p.reshape(mem_shape), mv_p.reshape(mem_shape),
            ckv_s.reshape(1, bs, ts, MLA_KV_RANK), kr_s.reshape(1, bs, ts, MLA_ROPE),
            st_s.reshape(1, bs, *st_shape))
```

```python
import functools

import jax
import jax.numpy as jnp
import numpy as np
from jax import lax
from jax.experimental import pallas as pl
from jax.experimental.pallas import tpu as pltpu

D_MODEL = 1024
CHUNK = 64
EPS = 1e-6
NEG_INF = -1e30

GLA_HEADS = 4
GLA_DK = 64
GLA_DV = 128
GLA_GATE_RANK = 16
GLA_GATE_NORM = 16.0
GLA_KW = GLA_HEADS * GLA_DK
GLA_WIDTH = GLA_HEADS * GLA_DV

MLA_HEADS = 4
MLA_Q_RANK = 256
MLA_KV_RANK = 128
MLA_NOPE = 64
MLA_ROPE = 32
MLA_DV = 128
MLA_WIDTH = MLA_HEADS * MLA_DV
MLA_SCALE = (MLA_NOPE + MLA_ROPE) ** -0.5
ROPE_THETA = 10000.0
HALF_ROPE = MLA_ROPE // 2
Q_PRESCALE = MLA_SCALE * float(np.log2(np.e))
SUBQ = 256
MASK_LANE0 = MLA_NOPE + MLA_ROPE
Q_TILE_CHUNKS = 128 - MASK_LANE0
ONES_ROWS = 16

MEM_HEADS = 4
MEM_DH = D_MODEL // MEM_HEADS
N_MEM = 256

LANES = 128
HEAD_PAD = 128
ROPE_LANE0 = MLA_NOPE

COL_MISC = 0
COL_Q = COL_MISC + LANES
COL_K = COL_Q + GLA_KW
COL_V = COL_K + GLA_KW
COL_Z = COL_V + GLA_WIDTH
COL_CQ = COL_Z + GLA_WIDTH
COL_CKV = COL_CQ + MLA_Q_RANK
COL_MZ = COL_CKV + MLA_KV_RANK
D_IN_PAD = COL_MZ + MLA_WIDTH
COL_MLA = COL_CQ
COL_LA = COL_MLA
GBUF_W = COL_LA + GLA_KW
PROJ_PIECE = 256

VMEM_LIMIT = 52 * 1024 * 1024

BF16 = jnp.bfloat16
F32 = jnp.float32


def _rms(x, g):
    ms = jnp.mean(x * x, axis=-1, keepdims=True)
    return x * lax.rsqrt(ms + EPS) * g


def _dot(a, b):
    return jnp.dot(a, b, preferred_element_type=F32)


def _dot_nt(a, b):
    return lax.dot_general(a, b, (((1,), (1,)), ((), ())), preferred_element_type=F32)


def _dot_tn(a, b):
    return lax.dot_general(a, b, (((0,), (0,)), ((), ())), preferred_element_type=F32)


def _rope_group(x, cos, sin):
    lane = lax.broadcasted_iota(jnp.int32, x.shape, 1)
    partner = jnp.where(lane < ROPE_LANE0 + HALF_ROPE,
                        pltpu.roll(x, LANES - HALF_ROPE, 1),
                        pltpu.roll(x, HALF_ROPE, 1))
    return x * cos + partner * sin


def _mem_kv_body(mem_ref, g_ref, wk_ref, wv_ref, mk_ref, mv_ref):
    m = _rms(mem_ref[...], g_ref[...]).astype(BF16)
    mk_ref[...] = _dot(m, wk_ref[...])
    mv_ref[...] = _dot(m, wv_ref[...])


def _mem_kv(mem2d, g, wk, wv, tm):
    rows = mem2d.shape[0]
    wspec = pl.BlockSpec((D_MODEL, D_MODEL), lambda i: (0, 0))
    ospec = pl.BlockSpec((tm, D_MODEL), lambda i: (i, 0))
    return pl.pallas_call(
        _mem_kv_body,
        grid=(rows // tm,),
        in_specs=[pl.BlockSpec((tm, D_MODEL), lambda i: (i, 0)),
                  pl.BlockSpec((1, D_MODEL), lambda i: (0, 0)), wspec, wspec],
        out_specs=[ospec, ospec],
        out_shape=[jax.ShapeDtypeStruct((rows, D_MODEL), F32)] * 2,
        compiler_params=pltpu.CompilerParams(dimension_semantics=("arbitrary",),
                                             vmem_limit_bytes=VMEM_LIMIT),
        name="mem_kv",
    )(mem2d, g, wk, wv)


class _InGla:
    def __init__(self, tabs, wts, amask_ref):
        self.cos_ref, self.sin_ref, self.kx_ref = tabs
        (self.gmix_ref, self.win_ref, self.wg2_ref, self.bg_ref, self.ggla_ref, self.gqa_ref,
         self.wqb_ref, self.gkva_ref, self.wk_ref, self.wvt_ref) = wts
        self.amask_ref = amask_ref
        ri = lax.broadcasted_iota(jnp.int32, (CHUNK, CHUNK), 0)
        ci = lax.broadcasted_iota(jnp.int32, (CHUNK, CHUNK), 1)
        tril = (ci <= ri).astype(BF16)
        self.tril3 = jnp.concatenate([tril, tril, tril], axis=1)
        klane = lax.broadcasted_iota(jnp.int32, (CHUNK, GLA_KW), 1)
        self.head_of_lane = klane // GLA_DK

    def prep_pieces(self, x_ref, rows, gbuf, proj_ref, outs, vt_dst):
        mz_ref, q_ref, k_ref, ckv_ref, kr_ref = outs
        loc = {}

        def norm():
            loc["h"] = _rms(x_ref[rows, :], self.gmix_ref[...]).astype(BF16)

        def proj(dst, d0, c0, c1):
            def run():
                dst[:, d0:d0 + c1 - c0] = _dot(loc["h"], self.win_ref[:, c0:c1])
            return run

        def queries():
            cos, sin = self.cos_ref[rows, :], self.sin_ref[rows, :]
            cq = proj_ref[:, COL_CQ - COL_MLA:COL_CQ - COL_MLA + MLA_Q_RANK]
            qf = _dot(_rms(cq, self.gqa_ref[...]).astype(BF16), self.wqb_ref[...])
            for hh in range(MLA_HEADS):
                qh = _rope_group(qf[:, hh * HEAD_PAD:(hh + 1) * HEAD_PAD], cos, sin)
                q_ref[hh, rows, :] = (qh * Q_PRESCALE).astype(BF16)

        def keys_values():
            cos, sin = self.cos_ref[rows, :], self.sin_ref[rows, :]
            ckv = _rms(proj_ref[:, COL_CKV - COL_MLA:COL_CKV - COL_MLA + MLA_KV_RANK],
                       self.gkva_ref[...])
            ckv_ref[rows, :] = ckv
            ckv_b = ckv.astype(BF16)
            misc = gbuf[:, COL_MISC:COL_MISC + LANES]
            lane = lax.broadcasted_iota(jnp.int32, misc.shape, 1)
            kr = jnp.where(lane >= ROPE_LANE0, _rope_group(misc, cos, sin), 0.0)
            kr_ref[rows, :] = kr[:, ROPE_LANE0:ROPE_LANE0 + MLA_ROPE]
            kr = kr + self.kx_ref[rows, :]
            kn = _dot(ckv_b, self.wk_ref[...])
            vt = _dot_nt(self.wvt_ref[...], ckv_b)
            for hh in range(MLA_HEADS):
                k_ref[hh, rows, :] = (kn[:, hh * HEAD_PAD:(hh + 1) * HEAD_PAD] + kr).astype(BF16)
                vt_dst(hh, vt[hh * MLA_DV:(hh + 1) * MLA_DV, :].astype(BF16))

        def gates():
            mz = proj_ref[:, COL_MZ - COL_MLA:COL_MZ - COL_MLA + MLA_WIDTH]
            mz_ref[rows, :] = (mz * jax.nn.sigmoid(mz)).astype(BF16)
            misc = gbuf[:, COL_MISC:COL_MISC + LANES]
            z = _dot(misc.astype(BF16), self.wg2_ref[...]) + self.bg_ref[...]
            gbuf[:, COL_LA:COL_LA + GLA_KW] = ((jnp.minimum(z, 0.0)
                                                - jnp.log1p(jnp.exp(-jnp.abs(z))))
                                               * (1.0 / GLA_GATE_NORM))

        cuts = list(range(0, COL_MLA, PROJ_PIECE)) + [COL_MLA]
        pieces = [norm] + [proj(gbuf, c0, c0, c1) for c0, c1 in zip(cuts[:-1], cuts[1:])]
        cuts = list(range(COL_MLA, D_IN_PAD, PROJ_PIECE)) + [D_IN_PAD]
        pieces += [proj(proj_ref, c0 - COL_MLA, c0, c1) for c0, c1 in zip(cuts[:-1], cuts[1:])]
        return pieces + [queries, keys_values, gates]

    def chunk_stages(self, c, gbuf, ogla_ref, state):
        rows = slice(c * CHUNK, (c + 1) * CHUNK)
        loc = {}

        def per_head(a):
            return jnp.concatenate(
                [jnp.where(self.head_of_lane == hh, a, 0.0) for hh in range(GLA_HEADS)], axis=0)

        def decay():
            la = gbuf[rows, COL_LA:COL_LA + GLA_KW]
            la_hi = la.astype(BF16)
            r1 = la - la_hi.astype(F32)
            la_mid = r1.astype(BF16)
            la_lo = (r1 - la_mid.astype(F32)).astype(BF16)
            loc["b"] = _dot(self.tril3, jnp.concatenate([la_hi, la_mid, la_lo], axis=0))

        def inner():
            b = loc["b"]
            b_last = b[CHUNK - 1:CHUNK, :]
            q = gbuf[rows, COL_Q:COL_Q + GLA_KW] * (GLA_DK ** -0.5)
            k = gbuf[rows, COL_K:COL_K + GLA_KW]
            loc["sdec"] = jnp.exp(b_last)
            loc["qs"] = per_head(q * jnp.exp(b)).astype(BF16)
            kt = jnp.concatenate([(k * jnp.exp(-b)).astype(BF16)] * GLA_HEADS, axis=0)
            loc["kd2"] = per_head(k * jnp.exp(b_last - b)).astype(BF16)
            loc["a"] = _dot_nt(loc["qs"], kt)

        def outer():
            a = jnp.where(self.amask_ref[...] != 0.0, loc["a"], 0.0).astype(BF16)
            v = gbuf[rows, COL_V:COL_V + GLA_WIDTH]
            vs = jnp.concatenate([v[:, hh * GLA_DV:(hh + 1) * GLA_DV] for hh in range(GLA_HEADS)],
                                 axis=0).astype(BF16)
            st = state["st"]
            loc["o"] = _dot_nt(loc["qs"], st.astype(BF16)) + _dot(a, vs)
            state["st"] = st * loc["sdec"] + _dot_tn(vs, loc["kd2"])

        def emit():
            gz = gbuf[rows, COL_Z:COL_Z + GLA_WIDTH]
            gate = gz * jax.nn.sigmoid(gz)
            on = _rms(loc["o"], self.ggla_ref[...])
            for hh in range(GLA_HEADS):
                cols = slice(hh * GLA_DV, (hh + 1) * GLA_DV)
                ogla_ref[rows, cols] = (on[hh * CHUNK:(hh + 1) * CHUNK, :]
                                        * gate[:, cols]).astype(BF16)

        return [decay, inner, outer, emit]


def _skewed(stage_lists):
    depth = len(stage_lists[0])
    order = []
    for tau in range(len(stage_lists) + depth - 1):
        for s in range(depth):
            c = tau - s
            if 0 <= c < len(stage_lists):
                order.append(stage_lists[c][s])
    return order


def _interleave(main, fill):
    done = 0
    for n, thunk in enumerate(main):
        thunk()
        want = (n + 1) * len(fill) // len(main)
        for piece in fill[done:want]:
            piece()
        done = want


def _in_gla_lag_body(x_ref, s0_ref, cos_ref, sin_ref, kx_ref, amask_ref, *refs):
    wts, refs = refs[:10], refs[10:]
    (ogla_a_ref, ogla_b_ref, mz_ref, q_ref, k_ref, vt_ref, ckv_ref, kr_ref, sfin_ref,
     proj_ref, gbuf_a, gbuf_b, st_ref) = refs
    hb = gbuf_a.shape[0]
    t = pl.program_id(1)
    last = pl.num_programs(1) - 1
    parts = _InGla((cos_ref, sin_ref, kx_ref), wts, amask_ref)
    outs = (mz_ref, q_ref, k_ref, ckv_ref, kr_ref)

    @pl.when(t == 0)
    def _():
        st_ref[...] = s0_ref[0].T
        gbuf_b[...] = jnp.zeros(gbuf_b.shape, F32)

    def bracket(gbuf_prev, ogla_ref, half, gbuf_next):
        state = {"st": st_ref[...]}
        rows = slice(half * hb, (half + 1) * hb)
        stages = _skewed([parts.chunk_stages(c, gbuf_prev, ogla_ref, state)
                          for c in range(hb // CHUNK)])
        pieces = parts.prep_pieces(
            x_ref, rows, gbuf_next, proj_ref, outs,
            lambda hh, val: vt_ref.__setitem__((hh, half), val))
        _interleave(stages, pieces)
        st_ref[...] = state["st"]

    bracket(gbuf_b, ogla_b_ref, 0, gbuf_a)

    @pl.when(t < last)
    def _():
        bracket(gbuf_a, ogla_a_ref, 1, gbuf_b)

    @pl.when(t == last)
    def _():
        sfin_ref[0] = st_ref[...].T


def _in_gla_batched_body(x_ref, s0_ref, cos_ref, sin_ref, kx_ref, amask_ref, *refs):
    wts, refs = refs[:10], refs[10:]
    (ogla_ref, mz_ref, q_ref, k_ref, vt_ref, ckv_ref, kr_ref, sfin_ref, proj_ref, gbuf) = refs
    tm = x_ref.shape[0]
    parts = _InGla((cos_ref, sin_ref, kx_ref), wts, amask_ref)
    for piece in parts.prep_pieces(
            x_ref, slice(0, tm), gbuf, proj_ref, (mz_ref, q_ref, k_ref, ckv_ref, kr_ref),
            lambda hh, val: vt_ref.__setitem__((hh, 0), val)):
        piece()
    states = [{"st": s0_ref[c].T} for c in range(tm // CHUNK)]
    for stage in _skewed([parts.chunk_stages(c, gbuf, ogla_ref, states[c])
                          for c in range(tm // CHUNK)]):
        stage()
    for c in range(tm // CHUNK):
        sfin_ref[c] = states[c]["st"].T


def _in_gla(x, s0, cos, sin, kx, amask, wts, *, hb, carry):
    g, tlen, _ = x.shape
    tm = 2 * hb if carry else hb
    nt = tlen // tm
    ns = 1 if carry else tm // CHUNK
    clamp = (lambda t: jnp.minimum(t, nt - 1)) if carry else (lambda t: t)
    smap = (lambda b, t: (b, 0, 0)) if carry else (lambda b, t: (t, 0, 0))
    tab_tiles = cos.shape[0] // tm
    tmap = (lambda b, t: (clamp(t), 0)) if tab_tiles > 1 else (lambda b, t: (0, 0))

    def const(a):
        return pl.BlockSpec(a.shape, lambda b, t: (0,) * a.ndim, pipeline_mode=pl.Buffered(1))

    row = lambda w: pl.BlockSpec((None, tm, w), lambda b, t: (b, clamp(t), 0))
    heads = pl.BlockSpec((None, MLA_HEADS, tm, HEAD_PAD), lambda b, t: (b, 0, clamp(t), 0))
    tab = pl.BlockSpec((tm, LANES), tmap)
    in_specs = [row(D_MODEL), pl.BlockSpec((ns, GLA_KW, GLA_DV), smap), tab, tab, tab,
                const(amask)] + [const(w) for w in wts]
    vt_spec = pl.BlockSpec((None, MLA_HEADS, tm // hb, MLA_DV, hb),
                           lambda b, t: (b, 0, clamp(t), 0, 0))
    common_specs = [row(MLA_WIDTH), heads, heads, vt_spec, row(MLA_KV_RANK), row(MLA_ROPE),
                    pl.BlockSpec((ns, GLA_KW, GLA_DV), smap)]
    hshape = jax.ShapeDtypeStruct((g, MLA_HEADS, tlen, HEAD_PAD), BF16)
    common_shapes = [
        jax.ShapeDtypeStruct((g, tlen, MLA_WIDTH), BF16), hshape, hshape,
        jax.ShapeDtypeStruct((g, MLA_HEADS, tlen // hb, MLA_DV, hb), BF16),
        jax.ShapeDtypeStruct((g, tlen, MLA_KV_RANK), F32),
        jax.ShapeDtypeStruct((g, tlen, MLA_ROPE), F32),
        jax.ShapeDtypeStruct(s0.shape, F32),
    ]
    gbuf = pltpu.VMEM((hb, GBUF_W), F32)
    proj = pltpu.VMEM((hb, D_IN_PAD - COL_MLA), F32)
    if carry:
        blk = lambda imap: pl.BlockSpec((None, None, hb, GLA_WIDTH), imap)
        ogla_specs = [blk(lambda b, t: (b, clamp(t), 0, 0)),
                      blk(lambda b, t: (b, jnp.maximum(t - 1, 0), 0, 0))]
        ogla_shapes = [jax.ShapeDtypeStruct((g, nt, hb, GLA_WIDTH), BF16)] * 2
        body, grid = _in_gla_lag_body, (g, nt + 1)
        scratch = [proj, gbuf, gbuf, pltpu.VMEM((GLA_DV, GLA_KW), F32)]
    else:
        ogla_specs = [row(GLA_WIDTH)]
        ogla_shapes = [jax.ShapeDtypeStruct((g, tlen, GLA_WIDTH), BF16)]
        body, grid = _in_gla_batched_body, (g, nt)
        scratch = [proj, gbuf]
    return pl.pallas_call(
        body,
        grid=grid,
        in_specs=in_specs,
        out_specs=ogla_specs + common_specs,
        out_shape=ogla_shapes + common_shapes,
        scratch_shapes=scratch,
        compiler_params=pltpu.CompilerParams(dimension_semantics=("arbitrary", "arbitrary"),
                                             vmem_limit_bytes=VMEM_LIMIT),
        name="in_gla_lagged" if carry else "in_gla_batched",
    )(x, s0, cos, sin, kx, amask, *wts)


def _latent_attn_body(q_ref, ckvn_ref, krn_ref, ckvc_ref, krc_ref, mz_ref, wk_ref, wvt_ref,
                      sel_ref, o_ref, *, past):
    ts = q_ref.shape[1]
    nk = past + ts
    ckv = jnp.concatenate([ckvc_ref[...], ckvn_ref[...]], axis=0).astype(BF16)
    kr = jnp.concatenate([krc_ref[...], krn_ref[...]], axis=0).astype(BF16)
    krp = _dot(kr, sel_ref[...]).astype(BF16)
    keys = jnp.concatenate([ckv, krp], axis=1)
    qcat = jnp.concatenate(
        [jnp.concatenate([_dot_nt(q_ref[hh], wk_ref[:, hh * HEAD_PAD:(hh + 1) * HEAD_PAD])
                          .astype(BF16), q_ref[hh]], axis=1) for hh in range(MLA_HEADS)],
        axis=0)
    s = _dot_nt(keys, qcat)
    kchunk = lax.broadcasted_iota(jnp.int32, (nk, 1), 0) // CHUNK
    qpos = past + lax.broadcasted_iota(jnp.int32, (1, MLA_HEADS * ts), 1) % ts
    s = jnp.where(kchunk <= qpos // CHUNK, s, NEG_INF)
    p = jnp.exp2(s - jnp.max(s, axis=0, keepdims=True)).astype(BF16)
    vals = jnp.concatenate([ckv, jnp.ones((nk, LANES), BF16)], axis=1)
    ol = _dot_tn(p, vals)
    olat = (ol[:, :MLA_KV_RANK] / ol[:, MLA_KV_RANK:]).astype(BF16)
    for hh in range(MLA_HEADS):
        cols = slice(hh * MLA_DV, (hh + 1) * MLA_DV)
        o = _dot_nt(olat[hh * ts:(hh + 1) * ts, :], wvt_ref[cols, :])
        o_ref[:, cols] = (o * mz_ref[:, cols].astype(F32)).astype(BF16)


def _latent_attention(q, ckv_new, kr_new, ckv_cache, kr_cache, mz, wk, wvt, sel):
    bsz, past, _ = ckv_cache.shape
    ts = q.shape[2]
    per_b = lambda r, w: pl.BlockSpec((None, r, w), lambda b: (b, 0, 0))
    const = lambda a: pl.BlockSpec(a.shape, lambda b: (0,) * a.ndim)
    return pl.pallas_call(
        functools.partial(_latent_attn_body, past=past),
        grid=(bsz,),
        in_specs=[pl.BlockSpec((MLA_HEADS, None, ts, HEAD_PAD), lambda b: (0, b, 0, 0)),
                  per_b(ts, MLA_KV_RANK), per_b(ts, MLA_ROPE),
                  per_b(past, MLA_KV_RANK), per_b(past, MLA_ROPE), per_b(ts, MLA_WIDTH),
                  const(wk), const(wvt), const(sel)],
        out_specs=per_b(ts, MLA_WIDTH),
        out_shape=jax.ShapeDtypeStruct(mz.shape, BF16),
        compiler_params=pltpu.CompilerParams(dimension_semantics=("arbitrary",),
                                             vmem_limit_bytes=VMEM_LIMIT),
        name="latent_attn",
    )(q, ckv_new, kr_new, ckv_cache, kr_cache, mz, wk, wvt, sel)


def _attn_pipe_body(q_ref, k_ref, vt_ref, mz_ref, o_ref, q2_ref, s0_ref, s1_ref, c0_ref, c1_ref,
                    p0_ref, p1_ref, a0_ref, a1_ref, m_ref, l_ref, acc_ref):
    tq = s0_ref.shape[1]
    tk = k_ref.shape[1]
    nsub = tq // SUBQ
    ndiag = tq // tk
    nq = q_ref.shape[0] // tq
    all_subs = tuple(range(nsub))
    diag_subs = [tuple(u for u in all_subs if (u + 1) * SUBQ > d * tk) for d in range(ndiag)]
    slots = ((s0_ref, c0_ref, p0_ref, a0_ref), (s1_ref, c1_ref, p1_ref, a1_ref))

    def cols(u):
        return slice(u * SUBQ, (u + 1) * SUBQ)

    def scores(j, slot, subs, diagonal):
        s_ref, c_ref, _, _ = slot
        kt = k_ref[j]
        for u in subs:
            s = _dot_nt(kt, q2_ref[diagonal, cols(u), :])
            s_ref[:, cols(u)] = s
            c_ref[:, cols(u)] = jnp.max(s, axis=0, keepdims=True)

    def softmax(slot, subs):
        s_ref, c_ref, p_ref, a_ref = slot
        for u in subs:
            m_old = m_ref[:, cols(u)]
            m_new = jnp.maximum(m_old, c_ref[:, cols(u)])
            a_ref[:, cols(u)] = jnp.exp2(m_old - m_new)
            p_ref[:, cols(u)] = jnp.exp2((s_ref[:, cols(u)] - m_new).astype(BF16))
            m_ref[:, cols(u)] = m_new

    ones_rows = jnp.ones((ONES_ROWS, tk), BF16)

    def values(j, slot, subs):
        _, _, p_ref, a_ref = slot
        vt1 = jnp.concatenate([vt_ref[j], ones_rows], axis=0)
        for u in subs:
            pv = _dot(vt1, p_ref[:, cols(u)])
            alpha = a_ref[:, cols(u)]
            acc_ref[:, cols(u)] = alpha * acc_ref[:, cols(u)] + pv[:MLA_DV, :]
            l_ref[:, cols(u)] = alpha * l_ref[:, cols(u)] + pv[MLA_DV:MLA_DV + 1, :]

    def start_tile(i):
        q = q_ref[i * tq:(i + 1) * tq, :]
        lane = lax.broadcasted_iota(jnp.int32, q.shape, 1) - MASK_LANE0
        qchunk = lax.broadcasted_iota(jnp.int32, q.shape, 0) // CHUNK
        q2_ref[0] = q
        q2_ref[1] = jnp.where(lane > qchunk, NEG_INF, q.astype(F32)).astype(BF16)
        scores(0, slots[0], all_subs, 1 if i == 0 else 0)

    def pair(u, n_plain):
        scores(2 * u + 1, slots[1], all_subs, 0)
        softmax(slots[0], all_subs)
        values(2 * u - 1, slots[1], all_subs)
        scores(2 * u + 2, slots[0], all_subs, jnp.where(2 * u + 2 == n_plain, 1, 0))
        softmax(slots[1], all_subs)
        values(2 * u, slots[0], all_subs)

    start_tile(0)
    for i in range(nq):
        n_plain = ndiag * i
        m_ref[...] = jnp.full(m_ref.shape, NEG_INF, F32)
        l_ref[...] = jnp.zeros(l_ref.shape, F32)
        acc_ref[...] = jnp.zeros(acc_ref.shape, F32)
        if n_plain > 0:
            scores(1, slots[1], all_subs, 0)
            softmax(slots[0], all_subs)
            scores(2, slots[0], all_subs, 1 if n_plain == 2 else 0)
            softmax(slots[1], all_subs)
            values(0, slots[0], all_subs)
            lax.fori_loop(1, n_plain // 2, lambda u, c: (pair(u, n_plain), c)[1], 0)
        for d in range(ndiag):
            if d + 1 < ndiag:
                scores(n_plain + d + 1, slots[(d + 1) % 2], diag_subs[d + 1], 1)
            softmax(slots[d % 2], diag_subs[d])
            if n_plain + d > 0:
                values(n_plain + d - 1, slots[(d + 1) % 2],
                       all_subs if d == 0 else diag_subs[d - 1])
        values(n_plain + ndiag - 1, slots[(ndiag - 1) % 2], diag_subs[ndiag - 1])
        if i + 1 < nq:
            start_tile(i + 1)
        rows = slice(i * tq, (i + 1) * tq)
        o = (acc_ref[...] / l_ref[...]).T
        o_ref[rows, :] = (o * mz_ref[rows, :].astype(F32)).astype(BF16)


def _attention_pipelined(q, k, vt, mz):
    bsz, nh, tlen, _ = q.shape
    nkt, tk = k.shape[2], k.shape[3]
    tq = Q_TILE_CHUNKS * CHUNK
    assert tq % (2 * tk) == 0 and tlen % tq == 0
    kmap = lambda b, h: (b, h, 0, 0, 0)
    omap = lambda b, h: (b, 0, h)
    stat = pltpu.VMEM((1, tq), F32)
    return pl.pallas_call(
        _attn_pipe_body,
        grid=(bsz, nh),
        in_specs=[pl.BlockSpec((None, None, tlen, HEAD_PAD), lambda b, h: (b, h, 0, 0)),
                  pl.BlockSpec((None, None, nkt, tk, HEAD_PAD), kmap),
                  pl.BlockSpec((None, None, nkt, MLA_DV, tk), kmap),
                  pl.BlockSpec((None, tlen, MLA_DV), omap)],
        out_specs=pl.BlockSpec((None, tlen, MLA_DV), omap),
        out_shape=jax.ShapeDtypeStruct(mz.shape, BF16),
        scratch_shapes=[pltpu.VMEM((2, tq, HEAD_PAD), BF16),
                        pltpu.VMEM((tk, tq), F32), pltpu.VMEM((tk, tq), F32), stat, stat,
                        pltpu.VMEM((tk, tq), BF16), pltpu.VMEM((tk, tq), BF16), stat, stat,
                        stat, stat, pltpu.VMEM((MLA_DV, tq), F32)],
        compiler_params=pltpu.CompilerParams(
            dimension_semantics=("arbitrary", "arbitrary"),
            vmem_limit_bytes=VMEM_LIMIT),
        name="mla_attn_pipe",
    )(q, k, vt, mz)


def _out_mem_body(*refs, n_ogla):
    ogla_refs, refs = refs[:n_ogla], refs[n_ogla:]
    (x_ref, omla_ref, mk_ref, mv_ref, wout_ref, gq_ref, wmq_ref, wmo_ref, gfin_ref,
     y_ref, o_scr) = refs
    nb = mk_ref.shape[0]
    rb = x_ref.shape[0] // nb
    ogla = jnp.concatenate([r[...] for r in ogla_refs], axis=0)
    x1 = (x_ref[...] + _dot(ogla, wout_ref[0:GLA_WIDTH, :])
          + _dot(omla_ref[...], wout_ref[GLA_WIDTH:GLA_WIDTH + MLA_WIDTH, :]))
    h = _rms(x1, gq_ref[...]).astype(BF16)
    qm = (_dot(h, wmq_ref[...]) * (MEM_DH ** -0.5)).astype(BF16)
    for bb in range(nb):
        rows = slice(bb * rb, (bb + 1) * rb)
        for hh in range(MEM_HEADS):
            cols = slice(hh * MEM_DH, (hh + 1) * MEM_DH)
            s = _dot_nt(qm[rows, cols], mk_ref[bb, :, cols].astype(BF16))
            p = jnp.exp(s - jnp.max(s, axis=1, keepdims=True))
            p = p / jnp.sum(p, axis=1, keepdims=True)
            o_scr[rows, cols] = _dot(p.astype(BF16), mv_ref[bb, :, cols].astype(BF16)).astype(BF16)
    x2 = x1 + _dot(o_scr[...], wmo_ref[...])
    y_ref[...] = _rms(x2, gfin_ref[...])


def _out_mem(x, ogla_parts, omla, mk, mv, wts, *, nb):
    g, tlen, _ = x.shape
    hb = ogla_parts[0].shape[2]
    tm = hb * len(ogla_parts)
    nt = tlen // tm
    mmap = (lambda b, t: (b, 0, 0)) if nb == 1 else (lambda b, t: (t, 0, 0))
    row = lambda w: pl.BlockSpec((None, tm, w), lambda b, t: (b, t, 0))
    part = pl.BlockSpec((None, None, hb, GLA_WIDTH), lambda b, t: (b, t, 0, 0))
    mspec = pl.BlockSpec((nb, N_MEM, D_MODEL), mmap)
    sq = pl.BlockSpec((D_MODEL, D_MODEL), lambda b, t: (0, 0))
    vec = pl.BlockSpec((1, D_MODEL), lambda b, t: (0, 0))
    return pl.pallas_call(
        functools.partial(_out_mem_body, n_ogla=len(ogla_parts)),
        grid=(g, nt),
        in_specs=[part] * len(ogla_parts) + [row(D_MODEL), row(MLA_WIDTH), mspec, mspec,
                                              sq, vec, sq, sq, vec],
        out_specs=row(D_MODEL),
        out_shape=jax.ShapeDtypeStruct(x.shape, F32),
        scratch_shapes=[pltpu.VMEM((tm, D_MODEL), BF16)],
        compiler_params=pltpu.CompilerParams(dimension_semantics=("arbitrary", "arbitrary"),
                                             vmem_limit_bytes=VMEM_LIMIT),
        name="out_mem",
    )(*ogla_parts, x, omla, mk, mv, *wts)


def _prep_w_in(w_in):
    splits = np.cumsum([GLA_KW, GLA_KW, GLA_WIDTH, GLA_GATE_RANK, GLA_WIDTH, MLA_Q_RANK,
                        MLA_KV_RANK, MLA_ROPE])
    gq, gk, gv, glr, gz, cq, ckv, kr, mz = jnp.split(w_in, splits.tolist(), axis=1)
    zeros = lambda n: jnp.zeros((D_MODEL, n), w_in.dtype)
    misc = jnp.concatenate([glr, zeros(ROPE_LANE0 - GLA_GATE_RANK), kr,
                            zeros(LANES - ROPE_LANE0 - MLA_ROPE)], axis=1)
    return jnp.concatenate([misc, gq, gk, gv, gz, cq, ckv, mz], axis=1).astype(BF16)


def _pad_heads(w, width):
    kdim = w.shape[0]
    w = w.reshape(kdim, MLA_HEADS, width)
    w = jnp.pad(w, ((0, 0), (0, 0), (0, HEAD_PAD - width)))
    return w.reshape(kdim, MLA_HEADS * HEAD_PAD)


def _rope_tables(pos):
    inv = ROPE_THETA ** (-jnp.arange(HALF_ROPE, dtype=F32) / HALF_ROPE)
    ang = pos.astype(F32)[:, None] * inv[None, :]
    c, s = jnp.cos(ang), jnp.sin(ang)
    n = pos.shape[0]
    cos = jnp.concatenate([jnp.ones((n, ROPE_LANE0), F32), c, c,
                           jnp.zeros((n, LANES - ROPE_LANE0 - MLA_ROPE), F32)], axis=1)
    sin = jnp.concatenate([jnp.zeros((n, ROPE_LANE0), F32), -s, s,
                           jnp.zeros((n, LANES - ROPE_LANE0 - MLA_ROPE), F32)], axis=1)
    return cos, sin


def _chunk_onehot(pos):
    cidx = (np.asarray(pos) // CHUNK) % Q_TILE_CHUNKS
    tab = np.zeros((len(cidx), LANES), np.float32)
    tab[np.arange(len(cidx)), MASK_LANE0 + cidx] = 1.0
    return jnp.asarray(tab)


def _gla_mask():
    r = np.arange(GLA_HEADS * CHUNK)
    same_head = (r[:, None] // CHUNK) == (r[None, :] // CHUNK)
    causal = (r[None, :] % CHUNK) <= (r[:, None] % CHUNK)
    return jnp.asarray((same_head & causal).astype(np.float32))


def kernel(x_prompt, x_sample, mem_prompt, cache_mla_ckv, cache_mla_krope, state_gla, cache_mem_k, cache_mem_v, g_mix, w_in, w_gla_g2, b_gla_g, g_gla_out, g_qa, w_qb, g_kva, w_kvb, w_out, g_mem_q, g_mem_kv, w_mq, w_mk, w_mv, w_mo, g_final):
    bp, tp, _ = x_prompt.shape
    bs, ts, _ = x_sample.shape
    past = cache_mla_ckv.shape[2]
    assert ts == CHUNK and g_mix.shape[0] == 1
    l = 0
    row = lambda g: g.reshape(1, -1)

    w_in_p = _prep_w_in(w_in[l])
    wg2_p = jnp.pad(w_gla_g2[l], ((0, LANES - GLA_GATE_RANK), (0, 0))).astype(BF16)
    wqb_p = _pad_heads(w_qb[l], MLA_NOPE + MLA_ROPE).astype(BF16)
    wkv = w_kvb[l].reshape(MLA_KV_RANK, MLA_HEADS, MLA_NOPE + MLA_DV)
    wk_p = _pad_heads(wkv[:, :, :MLA_NOPE].reshape(MLA_KV_RANK, -1), MLA_NOPE).astype(BF16)
    wvt_p = wkv[:, :, MLA_NOPE:].reshape(MLA_KV_RANK, MLA_WIDTH).T.astype(BF16)
    in_wts = (row(g_mix[l]), w_in_p, wg2_p, row(b_gla_g[l]), row(g_gla_out[l]), row(g_qa[l]),
              wqb_p, row(g_kva[l]), wk_p, wvt_p)
    out_wts = (w_out[l].astype(BF16), row(g_mem_q[l]), w_mq[l].astype(BF16),
               w_mo[l].astype(BF16), row(g_final))
    amask = _gla_mask()
    sel = jnp.zeros((MLA_ROPE, HEAD_PAD), BF16).at[
        jnp.arange(MLA_ROPE), ROPE_LANE0 + jnp.arange(MLA_ROPE)].set(1.0)

    tm = 512
    sample_rows = bs * ts
    bpt = tm // ts

    mk_p, mv_p = _mem_kv(mem_prompt.reshape(bp * N_MEM, D_MODEL), row(g_mem_kv[l]),
                         w_mk[l].astype(BF16), w_mv[l].astype(BF16), tm)
    cos_p, sin_p = _rope_tables(jnp.arange(tp, dtype=jnp.int32))
    s0_p = jnp.zeros((bp, GLA_KW, GLA_DV), F32)
    ogla_a, ogla_b, mz_p, q_p, k_p, vt_p, ckv_p, kr_p, st_p = _in_gla(
        x_prompt, s0_p, cos_p, sin_p, _chunk_onehot(np.arange(tp)), amask, in_wts,
        hb=tm, carry=True)
    omla_p = _attention_pipelined(
        q_p, k_p.reshape(bp, MLA_HEADS, tp // tm, tm, HEAD_PAD), vt_p, mz_p)
    y_prompt = _out_mem(x_prompt, (ogla_a, ogla_b), omla_p, mk_p.reshape(bp, N_MEM, D_MODEL),
                        mv_p.reshape(bp, N_MEM, D_MODEL), out_wts, nb=1)

    xs = x_sample.reshape(1, sample_rows, D_MODEL)
    cos_s, sin_s = _rope_tables(past + jnp.arange(ts, dtype=jnp.int32))
    cos_s, sin_s = jnp.tile(cos_s, (bpt, 1)), jnp.tile(sin_s, (bpt, 1))
    s0_s = state_gla[l].reshape(bs, GLA_KW, GLA_DV)
    ogla_s, mz_s, q_s, _, _, ckv_s, kr_s, st_s = _in_gla(
        xs, s0_s, cos_s, sin_s, jnp.zeros((tm, LANES), F32), amask, in_wts, hb=tm, carry=False)
    omla_s = _latent_attention(
        q_s.reshape(MLA_HEADS, bs, ts, HEAD_PAD), ckv_s.reshape(bs, ts, MLA_KV_RANK),
        kr_s.reshape(bs, ts, MLA_ROPE), cache_mla_ckv[l], cache_mla_krope[l],
        mz_s.reshape(bs, ts, MLA_WIDTH), wk_p, wvt_p, sel)
    omla_s = omla_s.reshape(1, sample_rows, MLA_WIDTH)
    nb_s = tm // ts
    y_sample = _out_mem(xs, (ogla_s.reshape(1, sample_rows // tm, tm, GLA_WIDTH),), omla_s,
                        cache_mem_k[l].reshape(bs, N_MEM, D_MODEL).astype(BF16),
                        cache_mem_v[l].reshape(bs, N_MEM, D_MODEL).astype(BF16), out_wts,
                        nb=nb_s)

    mem_shape = (1, bp, N_MEM, MEM_HEADS, MEM_DH)
    st_shape = (GLA_HEADS, GLA_DK, GLA_DV)
    return (y_prompt, y_sample.reshape(bs, ts, D_MODEL),
            ckv_p[None], kr_p[None], st_p.reshape(1, bp, *st_shape),
            mk_p.reshape(mem_shape), mv_p.reshape(mem_shape),
            ckv_s.reshape(1, bs, ts, MLA_KV_RANK), kr_s.reshape(1, bs, ts, MLA_ROPE),
            st_s.reshape(1, bs, *st_shape))
```

```python
import functools

import jax
import jax.numpy as jnp
import numpy as np
from jax import lax
from jax.experimental import pallas as pl
from jax.experimental.pallas import tpu as pltpu

D_MODEL = 1024
CHUNK = 64
EPS = 1e-6
NEG_INF = -1e30

GLA_HEADS = 4
GLA_DK = 64
GLA_DV = 128
GLA_GATE_RANK = 16
GLA_GATE_NORM = 16.0
GLA_KW = GLA_HEADS * GLA_DK
GLA_WIDTH = GLA_HEADS * GLA_DV

MLA_HEADS = 4
MLA_Q_RANK = 256
MLA_KV_RANK = 128
MLA_NOPE = 64
MLA_ROPE = 32
MLA_DV = 128
MLA_WIDTH = MLA_HEADS * MLA_DV
MLA_SCALE = (MLA_NOPE + MLA_ROPE) ** -0.5
ROPE_THETA = 10000.0
HALF_ROPE = MLA_ROPE // 2
Q_PRESCALE = MLA_SCALE * float(np.log2(np.e))
SUBQ = 256
MASK_LANE0 = MLA_NOPE + MLA_ROPE
Q_TILE_CHUNKS = 128 - MASK_LANE0
ONES_ROWS = 16
OUT_ROW_BLOCK = 256

MEM_HEADS = 4
MEM_DH = D_MODEL // MEM_HEADS
N_MEM = 256

LANES = 128
HEAD_PAD = 128
ROPE_LANE0 = MLA_NOPE

COL_MISC = 0
COL_Q = COL_MISC + LANES
COL_K = COL_Q + GLA_KW
COL_V = COL_K + GLA_KW
COL_Z = COL_V + GLA_WIDTH
COL_CQ = COL_Z + GLA_WIDTH
COL_CKV = COL_CQ + MLA_Q_RANK
COL_MZ = COL_CKV + MLA_KV_RANK
D_IN_PAD = COL_MZ + MLA_WIDTH
COL_MLA = COL_CQ
COL_LA = COL_MLA
GBUF_W = COL_LA + GLA_KW
PROJ_PIECE = 256

VMEM_LIMIT = 52 * 1024 * 1024

BF16 = jnp.bfloat16
F32 = jnp.float32


def _rms(x, g):
    ms = jnp.mean(x * x, axis=-1, keepdims=True)
    return x * lax.rsqrt(ms + EPS) * g


def _dot(a, b):
    return jnp.dot(a, b, preferred_element_type=F32)


def _dot_nt(a, b):
    return lax.dot_general(a, b, (((1,), (1,)), ((), ())), preferred_element_type=F32)


def _dot_tn(a, b):
    return lax.dot_general(a, b, (((0,), (0,)), ((), ())), preferred_element_type=F32)


def _rope_group(x, cos, sin):
    lane = lax.broadcasted_iota(jnp.int32, x.shape, 1)
    partner = jnp.where(lane < ROPE_LANE0 + HALF_ROPE,
                        pltpu.roll(x, LANES - HALF_ROPE, 1),
                        pltpu.roll(x, HALF_ROPE, 1))
    return x * cos + partner * sin


def _mem_kv_body(mem_ref, g_ref, wk_ref, wv_ref, mk_ref, mv_ref):
    m = _rms(mem_ref[...], g_ref[...]).astype(BF16)
    mk_ref[...] = _dot(m, wk_ref[...])
    mv_ref[...] = _dot(m, wv_ref[...])


def _mem_kv(mem2d, g, wk, wv, tm):
    rows = mem2d.shape[0]
    wspec = pl.BlockSpec((D_MODEL, D_MODEL), lambda i: (0, 0))
    ospec = pl.BlockSpec((tm, D_MODEL), lambda i: (i, 0))
    return pl.pallas_call(
        _mem_kv_body,
        grid=(rows // tm,),
        in_specs=[pl.BlockSpec((tm, D_MODEL), lambda i: (i, 0)),
                  pl.BlockSpec((1, D_MODEL), lambda i: (0, 0)), wspec, wspec],
        out_specs=[ospec, ospec],
        out_shape=[jax.ShapeDtypeStruct((rows, D_MODEL), F32)] * 2,
        compiler_params=pltpu.CompilerParams(dimension_semantics=("arbitrary",),
                                             vmem_limit_bytes=VMEM_LIMIT),
        name="mem_kv",
    )(mem2d, g, wk, wv)


class _InGla:
    def __init__(self, tabs, wts, amask_ref):
        self.cos_ref, self.sin_ref, self.kx_ref = tabs
        (self.gmix_ref, self.win_ref, self.wg2_ref, self.bg_ref, self.ggla_ref, self.gqa_ref,
         self.wqb_ref, self.gkva_ref, self.wk_ref, self.wvt_ref) = wts
        self.amask_ref = amask_ref
        ri = lax.broadcasted_iota(jnp.int32, (CHUNK, CHUNK), 0)
        ci = lax.broadcasted_iota(jnp.int32, (CHUNK, CHUNK), 1)
        tril = (ci <= ri).astype(BF16)
        self.tril3 = jnp.concatenate([tril, tril, tril], axis=1)
        klane = lax.broadcasted_iota(jnp.int32, (CHUNK, GLA_KW), 1)
        self.head_of_lane = klane // GLA_DK

    def prep_pieces(self, x_ref, rows, gbuf, proj_ref, outs, vt_dst):
        mz_ref, q_ref, k_ref, ckv_ref, kr_ref = outs
        loc = {}

        def norm():
            loc["h"] = _rms(x_ref[rows, :], self.gmix_ref[...]).astype(BF16)

        def proj(dst, d0, c0, c1):
            def run():
                dst[:, d0:d0 + c1 - c0] = _dot(loc["h"], self.win_ref[:, c0:c1])
            return run

        def queries():
            cos, sin = self.cos_ref[rows, :], self.sin_ref[rows, :]
            cq = proj_ref[:, COL_CQ - COL_MLA:COL_CQ - COL_MLA + MLA_Q_RANK]
            qf = _dot(_rms(cq, self.gqa_ref[...]).astype(BF16), self.wqb_ref[...])
            for hh in range(MLA_HEADS):
                qh = _rope_group(qf[:, hh * HEAD_PAD:(hh + 1) * HEAD_PAD], cos, sin)
                q_ref[hh, rows, :] = (qh * Q_PRESCALE).astype(BF16)

        def keys_values():
            cos, sin = self.cos_ref[rows, :], self.sin_ref[rows, :]
            ckv = _rms(proj_ref[:, COL_CKV - COL_MLA:COL_CKV - COL_MLA + MLA_KV_RANK],
                       self.gkva_ref[...])
            ckv_ref[rows, :] = ckv
            ckv_b = ckv.astype(BF16)
            misc = gbuf[:, COL_MISC:COL_MISC + LANES]
            lane = lax.broadcasted_iota(jnp.int32, misc.shape, 1)
            kr = jnp.where(lane >= ROPE_LANE0, _rope_group(misc, cos, sin), 0.0)
            kr_ref[rows, :] = kr[:, ROPE_LANE0:ROPE_LANE0 + MLA_ROPE]
            kr = kr + self.kx_ref[rows, :]
            kn = _dot(ckv_b, self.wk_ref[...])
            vt = _dot_nt(self.wvt_ref[...], ckv_b)
            for hh in range(MLA_HEADS):
                k_ref[hh, rows, :] = (kn[:, hh * HEAD_PAD:(hh + 1) * HEAD_PAD] + kr).astype(BF16)
                vt_dst(hh, vt[hh * MLA_DV:(hh + 1) * MLA_DV, :].astype(BF16))

        def gates():
            mz = proj_ref[:, COL_MZ - COL_MLA:COL_MZ - COL_MLA + MLA_WIDTH]
            mz_ref[rows, :] = (mz * jax.nn.sigmoid(mz)).astype(BF16)
            misc = gbuf[:, COL_MISC:COL_MISC + LANES]
            z = _dot(misc.astype(BF16), self.wg2_ref[...]) + self.bg_ref[...]
            gbuf[:, COL_LA:COL_LA + GLA_KW] = ((jnp.minimum(z, 0.0)
                                                - jnp.log1p(jnp.exp(-jnp.abs(z))))
                                               * (1.0 / GLA_GATE_NORM))

        cuts = list(range(0, COL_MLA, PROJ_PIECE)) + [COL_MLA]
        pieces = [norm] + [proj(gbuf, c0, c0, c1) for c0, c1 in zip(cuts[:-1], cuts[1:])]
        cuts = list(range(COL_MLA, D_IN_PAD, PROJ_PIECE)) + [D_IN_PAD]
        pieces += [proj(proj_ref, c0 - COL_MLA, c0, c1) for c0, c1 in zip(cuts[:-1], cuts[1:])]
        return pieces + [queries, keys_values, gates]

    def chunk_stages(self, c, gbuf, ogla_ref, state):
        rows = slice(c * CHUNK, (c + 1) * CHUNK)
        loc = {}

        def per_head(a):
            return jnp.concatenate(
                [jnp.where(self.head_of_lane == hh, a, 0.0) for hh in range(GLA_HEADS)], axis=0)

        def decay():
            la = gbuf[rows, COL_LA:COL_LA + GLA_KW]
            la_hi = la.astype(BF16)
            r1 = la - la_hi.astype(F32)
            la_mid = r1.astype(BF16)
            la_lo = (r1 - la_mid.astype(F32)).astype(BF16)
            loc["b"] = _dot(self.tril3, jnp.concatenate([la_hi, la_mid, la_lo], axis=0))

        def inner():
            b = loc["b"]
            b_last = b[CHUNK - 1:CHUNK, :]
            q = gbuf[rows, COL_Q:COL_Q + GLA_KW] * (GLA_DK ** -0.5)
            k = gbuf[rows, COL_K:COL_K + GLA_KW]
            loc["sdec"] = jnp.exp(b_last)
            loc["qs"] = per_head(q * jnp.exp(b)).astype(BF16)
            kt = jnp.concatenate([(k * jnp.exp(-b)).astype(BF16)] * GLA_HEADS, axis=0)
            loc["kd2"] = per_head(k * jnp.exp(b_last - b)).astype(BF16)
            loc["a"] = _dot_nt(loc["qs"], kt)

        def outer():
            a = jnp.where(self.amask_ref[...] != 0.0, loc["a"], 0.0).astype(BF16)
            v = gbuf[rows, COL_V:COL_V + GLA_WIDTH]
            vs = jnp.concatenate([v[:, hh * GLA_DV:(hh + 1) * GLA_DV] for hh in range(GLA_HEADS)],
                                 axis=0).astype(BF16)
            st = state["st"]
            loc["o"] = _dot_nt(loc["qs"], st.astype(BF16)) + _dot(a, vs)
            state["st"] = st * loc["sdec"] + _dot_tn(vs, loc["kd2"])

        def emit():
            gz = gbuf[rows, COL_Z:COL_Z + GLA_WIDTH]
            gate = gz * jax.nn.sigmoid(gz)
            on = _rms(loc["o"], self.ggla_ref[...])
            for hh in range(GLA_HEADS):
                cols = slice(hh * GLA_DV, (hh + 1) * GLA_DV)
                ogla_ref[rows, cols] = (on[hh * CHUNK:(hh + 1) * CHUNK, :]
                                        * gate[:, cols]).astype(BF16)

        return [decay, inner, outer, emit]


def _skewed(stage_lists):
    depth = len(stage_lists[0])
    order = []
    for tau in range(len(stage_lists) + depth - 1):
        for s in range(depth):
            c = tau - s
            if 0 <= c < len(stage_lists):
                order.append(stage_lists[c][s])
    return order


def _interleave(main, fill):
    done = 0
    for n, thunk in enumerate(main):
        thunk()
        want = (n + 1) * len(fill) // len(main)
        for piece in fill[done:want]:
            piece()
        done = want


def _in_gla_lag_body(x_ref, s0_ref, cos_ref, sin_ref, kx_ref, amask_ref, *refs):
    wts, refs = refs[:10], refs[10:]
    (ogla_a_ref, ogla_b_ref, mz_ref, q_ref, k_ref, vt_ref, ckv_ref, kr_ref, sfin_ref,
     proj_ref, gbuf_a, gbuf_b, st_ref) = refs
    hb = gbuf_a.shape[0]
    t = pl.program_id(1)
    last = pl.num_programs(1) - 1
    parts = _InGla((cos_ref, sin_ref, kx_ref), wts, amask_ref)
    outs = (mz_ref, q_ref, k_ref, ckv_ref, kr_ref)

    @pl.when(t == 0)
    def _():
        st_ref[...] = s0_ref[0].T
        gbuf_b[...] = jnp.zeros(gbuf_b.shape, F32)

    def bracket(gbuf_prev, ogla_ref, half, gbuf_next):
        state = {"st": st_ref[...]}
        rows = slice(half * hb, (half + 1) * hb)
        stages = _skewed([parts.chunk_stages(c, gbuf_prev, ogla_ref, state)
                          for c in range(hb // CHUNK)])
        pieces = parts.prep_pieces(
            x_ref, rows, gbuf_next, proj_ref, outs,
            lambda hh, val: vt_ref.__setitem__((hh, half), val))
        _interleave(stages, pieces)
        st_ref[...] = state["st"]

    bracket(gbuf_b, ogla_b_ref, 0, gbuf_a)

    @pl.when(t < last)
    def _():
        bracket(gbuf_a, ogla_a_ref, 1, gbuf_b)

    @pl.when(t == last)
    def _():
        sfin_ref[0] = st_ref[...].T


def _in_gla_batched_body(x_ref, s0_ref, cos_ref, sin_ref, kx_ref, amask_ref, *refs):
    wts, refs = refs[:10], refs[10:]
    (ogla_ref, mz_ref, q_ref, k_ref, vt_ref, ckv_ref, kr_ref, sfin_ref, proj_ref, gbuf) = refs
    tm = x_ref.shape[0]
    parts = _InGla((cos_ref, sin_ref, kx_ref), wts, amask_ref)
    for piece in parts.prep_pieces(
            x_ref, slice(0, tm), gbuf, proj_ref, (mz_ref, q_ref, k_ref, ckv_ref, kr_ref),
            lambda hh, val: vt_ref.__setitem__((hh, 0), val)):
        piece()
    states = [{"st": s0_ref[c].T} for c in range(tm // CHUNK)]
    for stage in _skewed([parts.chunk_stages(c, gbuf, ogla_ref, states[c])
                          for c in range(tm // CHUNK)]):
        stage()
    for c in range(tm // CHUNK):
        sfin_ref[c] = states[c]["st"].T


def _in_gla(x, s0, cos, sin, kx, amask, wts, *, hb, carry):
    g, tlen, _ = x.shape
    tm = 2 * hb if carry else hb
    nt = tlen // tm
    ns = 1 if carry else tm // CHUNK
    clamp = (lambda t: jnp.minimum(t, nt - 1)) if carry else (lambda t: t)
    smap = (lambda b, t: (b, 0, 0)) if carry else (lambda b, t: (t, 0, 0))
    tab_tiles = cos.shape[0] // tm
    tmap = (lambda b, t: (clamp(t), 0)) if tab_tiles > 1 else (lambda b, t: (0, 0))

    def const(a):
        return pl.BlockSpec(a.shape, lambda b, t: (0,) * a.ndim, pipeline_mode=pl.Buffered(1))

    row = lambda w: pl.BlockSpec((None, tm, w), lambda b, t: (b, clamp(t), 0))
    heads = pl.BlockSpec((None, MLA_HEADS, tm, HEAD_PAD), lambda b, t: (b, 0, clamp(t), 0))
    tab = pl.BlockSpec((tm, LANES), tmap)
    in_specs = [row(D_MODEL), pl.BlockSpec((ns, GLA_KW, GLA_DV), smap), tab, tab, tab,
                const(amask)] + [const(w) for w in wts]
    vt_spec = pl.BlockSpec((None, MLA_HEADS, tm // hb, MLA_DV, hb),
                           lambda b, t: (b, 0, clamp(t), 0, 0))
    common_specs = [row(MLA_WIDTH), heads, heads, vt_spec, row(MLA_KV_RANK), row(MLA_ROPE),
                    pl.BlockSpec((ns, GLA_KW, GLA_DV), smap)]
    hshape = jax.ShapeDtypeStruct((g, MLA_HEADS, tlen, HEAD_PAD), BF16)
    common_shapes = [
        jax.ShapeDtypeStruct((g, tlen, MLA_WIDTH), BF16), hshape, hshape,
        jax.ShapeDtypeStruct((g, MLA_HEADS, tlen // hb, MLA_DV, hb), BF16),
        jax.ShapeDtypeStruct((g, tlen, MLA_KV_RANK), F32),
        jax.ShapeDtypeStruct((g, tlen, MLA_ROPE), F32),
        jax.ShapeDtypeStruct(s0.shape, F32),
    ]
    gbuf = pltpu.VMEM((hb, GBUF_W), F32)
    proj = pltpu.VMEM((hb, D_IN_PAD - COL_MLA), F32)
    if carry:
        blk = lambda imap: pl.BlockSpec((None, None, hb, GLA_WIDTH), imap)
        ogla_specs = [blk(lambda b, t: (b, clamp(t), 0, 0)),
                      blk(lambda b, t: (b, jnp.maximum(t - 1, 0), 0, 0))]
        ogla_shapes = [jax.ShapeDtypeStruct((g, nt, hb, GLA_WIDTH), BF16)] * 2
        body, grid = _in_gla_lag_body, (g, nt + 1)
        scratch = [proj, gbuf, gbuf, pltpu.VMEM((GLA_DV, GLA_KW), F32)]
    else:
        ogla_specs = [row(GLA_WIDTH)]
        ogla_shapes = [jax.ShapeDtypeStruct((g, tlen, GLA_WIDTH), BF16)]
        body, grid = _in_gla_batched_body, (g, nt)
        scratch = [proj, gbuf]
    return pl.pallas_call(
        body,
        grid=grid,
        in_specs=in_specs,
        out_specs=ogla_specs + common_specs,
        out_shape=ogla_shapes + common_shapes,
        scratch_shapes=scratch,
        compiler_params=pltpu.CompilerParams(dimension_semantics=("arbitrary", "arbitrary"),
                                             vmem_limit_bytes=VMEM_LIMIT),
        name="in_gla_lagged" if carry else "in_gla_batched",
    )(x, s0, cos, sin, kx, amask, *wts)


def _latent_attn_body(q_ref, ckvn_ref, krn_ref, ckvc_ref, krc_ref, mz_ref, wk_ref, wvt_ref,
                      sel_ref, o_ref, *, past):
    ts = q_ref.shape[1]
    nk = past + ts
    ckv = jnp.concatenate([ckvc_ref[...], ckvn_ref[...]], axis=0).astype(BF16)
    kr = jnp.concatenate([krc_ref[...], krn_ref[...]], axis=0).astype(BF16)
    krp = _dot(kr, sel_ref[...]).astype(BF16)
    keys = jnp.concatenate([ckv, krp], axis=1)
    qcat = jnp.concatenate(
        [jnp.concatenate([_dot_nt(q_ref[hh], wk_ref[:, hh * HEAD_PAD:(hh + 1) * HEAD_PAD])
                          .astype(BF16), q_ref[hh]], axis=1) for hh in range(MLA_HEADS)],
        axis=0)
    s = _dot_nt(keys, qcat)
    kchunk = lax.broadcasted_iota(jnp.int32, (nk, 1), 0) // CHUNK
    qpos = past + lax.broadcasted_iota(jnp.int32, (1, MLA_HEADS * ts), 1) % ts
    s = jnp.where(kchunk <= qpos // CHUNK, s, NEG_INF)
    p = jnp.exp2(s - jnp.max(s, axis=0, keepdims=True)).astype(BF16)
    vals = jnp.concatenate([ckv, jnp.ones((nk, LANES), BF16)], axis=1)
    ol = _dot_tn(p, vals)
    olat = (ol[:, :MLA_KV_RANK] / ol[:, MLA_KV_RANK:]).astype(BF16)
    for hh in range(MLA_HEADS):
        cols = slice(hh * MLA_DV, (hh + 1) * MLA_DV)
        o = _dot_nt(olat[hh * ts:(hh + 1) * ts, :], wvt_ref[cols, :])
        o_ref[:, cols] = (o * mz_ref[:, cols].astype(F32)).astype(BF16)


def _latent_attention(q, ckv_new, kr_new, ckv_cache, kr_cache, mz, wk, wvt, sel):
    bsz, past, _ = ckv_cache.shape
    ts = q.shape[2]
    per_b = lambda r, w: pl.BlockSpec((None, r, w), lambda b: (b, 0, 0))
    const = lambda a: pl.BlockSpec(a.shape, lambda b: (0,) * a.ndim)
    return pl.pallas_call(
        functools.partial(_latent_attn_body, past=past),
        grid=(bsz,),
        in_specs=[pl.BlockSpec((MLA_HEADS, None, ts, HEAD_PAD), lambda b: (0, b, 0, 0)),
                  per_b(ts, MLA_KV_RANK), per_b(ts, MLA_ROPE),
                  per_b(past, MLA_KV_RANK), per_b(past, MLA_ROPE), per_b(ts, MLA_WIDTH),
                  const(wk), const(wvt), const(sel)],
        out_specs=per_b(ts, MLA_WIDTH),
        out_shape=jax.ShapeDtypeStruct(mz.shape, BF16),
        compiler_params=pltpu.CompilerParams(dimension_semantics=("arbitrary",),
                                             vmem_limit_bytes=VMEM_LIMIT),
        name="latent_attn",
    )(q, ckv_new, kr_new, ckv_cache, kr_cache, mz, wk, wvt, sel)


def _attn_pipe_body(q_ref, k_ref, vt_ref, mz_ref, o_ref, q2_ref, s0_ref, s1_ref, c0_ref, c1_ref,
                    p0_ref, p1_ref, a0_ref, a1_ref, m_ref, l_ref, acc_ref):
    tq = s0_ref.shape[1]
    tk = k_ref.shape[1]
    nsub = tq // SUBQ
    ndiag = tq // tk
    nq = q_ref.shape[0] // tq
    all_subs = tuple(range(nsub))
    diag_subs = [tuple(u for u in all_subs if (u + 1) * SUBQ > d * tk) for d in range(ndiag)]
    slots = ((s0_ref, c0_ref, p0_ref, a0_ref), (s1_ref, c1_ref, p1_ref, a1_ref))

    def cols(u):
        return slice(u * SUBQ, (u + 1) * SUBQ)

    def scores(j, slot, subs, diagonal):
        s_ref, c_ref, _, _ = slot
        kt = k_ref[j]
        for u in subs:
            s = _dot_nt(kt, q2_ref[diagonal, cols(u), :])
            s_ref[:, cols(u)] = s
            c_ref[:, cols(u)] = jnp.max(s, axis=0, keepdims=True)

    def softmax(slot, subs):
        s_ref, c_ref, p_ref, a_ref = slot
        for u in subs:
            m_old = m_ref[:, cols(u)]
            m_new = jnp.maximum(m_old, c_ref[:, cols(u)])
            a_ref[:, cols(u)] = jnp.exp2(m_old - m_new)
            p_ref[:, cols(u)] = jnp.exp2((s_ref[:, cols(u)] - m_new).astype(BF16))
            m_ref[:, cols(u)] = m_new

    ones_rows = jnp.ones((ONES_ROWS, tk), BF16)

    def values(j, slot, subs):
        _, _, p_ref, a_ref = slot
        vt1 = jnp.concatenate([vt_ref[j], ones_rows], axis=0)
        for u in subs:
            pv = _dot(vt1, p_ref[:, cols(u)])
            alpha = a_ref[:, cols(u)]
            acc_ref[:, cols(u)] = alpha * acc_ref[:, cols(u)] + pv[:MLA_DV, :]
            l_ref[:, cols(u)] = alpha * l_ref[:, cols(u)] + pv[MLA_DV:MLA_DV + 1, :]

    def start_tile(i):
        q = q_ref[i * tq:(i + 1) * tq, :]
        lane = lax.broadcasted_iota(jnp.int32, q.shape, 1) - MASK_LANE0
        qchunk = lax.broadcasted_iota(jnp.int32, q.shape, 0) // CHUNK
        q2_ref[0] = q
        q2_ref[1] = jnp.where(lane > qchunk, NEG_INF, q.astype(F32)).astype(BF16)
        scores(0, slots[0], all_subs, 1 if i == 0 else 0)

    def pair(u, n_plain):
        scores(2 * u + 1, slots[1], all_subs, 0)
        softmax(slots[0], all_subs)
        values(2 * u - 1, slots[1], all_subs)
        scores(2 * u + 2, slots[0], all_subs, jnp.where(2 * u + 2 == n_plain, 1, 0))
        softmax(slots[1], all_subs)
        values(2 * u, slots[0], all_subs)

    start_tile(0)
    for i in range(nq):
        n_plain = ndiag * i
        m_ref[...] = jnp.full(m_ref.shape, NEG_INF, F32)
        l_ref[...] = jnp.zeros(l_ref.shape, F32)
        acc_ref[...] = jnp.zeros(acc_ref.shape, F32)
        if n_plain > 0:
            scores(1, slots[1], all_subs, 0)
            softmax(slots[0], all_subs)
            scores(2, slots[0], all_subs, 1 if n_plain == 2 else 0)
            softmax(slots[1], all_subs)
            values(0, slots[0], all_subs)
            lax.fori_loop(1, n_plain // 2, lambda u, c: (pair(u, n_plain), c)[1], 0)
        for d in range(ndiag):
            if d + 1 < ndiag:
                scores(n_plain + d + 1, slots[(d + 1) % 2], diag_subs[d + 1], 1)
            softmax(slots[d % 2], diag_subs[d])
            if n_plain + d > 0:
                values(n_plain + d - 1, slots[(d + 1) % 2],
                       all_subs if d == 0 else diag_subs[d - 1])
        values(n_plain + ndiag - 1, slots[(ndiag - 1) % 2], diag_subs[ndiag - 1])
        if i + 1 < nq:
            start_tile(i + 1)
        rows = slice(i * tq, (i + 1) * tq)
        o = (acc_ref[...] / l_ref[...]).T
        o_ref[rows, :] = (o * mz_ref[rows, :].astype(F32)).astype(BF16)


def _attention_pipelined(q, k, vt, mz):
    bsz, nh, tlen, _ = q.shape
    nkt, tk = k.shape[2], k.shape[3]
    tq = Q_TILE_CHUNKS * CHUNK
    assert tq % (2 * tk) == 0 and tlen % tq == 0
    kmap = lambda b, h: (b, h, 0, 0, 0)
    omap = lambda b, h: (b, 0, h)
    stat = pltpu.VMEM((1, tq), F32)
    return pl.pallas_call(
        _attn_pipe_body,
        grid=(bsz, nh),
        in_specs=[pl.BlockSpec((None, None, tlen, HEAD_PAD), lambda b, h: (b, h, 0, 0)),
                  pl.BlockSpec((None, None, nkt, tk, HEAD_PAD), kmap),
                  pl.BlockSpec((None, None, nkt, MLA_DV, tk), kmap),
                  pl.BlockSpec((None, tlen, MLA_DV), omap)],
        out_specs=pl.BlockSpec((None, tlen, MLA_DV), omap),
        out_shape=jax.ShapeDtypeStruct(mz.shape, BF16),
        scratch_shapes=[pltpu.VMEM((2, tq, HEAD_PAD), BF16),
                        pltpu.VMEM((tk, tq), F32), pltpu.VMEM((tk, tq), F32), stat, stat,
                        pltpu.VMEM((tk, tq), BF16), pltpu.VMEM((tk, tq), BF16), stat, stat,
                        stat, stat, pltpu.VMEM((MLA_DV, tq), F32)],
        compiler_params=pltpu.CompilerParams(
            dimension_semantics=("arbitrary", "arbitrary"),
            vmem_limit_bytes=VMEM_LIMIT),
        name="mla_attn_pipe",
    )(q, k, vt, mz)


def _out_mem_body(*refs, n_ogla):
    ogla_refs, refs = refs[:n_ogla], refs[n_ogla:]
    (x_ref, omla_ref, mk_ref, mv_ref, wout_ref, gq_ref, wmq_ref, wmo_ref, gfin_ref,
     y_ref, o_scr) = refs
    nb = mk_ref.shape[0]
    tm = x_ref.shape[0]
    rb = tm // nb
    hb = ogla_refs[0].shape[0]
    sb = min(OUT_ROW_BLOCK, tm)
    mb = min(sb, rb)

    def block_stages(r):
        rows = slice(r * sb, (r + 1) * sb)
        part, off = divmod(r * sb, hb)
        loc = {}

        def mix():
            ogla = ogla_refs[part][off:off + sb, :]
            loc["x1"] = (x_ref[rows, :] + _dot(ogla, wout_ref[0:GLA_WIDTH, :])
                         + _dot(omla_ref[rows, :], wout_ref[GLA_WIDTH:GLA_WIDTH + MLA_WIDTH, :]))

        def query():
            h = _rms(loc["x1"], gq_ref[...]).astype(BF16)
            loc["qm"] = (_dot(h, wmq_ref[...]) * (MEM_DH ** -0.5)).astype(BF16)

        def head(hh):
            def run():
                cols = slice(hh * MEM_DH, (hh + 1) * MEM_DH)
                for i in range(sb // mb):
                    bb = (r * sb + i * mb) // rb
                    s = _dot_nt(loc["qm"][i * mb:(i + 1) * mb, cols],
                                mk_ref[bb, :, cols].astype(BF16))
                    p = jnp.exp(s - jnp.max(s, axis=1, keepdims=True))
                    p = p / jnp.sum(p, axis=1, keepdims=True)
                    o_scr[r * sb + i * mb:r * sb + (i + 1) * mb, cols] = _dot(
                        p.astype(BF16), mv_ref[bb, :, cols].astype(BF16)).astype(BF16)
            return run

        def out():
            x2 = loc["x1"] + _dot(o_scr[rows, :], wmo_ref[...])
            y_ref[rows, :] = _rms(x2, gfin_ref[...])

        return [mix, query] + [head(hh) for hh in range(MEM_HEADS)] + [out]

    for stage in _skewed([block_stages(r) for r in range(tm // sb)]):
        stage()


def _out_mem(x, ogla_parts, omla, mk, mv, wts, *, nb):
    g, tlen, _ = x.shape
    hb = ogla_parts[0].shape[2]
    tm = hb * len(ogla_parts)
    nt = tlen // tm
    mmap = (lambda b, t: (b, 0, 0)) if nb == 1 else (lambda b, t: (t, 0, 0))
    row = lambda w: pl.BlockSpec((None, tm, w), lambda b, t: (b, t, 0))
    part = pl.BlockSpec((None, None, hb, GLA_WIDTH), lambda b, t: (b, t, 0, 0))
    mspec = pl.BlockSpec((nb, N_MEM, D_MODEL), mmap)
    sq = pl.BlockSpec((D_MODEL, D_MODEL), lambda b, t: (0, 0))
    vec = pl.BlockSpec((1, D_MODEL), lambda b, t: (0, 0))
    return pl.pallas_call(
        functools.partial(_out_mem_body, n_ogla=len(ogla_parts)),
        grid=(g, nt),
        in_specs=[part] * len(ogla_parts) + [row(D_MODEL), row(MLA_WIDTH), mspec, mspec,
                                              sq, vec, sq, sq, vec],
        out_specs=row(D_MODEL),
        out_shape=jax.ShapeDtypeStruct(x.shape, F32),
        scratch_shapes=[pltpu.VMEM((tm, D_MODEL), BF16)],
        compiler_params=pltpu.CompilerParams(dimension_semantics=("arbitrary", "arbitrary"),
                                             vmem_limit_bytes=VMEM_LIMIT),
        name="out_mem",
    )(*ogla_parts, x, omla, mk, mv, *wts)


def _prep_w_in(w_in):
    splits = np.cumsum([GLA_KW, GLA_KW, GLA_WIDTH, GLA_GATE_RANK, GLA_WIDTH, MLA_Q_RANK,
                        MLA_KV_RANK, MLA_ROPE])
    gq, gk, gv, glr, gz, cq, ckv, kr, mz = jnp.split(w_in, splits.tolist(), axis=1)
    zeros = lambda n: jnp.zeros((D_MODEL, n), w_in.dtype)
    misc = jnp.concatenate([glr, zeros(ROPE_LANE0 - GLA_GATE_RANK), kr,
                            zeros(LANES - ROPE_LANE0 - MLA_ROPE)], axis=1)
    return jnp.concatenate([misc, gq, gk, gv, gz, cq, ckv, mz], axis=1).astype(BF16)


def _pad_heads(w, width):
    kdim = w.shape[0]
    w = w.reshape(kdim, MLA_HEADS, width)
    w = jnp.pad(w, ((0, 0), (0, 0), (0, HEAD_PAD - width)))
    return w.reshape(kdim, MLA_HEADS * HEAD_PAD)


def _rope_tables(pos):
    inv = ROPE_THETA ** (-jnp.arange(HALF_ROPE, dtype=F32) / HALF_ROPE)
    ang = pos.astype(F32)[:, None] * inv[None, :]
    c, s = jnp.cos(ang), jnp.sin(ang)
    n = pos.shape[0]
    cos = jnp.concatenate([jnp.ones((n, ROPE_LANE0), F32), c, c,
                           jnp.zeros((n, LANES - ROPE_LANE0 - MLA_ROPE), F32)], axis=1)
    sin = jnp.concatenate([jnp.zeros((n, ROPE_LANE0), F32), -s, s,
                           jnp.zeros((n, LANES - ROPE_LANE0 - MLA_ROPE), F32)], axis=1)
    return cos, sin


def _chunk_onehot(pos):
    cidx = (np.asarray(pos) // CHUNK) % Q_TILE_CHUNKS
    tab = np.zeros((len(cidx), LANES), np.float32)
    tab[np.arange(len(cidx)), MASK_LANE0 + cidx] = 1.0
    return jnp.asarray(tab)


def _gla_mask():
    r = np.arange(GLA_HEADS * CHUNK)
    same_head = (r[:, None] // CHUNK) == (r[None, :] // CHUNK)
    causal = (r[None, :] % CHUNK) <= (r[:, None] % CHUNK)
    return jnp.asarray((same_head & causal).astype(np.float32))


def kernel(x_prompt, x_sample, mem_prompt, cache_mla_ckv, cache_mla_krope, state_gla, cache_mem_k, cache_mem_v, g_mix, w_in, w_gla_g2, b_gla_g, g_gla_out, g_qa, w_qb, g_kva, w_kvb, w_out, g_mem_q, g_mem_kv, w_mq, w_mk, w_mv, w_mo, g_final):
    bp, tp, _ = x_prompt.shape
    bs, ts, _ = x_sample.shape
    past = cache_mla_ckv.shape[2]
    assert ts == CHUNK and g_mix.shape[0] == 1
    l = 0
    row = lambda g: g.reshape(1, -1)

    w_in_p = _prep_w_in(w_in[l])
    wg2_p = jnp.pad(w_gla_g2[l], ((0, LANES - GLA_GATE_RANK), (0, 0))).astype(BF16)
    wqb_p = _pad_heads(w_qb[l], MLA_NOPE + MLA_ROPE).astype(BF16)
    wkv = w_kvb[l].reshape(MLA_KV_RANK, MLA_HEADS, MLA_NOPE + MLA_DV)
    wk_p = _pad_heads(wkv[:, :, :MLA_NOPE].reshape(MLA_KV_RANK, -1), MLA_NOPE).astype(BF16)
    wvt_p = wkv[:, :, MLA_NOPE:].reshape(MLA_KV_RANK, MLA_WIDTH).T.astype(BF16)
    in_wts = (row(g_mix[l]), w_in_p, wg2_p, row(b_gla_g[l]), row(g_gla_out[l]), row(g_qa[l]),
              wqb_p, row(g_kva[l]), wk_p, wvt_p)
    out_wts = (w_out[l].astype(BF16), row(g_mem_q[l]), w_mq[l].astype(BF16),
               w_mo[l].astype(BF16), row(g_final))
    amask = _gla_mask()
    sel = jnp.zeros((MLA_ROPE, HEAD_PAD), BF16).at[
        jnp.arange(MLA_ROPE), ROPE_LANE0 + jnp.arange(MLA_ROPE)].set(1.0)

    tm = 512
    sample_rows = bs * ts
    bpt = tm // ts

    mk_p, mv_p = _mem_kv(mem_prompt.reshape(bp * N_MEM, D_MODEL), row(g_mem_kv[l]),
                         w_mk[l].astype(BF16), w_mv[l].astype(BF16), tm)
    cos_p, sin_p = _rope_tables(jnp.arange(tp, dtype=jnp.int32))
    s0_p = jnp.zeros((bp, GLA_KW, GLA_DV), F32)
    ogla_a, ogla_b, mz_p, q_p, k_p, vt_p, ckv_p, kr_p, st_p = _in_gla(
        x_prompt, s0_p, cos_p, sin_p, _chunk_onehot(np.arange(tp)), amask, in_wts,
        hb=tm, carry=True)
    omla_p = _attention_pipelined(
        q_p, k_p.reshape(bp, MLA_HEADS, tp // tm, tm, HEAD_PAD), vt_p, mz_p)
    y_prompt = _out_mem(x_prompt, (ogla_a, ogla_b), omla_p, mk_p.reshape(bp, N_MEM, D_MODEL),
                        mv_p.reshape(bp, N_MEM, D_MODEL), out_wts, nb=1)

    xs = x_sample.reshape(1, sample_rows, D_MODEL)
    cos_s, sin_s = _rope_tables(past + jnp.arange(ts, dtype=jnp.int32))
    cos_s, sin_s = jnp.tile(cos_s, (bpt, 1)), jnp.tile(sin_s, (bpt, 1))
    s0_s = state_gla[l].reshape(bs, GLA_KW, GLA_DV)
    ogla_s, mz_s, q_s, _, _, ckv_s, kr_s, st_s = _in_gla(
        xs, s0_s, cos_s, sin_s, jnp.zeros((tm, LANES), F32), amask, in_wts, hb=tm, carry=False)
    omla_s = _latent_attention(
        q_s.reshape(MLA_HEADS, bs, ts, HEAD_PAD), ckv_s.reshape(bs, ts, MLA_KV_RANK),
        kr_s.reshape(bs, ts, MLA_ROPE), cache_mla_ckv[l], cache_mla_krope[l],
        mz_s.reshape(bs, ts, MLA_WIDTH), wk_p, wvt_p, sel)
    omla_s = omla_s.reshape(1, sample_rows, MLA_WIDTH)
    nb_s = 4
    y_sample = _out_mem(xs, (ogla_s.reshape(1, bs // nb_s, nb_s * ts, GLA_WIDTH),), omla_s,
                        cache_mem_k[l].reshape(bs, N_MEM, D_MODEL),
                        cache_mem_v[l].reshape(bs, N_MEM, D_MODEL), out_wts, nb=nb_s)

    mem_shape = (1, bp, N_MEM, MEM_HEADS, MEM_DH)
    st_shape = (GLA_HEADS, GLA_DK, GLA_DV)
    return (y_prompt, y_sample.reshape(bs, ts, D_MODEL),
            ckv_p[None], kr_p[None], st_p.reshape(1, bp, *st_shape),
            mk_p.reshape(mem_shape), mv_p.reshape(mem_shape),
            ckv_s.reshape(1, bs, ts, MLA_KV_RANK), kr_s.reshape(1, bs, ts, MLA_ROPE),
            st_s.reshape(1, bs, *st_shape))
```

```python
import functools

import jax
import jax.numpy as jnp
import numpy as np
from jax import lax
from jax.experimental import pallas as pl
from jax.experimental.pallas import tpu as pltpu

D_MODEL = 1024
CHUNK = 64
EPS = 1e-6
NEG_INF = -1e30

GLA_HEADS = 4
GLA_DK = 64
GLA_DV = 128
GLA_GATE_RANK = 16
GLA_GATE_NORM = 16.0
GLA_KW = GLA_HEADS * GLA_DK
GLA_WIDTH = GLA_HEADS * GLA_DV

MLA_HEADS = 4
MLA_Q_RANK = 256
MLA_KV_RANK = 128
MLA_NOPE = 64
MLA_ROPE = 32
MLA_DV = 128
MLA_WIDTH = MLA_HEADS * MLA_DV
MLA_SCALE = (MLA_NOPE + MLA_ROPE) ** -0.5
ROPE_THETA = 10000.0
HALF_ROPE = MLA_ROPE // 2
Q_PRESCALE = MLA_SCALE * float(np.log2(np.e))
SUBQ = 256
LANES = 128
HEAD_PAD = LANES
MASK_LANE0 = MLA_NOPE + MLA_ROPE
Q_TILE_CHUNKS = HEAD_PAD - MASK_LANE0
ONES_ROWS = 16
ROW_BLOCK = 512
OUT_ROW_BLOCK = 256
N_IN_WTS = 10

MEM_HEADS = 4
MEM_DH = D_MODEL // MEM_HEADS
N_MEM = 256

ROPE_LANE0 = MLA_NOPE

COL_MISC = 0
COL_Q = COL_MISC + LANES
COL_K = COL_Q + GLA_KW
COL_V = COL_K + GLA_KW
COL_Z = COL_V + GLA_WIDTH
COL_CQ = COL_Z + GLA_WIDTH
COL_CKV = COL_CQ + MLA_Q_RANK
COL_MZ = COL_CKV + MLA_KV_RANK
D_IN_PAD = COL_MZ + MLA_WIDTH
COL_MLA = COL_CQ
COL_LA = COL_MLA
GBUF_W = COL_LA + GLA_KW
PROJ_PIECE = 256

VMEM_LIMIT = 52 * 1024 * 1024

BF16 = jnp.bfloat16
F32 = jnp.float32


def _rms(x, g):
    ms = jnp.mean(x * x, axis=-1, keepdims=True)
    return x * lax.rsqrt(ms + EPS) * g


def _dot(a, b):
    return jnp.dot(a, b, preferred_element_type=F32)


def _dot_nt(a, b):
    return lax.dot_general(a, b, (((1,), (1,)), ((), ())), preferred_element_type=F32)


def _dot_tn(a, b):
    return lax.dot_general(a, b, (((0,), (0,)), ((), ())), preferred_element_type=F32)


def _rope_group(x, cos, sin):
    lane = lax.broadcasted_iota(jnp.int32, x.shape, 1)
    partner = jnp.where(lane < ROPE_LANE0 + HALF_ROPE,
                        pltpu.roll(x, LANES - HALF_ROPE, 1),
                        pltpu.roll(x, HALF_ROPE, 1))
    return x * cos + partner * sin


def _mem_kv_body(mem_ref, g_ref, wk_ref, wv_ref, mk_ref, mv_ref):
    m = _rms(mem_ref[...], g_ref[...]).astype(BF16)
    mk_ref[...] = _dot(m, wk_ref[...])
    mv_ref[...] = _dot(m, wv_ref[...])


def _mem_kv(mem2d, g, wk, wv, tm):
    rows = mem2d.shape[0]
    wspec = pl.BlockSpec((D_MODEL, D_MODEL), lambda i: (0, 0))
    ospec = pl.BlockSpec((tm, D_MODEL), lambda i: (i, 0))
    return pl.pallas_call(
        _mem_kv_body,
        grid=(rows // tm,),
        in_specs=[pl.BlockSpec((tm, D_MODEL), lambda i: (i, 0)),
                  pl.BlockSpec((1, D_MODEL), lambda i: (0, 0)), wspec, wspec],
        out_specs=[ospec, ospec],
        out_shape=[jax.ShapeDtypeStruct((rows, D_MODEL), F32)] * 2,
        compiler_params=pltpu.CompilerParams(dimension_semantics=("arbitrary",),
                                             vmem_limit_bytes=VMEM_LIMIT),
        name="mem_kv",
    )(mem2d, g, wk, wv)


class _InGla:
    def __init__(self, tabs, wts, amask_ref):
        self.cos_ref, self.sin_ref, self.kx_ref = tabs
        (self.gmix_ref, self.win_ref, self.wg2_ref, self.bg_ref, self.ggla_ref, self.gqa_ref,
         self.wqb_ref, self.gkva_ref, self.wk_ref, self.wvt_ref) = wts
        self.amask_ref = amask_ref
        ri = lax.broadcasted_iota(jnp.int32, (CHUNK, CHUNK), 0)
        ci = lax.broadcasted_iota(jnp.int32, (CHUNK, CHUNK), 1)
        tril = (ci <= ri).astype(BF16)
        self.tril3 = jnp.concatenate([tril, tril, tril], axis=1)
        klane = lax.broadcasted_iota(jnp.int32, (CHUNK, GLA_KW), 1)
        self.head_of_lane = klane // GLA_DK

    def prep_pieces(self, x_ref, rows, gbuf, proj_ref, outs, vt_dst):
        mz_ref, q_ref, k_ref, ckv_ref, kr_ref = outs
        loc = {}

        def norm():
            loc["h"] = _rms(x_ref[rows, :], self.gmix_ref[...]).astype(BF16)

        def proj(dst, d0, c0, c1):
            def run():
                dst[:, d0:d0 + c1 - c0] = _dot(loc["h"], self.win_ref[:, c0:c1])
            return run

        def queries():
            cos, sin = self.cos_ref[rows, :], self.sin_ref[rows, :]
            cq = proj_ref[:, COL_CQ - COL_MLA:COL_CQ - COL_MLA + MLA_Q_RANK]
            qf = _dot(_rms(cq, self.gqa_ref[...]).astype(BF16), self.wqb_ref[...])
            for hh in range(MLA_HEADS):
                qh = _rope_group(qf[:, hh * HEAD_PAD:(hh + 1) * HEAD_PAD], cos, sin)
                q_ref[hh, rows, :] = (qh * Q_PRESCALE).astype(BF16)

        def keys_values():
            cos, sin = self.cos_ref[rows, :], self.sin_ref[rows, :]
            ckv = _rms(proj_ref[:, COL_CKV - COL_MLA:COL_CKV - COL_MLA + MLA_KV_RANK],
                       self.gkva_ref[...])
            ckv_ref[rows, :] = ckv
            ckv_b = ckv.astype(BF16)
            misc = gbuf[:, COL_MISC:COL_MISC + LANES]
            lane = lax.broadcasted_iota(jnp.int32, misc.shape, 1)
            kr = jnp.where(lane >= ROPE_LANE0, _rope_group(misc, cos, sin), 0.0)
            kr_ref[rows, :] = kr[:, ROPE_LANE0:ROPE_LANE0 + MLA_ROPE]
            kr = kr + self.kx_ref[rows, :]
            kn = _dot(ckv_b, self.wk_ref[...])
            vt = _dot_nt(self.wvt_ref[...], ckv_b)
            for hh in range(MLA_HEADS):
                k_ref[hh, rows, :] = (kn[:, hh * HEAD_PAD:(hh + 1) * HEAD_PAD] + kr).astype(BF16)
                vt_dst(hh, vt[hh * MLA_DV:(hh + 1) * MLA_DV, :].astype(BF16))

        def gates():
            mz = proj_ref[:, COL_MZ - COL_MLA:COL_MZ - COL_MLA + MLA_WIDTH]
            mz_ref[rows, :] = (mz * jax.nn.sigmoid(mz)).astype(BF16)
            misc = gbuf[:, COL_MISC:COL_MISC + LANES]
            z = _dot(misc.astype(BF16), self.wg2_ref[...]) + self.bg_ref[...]
            gbuf[:, COL_LA:COL_LA + GLA_KW] = ((jnp.minimum(z, 0.0)
                                                - jnp.log1p(jnp.exp(-jnp.abs(z))))
                                               * (1.0 / GLA_GATE_NORM))

        cuts = list(range(0, COL_MLA, PROJ_PIECE)) + [COL_MLA]
        to_gbuf = [proj(gbuf, c0, c0, c1) for c0, c1 in zip(cuts[:-1], cuts[1:])]
        cuts = list(range(COL_MLA, D_IN_PAD, PROJ_PIECE)) + [D_IN_PAD]
        to_proj = [proj(proj_ref, c0 - COL_MLA, c0, c1) for c0, c1 in zip(cuts[:-1], cuts[1:])]
        return ([norm] + to_proj + to_gbuf[:1] + [queries, keys_values] + to_gbuf[1:2] + [gates]
                + to_gbuf[2:])

    def chunk_stages(self, c, gbuf, ogla_ref, state):
        rows = slice(c * CHUNK, (c + 1) * CHUNK)
        loc = {}

        def per_head(a):
            return jnp.concatenate(
                [jnp.where(self.head_of_lane == hh, a, 0.0) for hh in range(GLA_HEADS)], axis=0)

        def decay():
            la = gbuf[rows, COL_LA:COL_LA + GLA_KW]
            la_hi = la.astype(BF16)
            r1 = la - la_hi.astype(F32)
            la_mid = r1.astype(BF16)
            la_lo = (r1 - la_mid.astype(F32)).astype(BF16)
            loc["b"] = _dot(self.tril3, jnp.concatenate([la_hi, la_mid, la_lo], axis=0))

        def inner():
            b = loc["b"]
            b_last = b[CHUNK - 1:CHUNK, :]
            q = gbuf[rows, COL_Q:COL_Q + GLA_KW] * (GLA_DK ** -0.5)
            k = gbuf[rows, COL_K:COL_K + GLA_KW]
            loc["sdec"] = jnp.exp(b_last)
            loc["qs"] = per_head(q * jnp.exp(b)).astype(BF16)
            kt = jnp.concatenate([(k * jnp.exp(-b)).astype(BF16)] * GLA_HEADS, axis=0)
            loc["kd2"] = per_head(k * jnp.exp(b_last - b)).astype(BF16)
            loc["a"] = _dot_nt(loc["qs"], kt)

        def outer():
            a = jnp.where(self.amask_ref[...] != 0.0, loc["a"], 0.0).astype(BF16)
            v = gbuf[rows, COL_V:COL_V + GLA_WIDTH]
            vs = jnp.concatenate([v[:, hh * GLA_DV:(hh + 1) * GLA_DV] for hh in range(GLA_HEADS)],
                                 axis=0).astype(BF16)
            st = state["st"]
            loc["o"] = _dot_nt(loc["qs"], st.astype(BF16)) + _dot(a, vs)
            state["st"] = st * loc["sdec"] + _dot_tn(vs, loc["kd2"])

        def emit():
            gz = gbuf[rows, COL_Z:COL_Z + GLA_WIDTH]
            gate = gz * jax.nn.sigmoid(gz)
            on = _rms(loc["o"], self.ggla_ref[...])
            for hh in range(GLA_HEADS):
                cols = slice(hh * GLA_DV, (hh + 1) * GLA_DV)
                ogla_ref[rows, cols] = (on[hh * CHUNK:(hh + 1) * CHUNK, :]
                                        * gate[:, cols]).astype(BF16)

        return [decay, inner, outer, emit]


def _skewed(stage_lists):
    depth = len(stage_lists[0])
    order = []
    for tau in range(len(stage_lists) + depth - 1):
        for s in range(depth):
            c = tau - s
            if 0 <= c < len(stage_lists):
                order.append(stage_lists[c][s])
    return order


def _interleave(main, fill):
    done = 0
    for n, thunk in enumerate(main):
        thunk()
        want = (n + 1) * len(fill) // len(main)
        for piece in fill[done:want]:
            piece()
        done = want


def _in_gla_lag_body(x_ref, s0_ref, cos_ref, sin_ref, kx_ref, amask_ref, *refs):
    wts, refs = refs[:N_IN_WTS], refs[N_IN_WTS:]
    (ogla_a_ref, ogla_b_ref, mz_ref, q_ref, k_ref, vt_ref, ckv_ref, kr_ref, sfin_ref,
     proj_ref, gbuf_a, gbuf_b, st_ref) = refs
    hb = gbuf_a.shape[0]
    t = pl.program_id(1)
    last = pl.num_programs(1) - 1
    parts = _InGla((cos_ref, sin_ref, kx_ref), wts, amask_ref)
    outs = (mz_ref, q_ref, k_ref, ckv_ref, kr_ref)

    @pl.when(t == 0)
    def _():
        st_ref[...] = s0_ref[0].T
        gbuf_b[...] = jnp.zeros(gbuf_b.shape, F32)

    def bracket(gbuf_prev, ogla_ref, half, gbuf_next):
        state = {"st": st_ref[...]}
        rows = slice(half * hb, (half + 1) * hb)
        stages = _skewed([parts.chunk_stages(c, gbuf_prev, ogla_ref, state)
                          for c in range(hb // CHUNK)])
        pieces = parts.prep_pieces(
            x_ref, rows, gbuf_next, proj_ref, outs,
            lambda hh, val: vt_ref.__setitem__((hh, half), val))
        _interleave(stages, pieces)
        st_ref[...] = state["st"]

    bracket(gbuf_b, ogla_b_ref, 0, gbuf_a)

    @pl.when(t < last)
    def _():
        bracket(gbuf_a, ogla_a_ref, 1, gbuf_b)

    @pl.when(t == last)
    def _():
        sfin_ref[0] = st_ref[...].T


def _in_gla_batched_body(x_ref, s0_ref, cos_ref, sin_ref, kx_ref, amask_ref, *refs):
    wts, refs = refs[:N_IN_WTS], refs[N_IN_WTS:]
    (ogla_ref, mz_ref, q_ref, k_ref, vt_ref, ckv_ref, kr_ref, sfin_ref, proj_ref, gbuf) = refs
    tm = x_ref.shape[0]
    parts = _InGla((cos_ref, sin_ref, kx_ref), wts, amask_ref)
    for piece in parts.prep_pieces(
            x_ref, slice(0, tm), gbuf, proj_ref, (mz_ref, q_ref, k_ref, ckv_ref, kr_ref),
            lambda hh, val: vt_ref.__setitem__((hh, 0), val)):
        piece()
    states = [{"st": s0_ref[c].T} for c in range(tm // CHUNK)]
    for stage in _skewed([parts.chunk_stages(c, gbuf, ogla_ref, states[c])
                          for c in range(tm // CHUNK)]):
        stage()
    for c in range(tm // CHUNK):
        sfin_ref[c] = states[c]["st"].T


def _in_gla(x, s0, cos, sin, kx, amask, wts, *, hb, carry):
    g, tlen, _ = x.shape
    tm = 2 * hb if carry else hb
    nt = tlen // tm
    ns = 1 if carry else tm // CHUNK
    clamp = (lambda t: jnp.minimum(t, nt - 1)) if carry else (lambda t: t)
    smap = (lambda b, t: (b, 0, 0)) if carry else (lambda b, t: (t, 0, 0))
    tab_tiles = cos.shape[0] // tm
    tmap = (lambda b, t: (clamp(t), 0)) if tab_tiles > 1 else (lambda b, t: (0, 0))

    def const(a):
        return pl.BlockSpec(a.shape, lambda b, t: (0,) * a.ndim, pipeline_mode=pl.Buffered(1))

    row = lambda w: pl.BlockSpec((None, tm, w), lambda b, t: (b, clamp(t), 0))
    heads = pl.BlockSpec((None, MLA_HEADS, tm, HEAD_PAD), lambda b, t: (b, 0, clamp(t), 0))
    tab = pl.BlockSpec((tm, LANES), tmap)
    in_specs = [row(D_MODEL), pl.BlockSpec((ns, GLA_KW, GLA_DV), smap), tab, tab, tab,
                const(amask)] + [const(w) for w in wts]
    vt_spec = pl.BlockSpec((None, MLA_HEADS, tm // hb, MLA_DV, hb),
                           lambda b, t: (b, 0, clamp(t), 0, 0))
    common_specs = [row(MLA_WIDTH), heads, heads, vt_spec, row(MLA_KV_RANK), row(MLA_ROPE),
                    pl.BlockSpec((ns, GLA_KW, GLA_DV), smap)]
    hshape = jax.ShapeDtypeStruct((g, MLA_HEADS, tlen, HEAD_PAD), BF16)
    common_shapes = [
        jax.ShapeDtypeStruct((g, tlen, MLA_WIDTH), BF16), hshape, hshape,
        jax.ShapeDtypeStruct((g, MLA_HEADS, tlen // hb, MLA_DV, hb), BF16),
        jax.ShapeDtypeStruct((g, tlen, MLA_KV_RANK), F32),
        jax.ShapeDtypeStruct((g, tlen, MLA_ROPE), F32),
        jax.ShapeDtypeStruct(s0.shape, F32),
    ]
    gbuf = pltpu.VMEM((hb, GBUF_W), F32)
    proj = pltpu.VMEM((hb, D_IN_PAD - COL_MLA), F32)
    if carry:
        blk = lambda imap: pl.BlockSpec((None, None, hb, GLA_WIDTH), imap)
        ogla_specs = [blk(lambda b, t: (b, clamp(t), 0, 0)),
                      blk(lambda b, t: (b, jnp.maximum(t - 1, 0), 0, 0))]
        ogla_shapes = [jax.ShapeDtypeStruct((g, nt, hb, GLA_WIDTH), BF16)] * 2
        body, grid = _in_gla_lag_body, (g, nt + 1)
        scratch = [proj, gbuf, gbuf, pltpu.VMEM((GLA_DV, GLA_KW), F32)]
    else:
        ogla_specs = [row(GLA_WIDTH)]
        ogla_shapes = [jax.ShapeDtypeStruct((g, tlen, GLA_WIDTH), BF16)]
        body, grid = _in_gla_batched_body, (g, nt)
        scratch = [proj, gbuf]
    return pl.pallas_call(
        body,
        grid=grid,
        in_specs=in_specs,
        out_specs=ogla_specs + common_specs,
        out_shape=ogla_shapes + common_shapes,
        scratch_shapes=scratch,
        compiler_params=pltpu.CompilerParams(dimension_semantics=("arbitrary", "arbitrary"),
                                             vmem_limit_bytes=VMEM_LIMIT),
        name="in_gla_lagged" if carry else "in_gla_batched",
    )(x, s0, cos, sin, kx, amask, *wts)


def _latent_attn_body(q_ref, ckvn_ref, krn_ref, ckvc_ref, krc_ref, mz_ref, wk_ref, wvt_ref,
                      sel_ref, o_ref, *, past):
    ts = q_ref.shape[1]
    nk = past + ts
    ckv = jnp.concatenate([ckvc_ref[...], ckvn_ref[...]], axis=0).astype(BF16)
    kr = jnp.concatenate([krc_ref[...], krn_ref[...]], axis=0).astype(BF16)
    krp = _dot(kr, sel_ref[...]).astype(BF16)
    keys = jnp.concatenate([ckv, krp], axis=1)
    qcat = jnp.concatenate(
        [jnp.concatenate([_dot_nt(q_ref[hh], wk_ref[:, hh * HEAD_PAD:(hh + 1) * HEAD_PAD])
                          .astype(BF16), q_ref[hh]], axis=1) for hh in range(MLA_HEADS)],
        axis=0)
    s = _dot_nt(keys, qcat)
    kchunk = lax.broadcasted_iota(jnp.int32, (nk, 1), 0) // CHUNK
    qpos = past + lax.broadcasted_iota(jnp.int32, (1, MLA_HEADS * ts), 1) % ts
    s = jnp.where(kchunk <= qpos // CHUNK, s, NEG_INF)
    p = jnp.exp2(s - jnp.max(s, axis=0, keepdims=True)).astype(BF16)
    vals = jnp.concatenate([ckv, jnp.ones((nk, LANES), BF16)], axis=1)
    ol = _dot_tn(p, vals)
    olat = (ol[:, :MLA_KV_RANK] / ol[:, MLA_KV_RANK:]).astype(BF16)
    for hh in range(MLA_HEADS):
        cols = slice(hh * MLA_DV, (hh + 1) * MLA_DV)
        o = _dot_nt(olat[hh * ts:(hh + 1) * ts, :], wvt_ref[cols, :])
        o_ref[:, cols] = (o * mz_ref[:, cols].astype(F32)).astype(BF16)


def _latent_attention(q, ckv_new, kr_new, ckv_cache, kr_cache, mz, wk, wvt, sel):
    bsz, past, _ = ckv_cache.shape
    ts = q.shape[2]
    per_b = lambda r, w: pl.BlockSpec((None, r, w), lambda b: (b, 0, 0))
    const = lambda a: pl.BlockSpec(a.shape, lambda b: (0,) * a.ndim)
    return pl.pallas_call(
        functools.partial(_latent_attn_body, past=past),
        grid=(bsz,),
        in_specs=[pl.BlockSpec((MLA_HEADS, None, ts, HEAD_PAD), lambda b: (0, b, 0, 0)),
                  per_b(ts, MLA_KV_RANK), per_b(ts, MLA_ROPE),
                  per_b(past, MLA_KV_RANK), per_b(past, MLA_ROPE), per_b(ts, MLA_WIDTH),
                  const(wk), const(wvt), const(sel)],
        out_specs=per_b(ts, MLA_WIDTH),
        out_shape=jax.ShapeDtypeStruct(mz.shape, BF16),
        compiler_params=pltpu.CompilerParams(dimension_semantics=("arbitrary",),
                                             vmem_limit_bytes=VMEM_LIMIT),
        name="latent_attn",
    )(q, ckv_new, kr_new, ckv_cache, kr_cache, mz, wk, wvt, sel)


def _attn_pipe_body(q_ref, k_ref, vt_ref, mz_ref, o_ref, q2_ref, s0_ref, s1_ref, c0_ref, c1_ref,
                    p0_ref, p1_ref, a0_ref, a1_ref, m_ref, l_ref, acc_ref):
    tq = s0_ref.shape[1]
    tk = k_ref.shape[1]
    nsub = tq // SUBQ
    ndiag = tq // tk
    nq = q_ref.shape[0] // tq
    all_subs = tuple(range(nsub))
    diag_subs = [tuple(u for u in all_subs if (u + 1) * SUBQ > d * tk) for d in range(ndiag)]
    slots = ((s0_ref, c0_ref, p0_ref, a0_ref), (s1_ref, c1_ref, p1_ref, a1_ref))

    def cols(u):
        return slice(u * SUBQ, (u + 1) * SUBQ)

    def scores(j, slot, subs, diagonal):
        s_ref, c_ref, _, _ = slot
        kt = k_ref[j]
        for u in subs:
            s = _dot_nt(kt, q2_ref[diagonal, cols(u), :])
            s_ref[:, cols(u)] = s
            c_ref[:, cols(u)] = jnp.max(s, axis=0, keepdims=True)

    def softmax(slot, subs):
        s_ref, c_ref, p_ref, a_ref = slot
        for u in subs:
            m_old = m_ref[:, cols(u)]
            m_new = jnp.maximum(m_old, c_ref[:, cols(u)])
            a_ref[:, cols(u)] = jnp.exp2(m_old - m_new)
            p_ref[:, cols(u)] = jnp.exp2((s_ref[:, cols(u)] - m_new).astype(BF16))
            m_ref[:, cols(u)] = m_new

    ones_rows = jnp.ones((ONES_ROWS, tk), BF16)

    def values(j, slot, subs):
        _, _, p_ref, a_ref = slot
        vt1 = jnp.concatenate([vt_ref[j], ones_rows], axis=0)
        for u in subs:
            pv = _dot(vt1, p_ref[:, cols(u)])
            alpha = a_ref[:, cols(u)]
            acc_ref[:, cols(u)] = alpha * acc_ref[:, cols(u)] + pv[:MLA_DV, :]
            l_ref[:, cols(u)] = alpha * l_ref[:, cols(u)] + pv[MLA_DV:MLA_DV + 1, :]

    def start_tile(i):
        q = q_ref[i * tq:(i + 1) * tq, :]
        lane = lax.broadcasted_iota(jnp.int32, q.shape, 1) - MASK_LANE0
        qchunk = lax.broadcasted_iota(jnp.int32, q.shape, 0) // CHUNK
        q2_ref[0] = q
        q2_ref[1] = jnp.where(lane > qchunk, NEG_INF, q.astype(F32)).astype(BF16)
        scores(0, slots[0], all_subs, 1 if i == 0 else 0)

    def pair(u, n_plain):
        scores(2 * u + 1, slots[1], all_subs, 0)
        softmax(slots[0], all_subs)
        values(2 * u - 1, slots[1], all_subs)
        scores(2 * u + 2, slots[0], all_subs, jnp.where(2 * u + 2 == n_plain, 1, 0))
        softmax(slots[1], all_subs)
        values(2 * u, slots[0], all_subs)

    start_tile(0)
    for i in range(nq):
        n_plain = ndiag * i
        m_ref[...] = jnp.full(m_ref.shape, NEG_INF, F32)
        l_ref[...] = jnp.zeros(l_ref.shape, F32)
        acc_ref[...] = jnp.zeros(acc_ref.shape, F32)
        if n_plain > 0:
            scores(1, slots[1], all_subs, 0)
            softmax(slots[0], all_subs)
            scores(2, slots[0], all_subs, 1 if n_plain == 2 else 0)
            softmax(slots[1], all_subs)
            values(0, slots[0], all_subs)
            lax.fori_loop(1, n_plain // 2, lambda u, c: (pair(u, n_plain), c)[1], 0)
        for d in range(ndiag):
            if d + 1 < ndiag:
                scores(n_plain + d + 1, slots[(d + 1) % 2], diag_subs[d + 1], 1)
            softmax(slots[d % 2], diag_subs[d])
            if n_plain + d > 0:
                values(n_plain + d - 1, slots[(d + 1) % 2],
                       all_subs if d == 0 else diag_subs[d - 1])
        values(n_plain + ndiag - 1, slots[(ndiag - 1) % 2], diag_subs[ndiag - 1])
        if i + 1 < nq:
            start_tile(i + 1)
        rows = slice(i * tq, (i + 1) * tq)
        o = (acc_ref[...] / l_ref[...]).T
        o_ref[rows, :] = (o * mz_ref[rows, :].astype(F32)).astype(BF16)


def _attention_pipelined(q, k, vt, mz):
    bsz, nh, tlen, _ = q.shape
    nkt, tk = k.shape[2], k.shape[3]
    tq = Q_TILE_CHUNKS * CHUNK
    assert tq % (2 * tk) == 0 and tlen % tq == 0
    kmap = lambda b, h: (b, h, 0, 0, 0)
    omap = lambda b, h: (b, 0, h)
    stat = pltpu.VMEM((1, tq), F32)
    return pl.pallas_call(
        _attn_pipe_body,
        grid=(bsz, nh),
        in_specs=[pl.BlockSpec((None, None, tlen, HEAD_PAD), lambda b, h: (b, h, 0, 0)),
                  pl.BlockSpec((None, None, nkt, tk, HEAD_PAD), kmap),
                  pl.BlockSpec((None, None, nkt, MLA_DV, tk), kmap),
                  pl.BlockSpec((None, tlen, MLA_DV), omap)],
        out_specs=pl.BlockSpec((None, tlen, MLA_DV), omap),
        out_shape=jax.ShapeDtypeStruct(mz.shape, BF16),
        scratch_shapes=[pltpu.VMEM((2, tq, HEAD_PAD), BF16),
                        pltpu.VMEM((tk, tq), F32), pltpu.VMEM((tk, tq), F32), stat, stat,
                        pltpu.VMEM((tk, tq), BF16), pltpu.VMEM((tk, tq), BF16), stat, stat,
                        stat, stat, pltpu.VMEM((MLA_DV, tq), F32)],
        compiler_params=pltpu.CompilerParams(
            dimension_semantics=("arbitrary", "arbitrary"),
            vmem_limit_bytes=VMEM_LIMIT),
        name="mla_attn_pipe",
    )(q, k, vt, mz)


def _out_mem_body(*refs, n_ogla):
    ogla_refs, refs = refs[:n_ogla], refs[n_ogla:]
    (x_ref, omla_ref, mk_ref, mv_ref, wout_ref, gq_ref, wmq_ref, wmo_ref, gfin_ref,
     y_ref, o_scr) = refs
    nb = mk_ref.shape[0]
    tm = x_ref.shape[0]
    rb = tm // nb
    hb = ogla_refs[0].shape[0]
    sb = min(OUT_ROW_BLOCK, tm)
    mb = min(sb, rb)

    def block_stages(r):
        rows = slice(r * sb, (r + 1) * sb)
        part, off = divmod(r * sb, hb)
        loc = {}

        def mix():
            ogla = ogla_refs[part][off:off + sb, :]
            loc["x1"] = (x_ref[rows, :] + _dot(ogla, wout_ref[0:GLA_WIDTH, :])
                         + _dot(omla_ref[rows, :], wout_ref[GLA_WIDTH:GLA_WIDTH + MLA_WIDTH, :]))

        def query():
            h = _rms(loc["x1"], gq_ref[...]).astype(BF16)
            loc["qm"] = (_dot(h, wmq_ref[...]) * (MEM_DH ** -0.5)).astype(BF16)

        def head(hh):
            def run():
                cols = slice(hh * MEM_DH, (hh + 1) * MEM_DH)
                for i in range(sb // mb):
                    bb = (r * sb + i * mb) // rb
                    s = _dot_nt(loc["qm"][i * mb:(i + 1) * mb, cols],
                                mk_ref[bb, :, cols].astype(BF16))
                    p = jnp.exp(s - jnp.max(s, axis=1, keepdims=True))
                    p = p / jnp.sum(p, axis=1, keepdims=True)
                    o_scr[r * sb + i * mb:r * sb + (i + 1) * mb, cols] = _dot(
                        p.astype(BF16), mv_ref[bb, :, cols].astype(BF16)).astype(BF16)
            return run

        def out():
            x2 = loc["x1"] + _dot(o_scr[rows, :], wmo_ref[...])
            y_ref[rows, :] = _rms(x2, gfin_ref[...])

        return [mix, query] + [head(hh) for hh in range(MEM_HEADS)] + [out]

    for stage in _skewed([block_stages(r) for r in range(tm // sb)]):
        stage()


def _out_mem(x, ogla_parts, omla, mk, mv, wts, *, nb):
    g, tlen, _ = x.shape
    hb = ogla_parts[0].shape[2]
    tm = hb * len(ogla_parts)
    nt = tlen // tm
    mmap = (lambda b, t: (b, 0, 0)) if nb == 1 else (lambda b, t: (t, 0, 0))
    row = lambda w: pl.BlockSpec((None, tm, w), lambda b, t: (b, t, 0))
    part = pl.BlockSpec((None, None, hb, GLA_WIDTH), lambda b, t: (b, t, 0, 0))
    mspec = pl.BlockSpec((nb, N_MEM, D_MODEL), mmap)
    sq = pl.BlockSpec((D_MODEL, D_MODEL), lambda b, t: (0, 0))
    vec = pl.BlockSpec((1, D_MODEL), lambda b, t: (0, 0))
    return pl.pallas_call(
        functools.partial(_out_mem_body, n_ogla=len(ogla_parts)),
        grid=(g, nt),
        in_specs=[part] * len(ogla_parts) + [row(D_MODEL), row(MLA_WIDTH), mspec, mspec,
                                              sq, vec, sq, sq, vec],
        out_specs=row(D_MODEL),
        out_shape=jax.ShapeDtypeStruct(x.shape, F32),
        scratch_shapes=[pltpu.VMEM((tm, D_MODEL), BF16)],
        compiler_params=pltpu.CompilerParams(dimension_semantics=("arbitrary", "arbitrary"),
                                             vmem_limit_bytes=VMEM_LIMIT),
        name="out_mem",
    )(*ogla_parts, x, omla, mk, mv, *wts)


def _prep_w_in(w_in):
    splits = np.cumsum([GLA_KW, GLA_KW, GLA_WIDTH, GLA_GATE_RANK, GLA_WIDTH, MLA_Q_RANK,
                        MLA_KV_RANK, MLA_ROPE])
    gq, gk, gv, glr, gz, cq, ckv, kr, mz = jnp.split(w_in, splits.tolist(), axis=1)
    zeros = lambda n: jnp.zeros((D_MODEL, n), w_in.dtype)
    misc = jnp.concatenate([glr, zeros(ROPE_LANE0 - GLA_GATE_RANK), kr,
                            zeros(LANES - ROPE_LANE0 - MLA_ROPE)], axis=1)
    return jnp.concatenate([misc, gq, gk, gv, gz, cq, ckv, mz], axis=1).astype(BF16)


def _pad_heads(w, width):
    kdim = w.shape[0]
    w = w.reshape(kdim, MLA_HEADS, width)
    w = jnp.pad(w, ((0, 0), (0, 0), (0, HEAD_PAD - width)))
    return w.reshape(kdim, MLA_HEADS * HEAD_PAD)


def _rope_tables(pos):
    inv = ROPE_THETA ** (-jnp.arange(HALF_ROPE, dtype=F32) / HALF_ROPE)
    ang = pos.astype(F32)[:, None] * inv[None, :]
    c, s = jnp.cos(ang), jnp.sin(ang)
    n = pos.shape[0]
    cos = jnp.concatenate([jnp.ones((n, ROPE_LANE0), F32), c, c,
                           jnp.zeros((n, LANES - ROPE_LANE0 - MLA_ROPE), F32)], axis=1)
    sin = jnp.concatenate([jnp.zeros((n, ROPE_LANE0), F32), -s, s,
                           jnp.zeros((n, LANES - ROPE_LANE0 - MLA_ROPE), F32)], axis=1)
    return cos, sin


def _chunk_onehot(pos):
    cidx = (np.asarray(pos) // CHUNK) % Q_TILE_CHUNKS
    tab = np.zeros((len(cidx), LANES), np.float32)
    tab[np.arange(len(cidx)), MASK_LANE0 + cidx] = 1.0
    return jnp.asarray(tab)


def _gla_mask():
    r = np.arange(GLA_HEADS * CHUNK)
    same_head = (r[:, None] // CHUNK) == (r[None, :] // CHUNK)
    causal = (r[None, :] % CHUNK) <= (r[:, None] % CHUNK)
    return jnp.asarray((same_head & causal).astype(np.float32))


def kernel(x_prompt, x_sample, mem_prompt, cache_mla_ckv, cache_mla_krope, state_gla, cache_mem_k, cache_mem_v, g_mix, w_in, w_gla_g2, b_gla_g, g_gla_out, g_qa, w_qb, g_kva, w_kvb, w_out, g_mem_q, g_mem_kv, w_mq, w_mk, w_mv, w_mo, g_final):
    bp, tp, _ = x_prompt.shape
    bs, ts, _ = x_sample.shape
    past = cache_mla_ckv.shape[2]
    assert ts == CHUNK and g_mix.shape[0] == 1
    l = 0
    row = lambda g: g.reshape(1, -1)

    w_in_p = _prep_w_in(w_in[l])
    wg2_p = jnp.pad(w_gla_g2[l], ((0, LANES - GLA_GATE_RANK), (0, 0))).astype(BF16)
    wqb_p = _pad_heads(w_qb[l], MLA_NOPE + MLA_ROPE).astype(BF16)
    wkv = w_kvb[l].reshape(MLA_KV_RANK, MLA_HEADS, MLA_NOPE + MLA_DV)
    wk_p = _pad_heads(wkv[:, :, :MLA_NOPE].reshape(MLA_KV_RANK, -1), MLA_NOPE).astype(BF16)
    wvt_p = wkv[:, :, MLA_NOPE:].reshape(MLA_KV_RANK, MLA_WIDTH).T.astype(BF16)
    in_wts = (row(g_mix[l]), w_in_p, wg2_p, row(b_gla_g[l]), row(g_gla_out[l]), row(g_qa[l]),
              wqb_p, row(g_kva[l]), wk_p, wvt_p)
    assert len(in_wts) == N_IN_WTS
    out_wts = (w_out[l].astype(BF16), row(g_mem_q[l]), w_mq[l].astype(BF16),
               w_mo[l].astype(BF16), row(g_final))
    amask = _gla_mask()
    sel = jnp.zeros((MLA_ROPE, HEAD_PAD), BF16).at[
        jnp.arange(MLA_ROPE), ROPE_LANE0 + jnp.arange(MLA_ROPE)].set(1.0)

    tm = ROW_BLOCK
    sample_rows = bs * ts
    bpt = tm // ts

    mk_p, mv_p = _mem_kv(mem_prompt.reshape(bp * N_MEM, D_MODEL), row(g_mem_kv[l]),
                         w_mk[l].astype(BF16), w_mv[l].astype(BF16), tm)
    cos_p, sin_p = _rope_tables(jnp.arange(tp, dtype=jnp.int32))
    s0_p = jnp.zeros((bp, GLA_KW, GLA_DV), F32)
    ogla_a, ogla_b, mz_p, q_p, k_p, vt_p, ckv_p, kr_p, st_p = _in_gla(
        x_prompt, s0_p, cos_p, sin_p, _chunk_onehot(np.arange(tp)), amask, in_wts,
        hb=tm, carry=True)
    omla_p = _attention_pipelined(
        q_p, k_p.reshape(bp, MLA_HEADS, tp // tm, tm, HEAD_PAD), vt_p, mz_p)
    y_prompt = _out_mem(x_prompt, (ogla_a, ogla_b), omla_p, mk_p.reshape(bp, N_MEM, D_MODEL),
                        mv_p.reshape(bp, N_MEM, D_MODEL), out_wts, nb=1)

    xs = x_sample.reshape(1, sample_rows, D_MODEL)
    cos_s, sin_s = _rope_tables(past + jnp.arange(ts, dtype=jnp.int32))
    cos_s, sin_s = jnp.tile(cos_s, (bpt, 1)), jnp.tile(sin_s, (bpt, 1))
    s0_s = state_gla[l].reshape(bs, GLA_KW, GLA_DV)
    ogla_s, mz_s, q_s, _, _, ckv_s, kr_s, st_s = _in_gla(
        xs, s0_s, cos_s, sin_s, jnp.zeros((tm, LANES), F32), amask, in_wts, hb=tm, carry=False)
    omla_s = _latent_attention(
        q_s.reshape(MLA_HEADS, bs, ts, HEAD_PAD), ckv_s.reshape(bs, ts, MLA_KV_RANK),
        kr_s.reshape(bs, ts, MLA_ROPE), cache_mla_ckv[l], cache_mla_krope[l],
        mz_s.reshape(bs, ts, MLA_WIDTH), wk_p, wvt_p, sel)
    omla_s = omla_s.reshape(1, sample_rows, MLA_WIDTH)
    nb_s = 4
    y_sample = _out_mem(xs, (ogla_s.reshape(1, bs // nb_s, nb_s * ts, GLA_WIDTH),), omla_s,
                        cache_mem_k[l].reshape(bs, N_MEM, D_MODEL),
                        cache_mem_v[l].reshape(bs, N_MEM, D_MODEL), out_wts, nb=nb_s)

    mem_shape = (1, bp, N_MEM, MEM_HEADS, MEM_DH)
    st_shape = (GLA_HEADS, GLA_DK, GLA_DV)
    return (y_prompt, y_sample.reshape(bs, ts, D_MODEL),
            ckv_p[None], kr_p[None], st_p.reshape(1, bp, *st_shape),
            mk_p.reshape(mem_shape), mv_p.reshape(mem_shape),
            ckv_s.reshape(1, bs, ts, MLA_KV_RANK), kr_s.reshape(1, bs, ts, MLA_ROPE),
            st_s.reshape(1, bs, *st_shape))
```

```python
import functools

import jax
import jax.numpy as jnp
import numpy as np
from jax import lax
from jax.experimental import pallas as pl
from jax.experimental.pallas import tpu as pltpu

D_MODEL = 1024
CHUNK = 64
EPS = 1e-6
NEG_INF = -1e30

GLA_HEADS = 4
GLA_DK = 64
GLA_DV = 128
GLA_GATE_RANK = 16
GLA_GATE_NORM = 16.0
GLA_KW = GLA_HEADS * GLA_DK
GLA_WIDTH = GLA_HEADS * GLA_DV

MLA_HEADS = 4
MLA_Q_RANK = 256
MLA_KV_RANK = 128
MLA_NOPE = 64
MLA_ROPE = 32
MLA_DV = 128
MLA_WIDTH = MLA_HEADS * MLA_DV
MLA_SCALE = (MLA_NOPE + MLA_ROPE) ** -0.5
ROPE_THETA = 10000.0
HALF_ROPE = MLA_ROPE // 2
Q_PRESCALE = MLA_SCALE * float(np.log2(np.e))
SUBQ = 256
LANES = 128
HEAD_PAD = LANES
MASK_LANE0 = MLA_NOPE + MLA_ROPE
Q_TILE_CHUNKS = HEAD_PAD - MASK_LANE0
ONES_ROWS = 16
ROW_BLOCK = 512
OUT_ROW_BLOCK = 256
N_IN_WTS = 10

MEM_HEADS = 4
MEM_DH = D_MODEL // MEM_HEADS
N_MEM = 256

ROPE_LANE0 = MLA_NOPE

COL_MISC = 0
COL_Q = COL_MISC + LANES
COL_K = COL_Q + GLA_KW
COL_V = COL_K + GLA_KW
COL_Z = COL_V + GLA_WIDTH
COL_CQ = COL_Z + GLA_WIDTH
COL_CKV = COL_CQ + MLA_Q_RANK
COL_MZ = COL_CKV + MLA_KV_RANK
D_IN_PAD = COL_MZ + MLA_WIDTH
COL_MLA = COL_CQ
COL_LA = COL_MLA
GBUF_W = COL_LA + GLA_KW
PROJ_PIECE = 256

VMEM_LIMIT = 52 * 1024 * 1024

BF16 = jnp.bfloat16
F32 = jnp.float32


def _rms(x, g):
    ms = jnp.mean(x * x, axis=-1, keepdims=True)
    return x * lax.rsqrt(ms + EPS) * g


def _dot(a, b):
    return jnp.dot(a, b, preferred_element_type=F32)


def _dot_nt(a, b):
    return lax.dot_general(a, b, (((1,), (1,)), ((), ())), preferred_element_type=F32)


def _dot_tn(a, b):
    return lax.dot_general(a, b, (((0,), (0,)), ((), ())), preferred_element_type=F32)


def _rope_group(x, cos, sin):
    lane = lax.broadcasted_iota(jnp.int32, x.shape, 1)
    partner = jnp.where(lane < ROPE_LANE0 + HALF_ROPE,
                        pltpu.roll(x, LANES - HALF_ROPE, 1),
                        pltpu.roll(x, HALF_ROPE, 1))
    return x * cos + partner * sin


def _mem_kv_body(mem_ref, g_ref, wk_ref, wv_ref, mk_ref, mv_ref):
    m = _rms(mem_ref[...], g_ref[...]).astype(BF16)
    mk_ref[...] = _dot(m, wk_ref[...])
    mv_ref[...] = _dot(m, wv_ref[...])


def _mem_kv(mem2d, g, wk, wv, tm):
    rows = mem2d.shape[0]
    wspec = pl.BlockSpec((D_MODEL, D_MODEL), lambda i: (0, 0))
    ospec = pl.BlockSpec((tm, D_MODEL), lambda i: (i, 0))
    return pl.pallas_call(
        _mem_kv_body,
        grid=(rows // tm,),
        in_specs=[pl.BlockSpec((tm, D_MODEL), lambda i: (i, 0)),
                  pl.BlockSpec((1, D_MODEL), lambda i: (0, 0)), wspec, wspec],
        out_specs=[ospec, ospec],
        out_shape=[jax.ShapeDtypeStruct((rows, D_MODEL), F32)] * 2,
        compiler_params=pltpu.CompilerParams(dimension_semantics=("arbitrary",),
                                             vmem_limit_bytes=VMEM_LIMIT),
        name="mem_kv",
    )(mem2d, g, wk, wv)


class _InGla:
    def __init__(self, tabs, wts, amask_ref):
        self.cos_ref, self.sin_ref, self.kx_ref = tabs
        (self.gmix_ref, self.win_ref, self.wg2_ref, self.bg_ref, self.ggla_ref, self.gqa_ref,
         self.wqb_ref, self.gkva_ref, self.wk_ref, self.wvt_ref) = wts
        self.amask_ref = amask_ref
        ri = lax.broadcasted_iota(jnp.int32, (CHUNK, CHUNK), 0)
        ci = lax.broadcasted_iota(jnp.int32, (CHUNK, CHUNK), 1)
        tril = (ci <= ri).astype(BF16)
        self.tril3 = jnp.concatenate([tril, tril, tril], axis=1)
        klane = lax.broadcasted_iota(jnp.int32, (CHUNK, GLA_KW), 1)
        self.head_of_lane = klane // GLA_DK

    def prep_pieces(self, x_ref, rows, gbuf, proj_ref, outs, vt_dst):
        mz_ref, q_ref, k_ref, ckv_ref, kr_ref = outs
        loc = {}

        def norm():
            loc["h"] = _rms(x_ref[rows, :], self.gmix_ref[...]).astype(BF16)

        def proj(dst, d0, c0, c1):
            def run():
                dst[:, d0:d0 + c1 - c0] = _dot(loc["h"], self.win_ref[:, c0:c1])
            return run

        def queries():
            cos, sin = self.cos_ref[rows, :], self.sin_ref[rows, :]
            cq = proj_ref[:, COL_CQ - COL_MLA:COL_CQ - COL_MLA + MLA_Q_RANK]
            qf = _dot(_rms(cq, self.gqa_ref[...]).astype(BF16), self.wqb_ref[...])
            for hh in range(MLA_HEADS):
                qh = _rope_group(qf[:, hh * HEAD_PAD:(hh + 1) * HEAD_PAD], cos, sin)
                q_ref[hh, rows, :] = (qh * Q_PRESCALE).astype(BF16)

        def keys_values():
            cos, sin = self.cos_ref[rows, :], self.sin_ref[rows, :]
            ckv = _rms(proj_ref[:, COL_CKV - COL_MLA:COL_CKV - COL_MLA + MLA_KV_RANK],
                       self.gkva_ref[...])
            ckv_ref[rows, :] = ckv
            ckv_b = ckv.astype(BF16)
            misc = gbuf[:, COL_MISC:COL_MISC + LANES]
            lane = lax.broadcasted_iota(jnp.int32, misc.shape, 1)
            kr = jnp.where(lane >= ROPE_LANE0, _rope_group(misc, cos, sin), 0.0)
            kr_ref[rows, :] = kr[:, ROPE_LANE0:ROPE_LANE0 + MLA_ROPE]
            kr = kr + self.kx_ref[rows, :]
            kn = _dot(ckv_b, self.wk_ref[...])
            vt = _dot_nt(self.wvt_ref[...], ckv_b)
            for hh in range(MLA_HEADS):
                k_ref[hh, rows, :] = (kn[:, hh * HEAD_PAD:(hh + 1) * HEAD_PAD] + kr).astype(BF16)
                vt_dst(hh, vt[hh * MLA_DV:(hh + 1) * MLA_DV, :].astype(BF16))

        def gates():
            mz = proj_ref[:, COL_MZ - COL_MLA:COL_MZ - COL_MLA + MLA_WIDTH]
            mz_ref[rows, :] = (mz * jax.nn.sigmoid(mz)).astype(BF16)
            misc = gbuf[:, COL_MISC:COL_MISC + LANES]
            z = _dot(misc.astype(BF16), self.wg2_ref[...]) + self.bg_ref[...]
            gbuf[:, COL_LA:COL_LA + GLA_KW] = ((jnp.minimum(z, 0.0)
                                                - jnp.log1p(jnp.exp(-jnp.abs(z))))
                                               * (1.0 / GLA_GATE_NORM))

        cuts = list(range(0, COL_MLA, PROJ_PIECE)) + [COL_MLA]
        to_gbuf = [proj(gbuf, c0, c0, c1) for c0, c1 in zip(cuts[:-1], cuts[1:])]
        cuts = list(range(COL_MLA, D_IN_PAD, PROJ_PIECE)) + [D_IN_PAD]
        to_proj = [proj(proj_ref, c0 - COL_MLA, c0, c1) for c0, c1 in zip(cuts[:-1], cuts[1:])]
        return ([norm] + to_proj + to_gbuf[:1] + [queries, keys_values] + to_gbuf[1:2] + [gates]
                + to_gbuf[2:])

    def chunk_stages(self, c, gbuf, ogla_ref, state):
        rows = slice(c * CHUNK, (c + 1) * CHUNK)
        loc = {}

        def per_head(a):
            return jnp.concatenate(
                [jnp.where(self.head_of_lane == hh, a, 0.0) for hh in range(GLA_HEADS)], axis=0)

        def decay():
            la = gbuf[rows, COL_LA:COL_LA + GLA_KW]
            la_hi = la.astype(BF16)
            r1 = la - la_hi.astype(F32)
            la_mid = r1.astype(BF16)
            la_lo = (r1 - la_mid.astype(F32)).astype(BF16)
            loc["b"] = _dot(self.tril3, jnp.concatenate([la_hi, la_mid, la_lo], axis=0))

        def inner():
            b = loc["b"]
            b_last = b[CHUNK - 1:CHUNK, :]
            q = gbuf[rows, COL_Q:COL_Q + GLA_KW] * (GLA_DK ** -0.5)
            k = gbuf[rows, COL_K:COL_K + GLA_KW]
            loc["sdec"] = jnp.exp(b_last)
            loc["qs"] = per_head(q * jnp.exp(b)).astype(BF16)
            kt = jnp.concatenate([(k * jnp.exp(-b)).astype(BF16)] * GLA_HEADS, axis=0)
            loc["kd2"] = per_head(k * jnp.exp(b_last - b)).astype(BF16)
            loc["a"] = _dot_nt(loc["qs"], kt)

        def outer():
            a = jnp.where(self.amask_ref[...] != 0.0, loc["a"], 0.0).astype(BF16)
            v = gbuf[rows, COL_V:COL_V + GLA_WIDTH]
            vs = jnp.concatenate([v[:, hh * GLA_DV:(hh + 1) * GLA_DV] for hh in range(GLA_HEADS)],
                                 axis=0).astype(BF16)
            st = state["st"]
            loc["o"] = _dot_nt(loc["qs"], st.astype(BF16)) + _dot(a, vs)
            state["st"] = st * loc["sdec"] + _dot_tn(vs, loc["kd2"])

        def emit():
            gz = gbuf[rows, COL_Z:COL_Z + GLA_WIDTH]
            gate = gz * jax.nn.sigmoid(gz)
            on = _rms(loc["o"], self.ggla_ref[...])
            for hh in range(GLA_HEADS):
                cols = slice(hh * GLA_DV, (hh + 1) * GLA_DV)
                ogla_ref[rows, cols] = (on[hh * CHUNK:(hh + 1) * CHUNK, :]
                                        * gate[:, cols]).astype(BF16)

        return [decay, inner, outer, emit]


def _skewed(stage_lists):
    depth = len(stage_lists[0])
    order = []
    for tau in range(len(stage_lists) + depth - 1):
        for s in range(depth):
            c = tau - s
            if 0 <= c < len(stage_lists):
                order.append(stage_lists[c][s])
    return order


def _interleave(main, fill):
    done = 0
    for n, thunk in enumerate(main):
        thunk()
        want = (n + 1) * len(fill) // len(main)
        for piece in fill[done:want]:
            piece()
        done = want


def _in_gla_lag_body(x_ref, s0_ref, cos_ref, sin_ref, kx_ref, amask_ref, *refs):
    wts, refs = refs[:N_IN_WTS], refs[N_IN_WTS:]
    (ogla_a_ref, ogla_b_ref, mz_ref, q_ref, k_ref, vt_ref, ckv_ref, kr_ref, sfin_ref,
     proj_ref, gbuf_a, gbuf_b, st_ref) = refs
    hb = gbuf_a.shape[0]
    t = pl.program_id(1)
    last = pl.num_programs(1) - 1
    parts = _InGla((cos_ref, sin_ref, kx_ref), wts, amask_ref)
    outs = (mz_ref, q_ref, k_ref, ckv_ref, kr_ref)

    @pl.when(t == 0)
    def _():
        st_ref[...] = s0_ref[0].T
        gbuf_b[...] = jnp.zeros(gbuf_b.shape, F32)

    def bracket(gbuf_prev, ogla_ref, half, gbuf_next):
        state = {"st": st_ref[...]}
        rows = slice(half * hb, (half + 1) * hb)
        stages = _skewed([parts.chunk_stages(c, gbuf_prev, ogla_ref, state)
                          for c in range(hb // CHUNK)])
        pieces = parts.prep_pieces(
            x_ref, rows, gbuf_next, proj_ref, outs,
            lambda hh, val: vt_ref.__setitem__((hh, half), val))
        _interleave(stages, pieces)
        st_ref[...] = state["st"]

    bracket(gbuf_b, ogla_b_ref, 0, gbuf_a)

    @pl.when(t < last)
    def _():
        bracket(gbuf_a, ogla_a_ref, 1, gbuf_b)

    @pl.when(t == last)
    def _():
        sfin_ref[0] = st_ref[...].T


def _in_gla_batched_body(x_ref, s0_ref, cos_ref, sin_ref, kx_ref, amask_ref, *refs):
    wts, refs = refs[:N_IN_WTS], refs[N_IN_WTS:]
    (ogla_ref, mz_ref, q_ref, k_ref, vt_ref, ckv_ref, kr_ref, sfin_ref, proj_ref, gbuf) = refs
    tm = x_ref.shape[0]
    parts = _InGla((cos_ref, sin_ref, kx_ref), wts, amask_ref)
    for piece in parts.prep_pieces(
            x_ref, slice(0, tm), gbuf, proj_ref, (mz_ref, q_ref, k_ref, ckv_ref, kr_ref),
            lambda hh, val: vt_ref.__setitem__((hh, 0), val)):
        piece()
    states = [{"st": s0_ref[c].T} for c in range(tm // CHUNK)]
    for stage in _skewed([parts.chunk_stages(c, gbuf, ogla_ref, states[c])
                          for c in range(tm // CHUNK)]):
        stage()
    for c in range(tm // CHUNK):
        sfin_ref[c] = states[c]["st"].T


def _in_gla(x, s0, cos, sin, kx, amask, wts, *, hb, carry):
    g, tlen, _ = x.shape
    tm = 2 * hb if carry else hb
    nt = tlen // tm
    ns = 1 if carry else tm // CHUNK
    clamp = (lambda t: jnp.minimum(t, nt - 1)) if carry else (lambda t: t)
    smap = (lambda b, t: (b, 0, 0)) if carry else (lambda b, t: (t, 0, 0))
    tab_tiles = cos.shape[0] // tm
    tmap = (lambda b, t: (clamp(t), 0)) if tab_tiles > 1 else (lambda b, t: (0, 0))

    def const(a):
        return pl.BlockSpec(a.shape, lambda b, t: (0,) * a.ndim, pipeline_mode=pl.Buffered(1))

    row = lambda w: pl.BlockSpec((None, tm, w), lambda b, t: (b, clamp(t), 0))
    heads = pl.BlockSpec((None, MLA_HEADS, tm, HEAD_PAD), lambda b, t: (b, 0, clamp(t), 0))
    tab = pl.BlockSpec((tm, LANES), tmap)
    in_specs = [row(D_MODEL), pl.BlockSpec((ns, GLA_KW, GLA_DV), smap), tab, tab, tab,
                const(amask)] + [const(w) for w in wts]
    vt_spec = pl.BlockSpec((None, MLA_HEADS, tm // hb, MLA_DV, hb),
                           lambda b, t: (b, 0, clamp(t), 0, 0))
    common_specs = [row(MLA_WIDTH), heads, heads, vt_spec, row(MLA_KV_RANK), row(MLA_ROPE),
                    pl.BlockSpec((ns, GLA_KW, GLA_DV), smap)]
    hshape = jax.ShapeDtypeStruct((g, MLA_HEADS, tlen, HEAD_PAD), BF16)
    common_shapes = [
        jax.ShapeDtypeStruct((g, tlen, MLA_WIDTH), BF16), hshape, hshape,
        jax.ShapeDtypeStruct((g, MLA_HEADS, tlen // hb, MLA_DV, hb), BF16),
        jax.ShapeDtypeStruct((g, tlen, MLA_KV_RANK), F32),
        jax.ShapeDtypeStruct((g, tlen, MLA_ROPE), F32),
        jax.ShapeDtypeStruct(s0.shape, F32),
    ]
    gbuf = pltpu.VMEM((hb, GBUF_W), F32)
    proj = pltpu.VMEM((hb, D_IN_PAD - COL_MLA), F32)
    if carry:
        blk = lambda imap: pl.BlockSpec((None, None, hb, GLA_WIDTH), imap)
        ogla_specs = [blk(lambda b, t: (b, clamp(t), 0, 0)),
                      blk(lambda b, t: (b, jnp.maximum(t - 1, 0), 0, 0))]
        ogla_shapes = [jax.ShapeDtypeStruct((g, nt, hb, GLA_WIDTH), BF16)] * 2
        body, grid = _in_gla_lag_body, (g, nt + 1)
        scratch = [proj, gbuf, gbuf, pltpu.VMEM((GLA_DV, GLA_KW), F32)]
    else:
        ogla_specs = [row(GLA_WIDTH)]
        ogla_shapes = [jax.ShapeDtypeStruct((g, tlen, GLA_WIDTH), BF16)]
        body, grid = _in_gla_batched_body, (g, nt)
        scratch = [proj, gbuf]
    return pl.pallas_call(
        body,
        grid=grid,
        in_specs=in_specs,
        out_specs=ogla_specs + common_specs,
        out_shape=ogla_shapes + common_shapes,
        scratch_shapes=scratch,
        compiler_params=pltpu.CompilerParams(dimension_semantics=("arbitrary", "arbitrary"),
                                             vmem_limit_bytes=VMEM_LIMIT),
        name="in_gla_lagged" if carry else "in_gla_batched",
    )(x, s0, cos, sin, kx, amask, *wts)


def _latent_attn_body(q_ref, ckvn_ref, krn_ref, ckvc_ref, krc_ref, mz_ref, wk_ref, wvt_ref,
                      sel_ref, o_ref, *, past):
    ts = q_ref.shape[1]
    nk = past + ts
    ckv = jnp.concatenate([ckvc_ref[...], ckvn_ref[...]], axis=0).astype(BF16)
    kr = jnp.concatenate([krc_ref[...], krn_ref[...]], axis=0).astype(BF16)
    krp = _dot(kr, sel_ref[...]).astype(BF16)
    keys = jnp.concatenate([ckv, krp], axis=1)
    qcat = jnp.concatenate(
        [jnp.concatenate([_dot_nt(q_ref[hh], wk_ref[:, hh * HEAD_PAD:(hh + 1) * HEAD_PAD])
                          .astype(BF16), q_ref[hh]], axis=1) for hh in range(MLA_HEADS)],
        axis=0)
    s = _dot_nt(keys, qcat)
    kchunk = lax.broadcasted_iota(jnp.int32, (nk, 1), 0) // CHUNK
    qpos = past + lax.broadcasted_iota(jnp.int32, (1, MLA_HEADS * ts), 1) % ts
    s = jnp.where(kchunk <= qpos // CHUNK, s, NEG_INF)
    p = jnp.exp2(s - jnp.max(s, axis=0, keepdims=True)).astype(BF16)
    vals = jnp.concatenate([ckv, jnp.ones((nk, LANES), BF16)], axis=1)
    ol = _dot_tn(p, vals)
    olat = (ol[:, :MLA_KV_RANK] / ol[:, MLA_KV_RANK:]).astype(BF16)
    for hh in range(MLA_HEADS):
        cols = slice(hh * MLA_DV, (hh + 1) * MLA_DV)
        o = _dot_nt(olat[hh * ts:(hh + 1) * ts, :], wvt_ref[cols, :])
        o_ref[:, cols] = (o * mz_ref[:, cols].astype(F32)).astype(BF16)


def _latent_attention(q, ckv_new, kr_new, ckv_cache, kr_cache, mz, wk, wvt, sel):
    bsz, past, _ = ckv_cache.shape
    ts = q.shape[2]
    per_b = lambda r, w: pl.BlockSpec((None, r, w), lambda b: (b, 0, 0))
    const = lambda a: pl.BlockSpec(a.shape, lambda b: (0,) * a.ndim)
    return pl.pallas_call(
        functools.partial(_latent_attn_body, past=past),
        grid=(bsz,),
        in_specs=[pl.BlockSpec((MLA_HEADS, None, ts, HEAD_PAD), lambda b: (0, b, 0, 0)),
                  per_b(ts, MLA_KV_RANK), per_b(ts, MLA_ROPE),
                  per_b(past, MLA_KV_RANK), per_b(past, MLA_ROPE), per_b(ts, MLA_WIDTH),
                  const(wk), const(wvt), const(sel)],
        out_specs=per_b(ts, MLA_WIDTH),
        out_shape=jax.ShapeDtypeStruct(mz.shape, BF16),
        compiler_params=pltpu.CompilerParams(dimension_semantics=("arbitrary",),
                                             vmem_limit_bytes=VMEM_LIMIT),
        name="latent_attn",
    )(q, ckv_new, kr_new, ckv_cache, kr_cache, mz, wk, wvt, sel)


def _attn_pipe_body(q_ref, k_ref, vt_ref, mz_ref, o_ref, q2_ref, s0_ref, s1_ref, c0_ref, c1_ref,
                    p0_ref, p1_ref, a0_ref, a1_ref, m_ref, l_ref, acc_ref):
    tq = s0_ref.shape[1]
    tk = k_ref.shape[1]
    nsub = tq // SUBQ
    ndiag = tq // tk
    nq = q_ref.shape[0] // tq
    all_subs = tuple(range(nsub))
    diag_subs = [tuple(u for u in all_subs if (u + 1) * SUBQ > d * tk) for d in range(ndiag)]
    slots = ((s0_ref, c0_ref, p0_ref, a0_ref), (s1_ref, c1_ref, p1_ref, a1_ref))

    def cols(u):
        return slice(u * SUBQ, (u + 1) * SUBQ)

    def scores(j, slot, subs, diagonal):
        s_ref, c_ref, _, _ = slot
        kt = k_ref[j]
        for u in subs:
            s = _dot_nt(kt, q2_ref[diagonal, cols(u), :])
            s_ref[:, cols(u)] = s
            c_ref[:, cols(u)] = jnp.max(s, axis=0, keepdims=True)

    def softmax(slot, subs):
        s_ref, c_ref, p_ref, a_ref = slot
        for u in subs:
            m_old = m_ref[:, cols(u)]
            m_new = jnp.maximum(m_old, c_ref[:, cols(u)])
            a_ref[:, cols(u)] = jnp.exp2(m_old - m_new)
            p_ref[:, cols(u)] = jnp.exp2((s_ref[:, cols(u)] - m_new).astype(BF16))
            m_ref[:, cols(u)] = m_new

    ones_rows = jnp.ones((ONES_ROWS, tk), BF16)

    def values(j, slot, subs):
        _, _, p_ref, a_ref = slot
        vt1 = jnp.concatenate([vt_ref[j], ones_rows], axis=0)
        for u in subs:
            pv = _dot(vt1, p_ref[:, cols(u)])
            alpha = a_ref[:, cols(u)]
            acc_ref[:, cols(u)] = alpha * acc_ref[:, cols(u)] + pv[:MLA_DV, :]
            l_ref[:, cols(u)] = alpha * l_ref[:, cols(u)] + pv[MLA_DV:MLA_DV + 1, :]

    def start_tile(i):
        q = q_ref[i * tq:(i + 1) * tq, :]
        lane = lax.broadcasted_iota(jnp.int32, q.shape, 1) - MASK_LANE0
        qchunk = lax.broadcasted_iota(jnp.int32, q.shape, 0) // CHUNK
        q2_ref[0] = q
        q2_ref[1] = jnp.where(lane > qchunk, NEG_INF, q.astype(F32)).astype(BF16)
        scores(0, slots[0], all_subs, 1 if i == 0 else 0)

    def pair(u, n_plain):
        scores(2 * u + 1, slots[1], all_subs, 0)
        softmax(slots[0], all_subs)
        values(2 * u - 1, slots[1], all_subs)
        scores(2 * u + 2, slots[0], all_subs, jnp.where(2 * u + 2 == n_plain, 1, 0))
        softmax(slots[1], all_subs)
        values(2 * u, slots[0], all_subs)

    start_tile(0)
    for i in range(nq):
        n_plain = ndiag * i
        m_ref[...] = jnp.full(m_ref.shape, NEG_INF, F32)
        l_ref[...] = jnp.zeros(l_ref.shape, F32)
        acc_ref[...] = jnp.zeros(acc_ref.shape, F32)
        if n_plain > 0:
            scores(1, slots[1], all_subs, 0)
            softmax(slots[0], all_subs)
            scores(2, slots[0], all_subs, 1 if n_plain == 2 else 0)
            softmax(slots[1], all_subs)
            values(0, slots[0], all_subs)
            lax.fori_loop(1, n_plain // 2, lambda u, c: (pair(u, n_plain), c)[1], 0)
        for d in range(ndiag):
            if d + 1 < ndiag:
                scores(n_plain + d + 1, slots[(d + 1) % 2], diag_subs[d + 1], 1)
            softmax(slots[d % 2], diag_subs[d])
            if n_plain + d > 0:
                values(n_plain + d - 1, slots[(d + 1) % 2],
                       all_subs if d == 0 else diag_subs[d - 1])
        values(n_plain + ndiag - 1, slots[(ndiag - 1) % 2], diag_subs[ndiag - 1])
        if i + 1 < nq:
            start_tile(i + 1)
        rows = slice(i * tq, (i + 1) * tq)
        o = (acc_ref[...] / l_ref[...]).T
        o_ref[rows, :] = (o * mz_ref[rows, :].astype(F32)).astype(BF16)


def _attention_pipelined(q, k, vt, mz):
    bsz, nh, tlen, _ = q.shape
    nkt, tk = k.shape[2], k.shape[3]
    tq = Q_TILE_CHUNKS * CHUNK
    assert tq % (2 * tk) == 0 and tlen % tq == 0
    kmap = lambda b, h: (b, h, 0, 0, 0)
    omap = lambda b, h: (b, 0, h)
    stat = pltpu.VMEM((1, tq), F32)
    return pl.pallas_call(
        _attn_pipe_body,
        grid=(bsz, nh),
        in_specs=[pl.BlockSpec((None, None, tlen, HEAD_PAD), lambda b, h: (b, h, 0, 0)),
                  pl.BlockSpec((None, None, nkt, tk, HEAD_PAD), kmap),
                  pl.BlockSpec((None, None, nkt, MLA_DV, tk), kmap),
                  pl.BlockSpec((None, tlen, MLA_DV), omap)],
        out_specs=pl.BlockSpec((None, tlen, MLA_DV), omap),
        out_shape=jax.ShapeDtypeStruct(mz.shape, BF16),
        scratch_shapes=[pltpu.VMEM((2, tq, HEAD_PAD), BF16),
                        pltpu.VMEM((tk, tq), F32), pltpu.VMEM((tk, tq), F32), stat, stat,
                        pltpu.VMEM((tk, tq), BF16), pltpu.VMEM((tk, tq), BF16), stat, stat,
                        stat, stat, pltpu.VMEM((MLA_DV, tq), F32)],
        compiler_params=pltpu.CompilerParams(
            dimension_semantics=("arbitrary", "arbitrary"),
            vmem_limit_bytes=VMEM_LIMIT),
        name="mla_attn_pipe",
    )(q, k, vt, mz)


def _out_mem_body(*refs, n_ogla):
    ogla_refs, refs = refs[:n_ogla], refs[n_ogla:]
    (x_ref, omla_ref, mk_ref, mv_ref, wout_ref, gq_ref, wmq_ref, wmo_ref, gfin_ref,
     y_ref, o_scr) = refs
    nb = mk_ref.shape[0]
    tm = x_ref.shape[0]
    rb = tm // nb
    hb = ogla_refs[0].shape[0]
    sb = min(OUT_ROW_BLOCK, tm)
    mb = min(sb, rb)

    def block_stages(r):
        rows = slice(r * sb, (r + 1) * sb)
        part, off = divmod(r * sb, hb)
        loc = {}

        def mix():
            ogla = ogla_refs[part][off:off + sb, :]
            loc["x1"] = (x_ref[rows, :] + _dot(ogla, wout_ref[0:GLA_WIDTH, :])
                         + _dot(omla_ref[rows, :], wout_ref[GLA_WIDTH:GLA_WIDTH + MLA_WIDTH, :]))

        def query():
            h = _rms(loc["x1"], gq_ref[...]).astype(BF16)
            loc["qm"] = (_dot(h, wmq_ref[...]) * (MEM_DH ** -0.5)).astype(BF16)

        def head(hh):
            def run():
                cols = slice(hh * MEM_DH, (hh + 1) * MEM_DH)
                for i in range(sb // mb):
                    bb = (r * sb + i * mb) // rb
                    s = _dot_nt(loc["qm"][i * mb:(i + 1) * mb, cols],
                                mk_ref[bb, :, cols].astype(BF16))
                    p = jnp.exp(s - jnp.max(s, axis=1, keepdims=True))
                    p = p / jnp.sum(p, axis=1, keepdims=True)
                    o_scr[r * sb + i * mb:r * sb + (i + 1) * mb, cols] = _dot(
                        p.astype(BF16), mv_ref[bb, :, cols].astype(BF16)).astype(BF16)
            return run

        def batch(i):
            def run():
                bb = (r * sb + i * mb) // rb
                qb = loc["qm"][i * mb:(i + 1) * mb, :]
                qs = jnp.concatenate([qb[:, hh * MEM_DH:(hh + 1) * MEM_DH]
                                      for hh in range(MEM_HEADS)], axis=0)
                mk2 = mk_ref[bb].reshape(N_MEM * MEM_HEADS, MEM_DH).astype(BF16)
                mv2 = mv_ref[bb].reshape(N_MEM * MEM_HEADS, MEM_DH).astype(BF16)
                s = _dot_nt(qs, mk2)
                row_head = lax.broadcasted_iota(jnp.int32, s.shape, 0) // mb
                col_head = lax.broadcasted_iota(jnp.int32, s.shape, 1) % MEM_HEADS
                s = jnp.where(row_head == col_head, s, NEG_INF)
                p = jnp.exp(s - jnp.max(s, axis=1, keepdims=True))
                p = p / jnp.sum(p, axis=1, keepdims=True)
                o = _dot(p.astype(BF16), mv2).astype(BF16)
                for hh in range(MEM_HEADS):
                    o_scr[r * sb + i * mb:r * sb + (i + 1) * mb,
                          hh * MEM_DH:(hh + 1) * MEM_DH] = o[hh * mb:(hh + 1) * mb, :]
            return run

        def out():
            x2 = loc["x1"] + _dot(o_scr[rows, :], wmo_ref[...])
            y_ref[rows, :] = _rms(x2, gfin_ref[...])

        if len(mk_ref.shape) == 4:
            attend = [batch(i) for i in range(sb // mb)]
        else:
            attend = [head(hh) for hh in range(MEM_HEADS)]
        return [mix, query] + attend + [out]

    for stage in _skewed([block_stages(r) for r in range(tm // sb)]):
        stage()


def _out_mem(x, ogla_parts, omla, mk, mv, wts, *, nb):
    g, tlen, _ = x.shape
    hb = ogla_parts[0].shape[2]
    tm = hb * len(ogla_parts)
    nt = tlen // tm
    tail = (0,) * (mk.ndim - 1)
    mmap = (lambda b, t: (b,) + tail) if nb == 1 else (lambda b, t: (t,) + tail)
    row = lambda w: pl.BlockSpec((None, tm, w), lambda b, t: (b, t, 0))
    part = pl.BlockSpec((None, None, hb, GLA_WIDTH), lambda b, t: (b, t, 0, 0))
    mspec = pl.BlockSpec((nb,) + mk.shape[1:], mmap)
    sq = pl.BlockSpec((D_MODEL, D_MODEL), lambda b, t: (0, 0))
    vec = pl.BlockSpec((1, D_MODEL), lambda b, t: (0, 0))
    return pl.pallas_call(
        functools.partial(_out_mem_body, n_ogla=len(ogla_parts)),
        grid=(g, nt),
        in_specs=[part] * len(ogla_parts) + [row(D_MODEL), row(MLA_WIDTH), mspec, mspec,
                                              sq, vec, sq, sq, vec],
        out_specs=row(D_MODEL),
        out_shape=jax.ShapeDtypeStruct(x.shape, F32),
        scratch_shapes=[pltpu.VMEM((tm, D_MODEL), BF16)],
        compiler_params=pltpu.CompilerParams(dimension_semantics=("arbitrary", "arbitrary"),
                                             vmem_limit_bytes=VMEM_LIMIT),
        name="out_mem",
    )(*ogla_parts, x, omla, mk, mv, *wts)


def _prep_w_in(w_in):
    splits = np.cumsum([GLA_KW, GLA_KW, GLA_WIDTH, GLA_GATE_RANK, GLA_WIDTH, MLA_Q_RANK,
                        MLA_KV_RANK, MLA_ROPE])
    gq, gk, gv, glr, gz, cq, ckv, kr, mz = jnp.split(w_in, splits.tolist(), axis=1)
    zeros = lambda n: jnp.zeros((D_MODEL, n), w_in.dtype)
    misc = jnp.concatenate([glr, zeros(ROPE_LANE0 - GLA_GATE_RANK), kr,
                            zeros(LANES - ROPE_LANE0 - MLA_ROPE)], axis=1)
    return jnp.concatenate([misc, gq, gk, gv, gz, cq, ckv, mz], axis=1).astype(BF16)


def _pad_heads(w, width):
    kdim = w.shape[0]
    w = w.reshape(kdim, MLA_HEADS, width)
    w = jnp.pad(w, ((0, 0), (0, 0), (0, HEAD_PAD - width)))
    return w.reshape(kdim, MLA_HEADS * HEAD_PAD)


def _rope_tables(pos):
    inv = ROPE_THETA ** (-jnp.arange(HALF_ROPE, dtype=F32) / HALF_ROPE)
    ang = pos.astype(F32)[:, None] * inv[None, :]
    c, s = jnp.cos(ang), jnp.sin(ang)
    n = pos.shape[0]
    cos = jnp.concatenate([jnp.ones((n, ROPE_LANE0), F32), c, c,
                           jnp.zeros((n, LANES - ROPE_LANE0 - MLA_ROPE), F32)], axis=1)
    sin = jnp.concatenate([jnp.zeros((n, ROPE_LANE0), F32), -s, s,
                           jnp.zeros((n, LANES - ROPE_LANE0 - MLA_ROPE), F32)], axis=1)
    return cos, sin


def _chunk_onehot(pos):
    cidx = (np.asarray(pos) // CHUNK) % Q_TILE_CHUNKS
    tab = np.zeros((len(cidx), LANES), np.float32)
    tab[np.arange(len(cidx)), MASK_LANE0 + cidx] = 1.0
    return jnp.asarray(tab)


def _gla_mask():
    r = np.arange(GLA_HEADS * CHUNK)
    same_head = (r[:, None] // CHUNK) == (r[None, :] // CHUNK)
    causal = (r[None, :] % CHUNK) <= (r[:, None] % CHUNK)
    return jnp.asarray((same_head & causal).astype(np.float32))


def kernel(x_prompt, x_sample, mem_prompt, cache_mla_ckv, cache_mla_krope, state_gla, cache_mem_k, cache_mem_v, g_mix, w_in, w_gla_g2, b_gla_g, g_gla_out, g_qa, w_qb, g_kva, w_kvb, w_out, g_mem_q, g_mem_kv, w_mq, w_mk, w_mv, w_mo, g_final):
    bp, tp, _ = x_prompt.shape
    bs, ts, _ = x_sample.shape
    past = cache_mla_ckv.shape[2]
    assert ts == CHUNK and g_mix.shape[0] == 1
    l = 0
    row = lambda g: g.reshape(1, -1)

    w_in_p = _prep_w_in(w_in[l])
    wg2_p = jnp.pad(w_gla_g2[l], ((0, LANES - GLA_GATE_RANK), (0, 0))).astype(BF16)
    wqb_p = _pad_heads(w_qb[l], MLA_NOPE + MLA_ROPE).astype(BF16)
    wkv = w_kvb[l].reshape(MLA_KV_RANK, MLA_HEADS, MLA_NOPE + MLA_DV)
    wk_p = _pad_heads(wkv[:, :, :MLA_NOPE].reshape(MLA_KV_RANK, -1), MLA_NOPE).astype(BF16)
    wvt_p = wkv[:, :, MLA_NOPE:].reshape(MLA_KV_RANK, MLA_WIDTH).T.astype(BF16)
    in_wts = (row(g_mix[l]), w_in_p, wg2_p, row(b_gla_g[l]), row(g_gla_out[l]), row(g_qa[l]),
              wqb_p, row(g_kva[l]), wk_p, wvt_p)
    assert len(in_wts) == N_IN_WTS
    out_wts = (w_out[l].astype(BF16), row(g_mem_q[l]), w_mq[l].astype(BF16),
               w_mo[l].astype(BF16), row(g_final))
    amask = _gla_mask()
    sel = jnp.zeros((MLA_ROPE, HEAD_PAD), BF16).at[
        jnp.arange(MLA_ROPE), ROPE_LANE0 + jnp.arange(MLA_ROPE)].set(1.0)

    tm = ROW_BLOCK
    sample_rows = bs * ts
    bpt = tm // ts

    mk_p, mv_p = _mem_kv(mem_prompt.reshape(bp * N_MEM, D_MODEL), row(g_mem_kv[l]),
                         w_mk[l].astype(BF16), w_mv[l].astype(BF16), tm)
    cos_p, sin_p = _rope_tables(jnp.arange(tp, dtype=jnp.int32))
    s0_p = jnp.zeros((bp, GLA_KW, GLA_DV), F32)
    ogla_a, ogla_b, mz_p, q_p, k_p, vt_p, ckv_p, kr_p, st_p = _in_gla(
        x_prompt, s0_p, cos_p, sin_p, _chunk_onehot(np.arange(tp)), amask, in_wts,
        hb=tm, carry=True)
    omla_p = _attention_pipelined(
        q_p, k_p.reshape(bp, MLA_HEADS, tp // tm, tm, HEAD_PAD), vt_p, mz_p)
    y_prompt = _out_mem(x_prompt, (ogla_a, ogla_b), omla_p, mk_p.reshape(bp, N_MEM, D_MODEL),
                        mv_p.reshape(bp, N_MEM, D_MODEL), out_wts, nb=1)

    xs = x_sample.reshape(1, sample_rows, D_MODEL)
    cos_s, sin_s = _rope_tables(past + jnp.arange(ts, dtype=jnp.int32))
    cos_s, sin_s = jnp.tile(cos_s, (bpt, 1)), jnp.tile(sin_s, (bpt, 1))
    s0_s = state_gla[l].reshape(bs, GLA_KW, GLA_DV)
    ogla_s, mz_s, q_s, _, _, ckv_s, kr_s, st_s = _in_gla(
        xs, s0_s, cos_s, sin_s, jnp.zeros((tm, LANES), F32), amask, in_wts, hb=tm, carry=False)
    omla_s = _latent_attention(
        q_s.reshape(MLA_HEADS, bs, ts, HEAD_PAD), ckv_s.reshape(bs, ts, MLA_KV_RANK),
        kr_s.reshape(bs, ts, MLA_ROPE), cache_mla_ckv[l], cache_mla_krope[l],
        mz_s.reshape(bs, ts, MLA_WIDTH), wk_p, wvt_p, sel)
    omla_s = omla_s.reshape(1, sample_rows, MLA_WIDTH)
    nb_s = 4
    y_sample = _out_mem(xs, (ogla_s.reshape(1, bs // nb_s, nb_s * ts, GLA_WIDTH),), omla_s,
                        cache_mem_k[l], cache_mem_v[l], out_wts, nb=nb_s)

    mem_shape = (1, bp, N_MEM, MEM_HEADS, MEM_DH)
    st_shape = (GLA_HEADS, GLA_DK, GLA_DV)
    return (y_prompt, y_sample.reshape(bs, ts, D_MODEL),
            ckv_p[None], kr_p[None], st_p.reshape(1, bp, *st_shape),
            mk_p.reshape(mem_shape), mv_p.reshape(mem_shape),
            ckv_s.reshape(1, bs, ts, MLA_KV_RANK), kr_s.reshape(1, bs, ts, MLA_ROPE),
            st_s.reshape(1, bs, *st_shape))
```

```python
import functools

import jax
import jax.numpy as jnp
import numpy as np
from jax import lax
from jax.experimental import pallas as pl
from jax.experimental.pallas import tpu as pltpu

D_MODEL = 1024
CHUNK = 64
EPS = 1e-6
NEG_INF = -1e30

GLA_HEADS = 4
GLA_DK = 64
GLA_DV = 128
GLA_GATE_RANK = 16
GLA_GATE_NORM = 16.0
GLA_KW = GLA_HEADS * GLA_DK
GLA_WIDTH = GLA_HEADS * GLA_DV

MLA_HEADS = 4
MLA_Q_RANK = 256
MLA_KV_RANK = 128
MLA_NOPE = 64
MLA_ROPE = 32
MLA_DV = 128
MLA_WIDTH = MLA_HEADS * MLA_DV
MLA_SCALE = (MLA_NOPE + MLA_ROPE) ** -0.5
ROPE_THETA = 10000.0
HALF_ROPE = MLA_ROPE // 2
Q_PRESCALE = MLA_SCALE * float(np.log2(np.e))
SUBQ = 256
LANES = 128
HEAD_PAD = LANES
MASK_LANE0 = MLA_NOPE + MLA_ROPE
Q_TILE_CHUNKS = HEAD_PAD - MASK_LANE0
ONES_ROWS = 16
ROW_BLOCK = 512
OUT_ROW_BLOCK = 256
N_IN_WTS = 10

MEM_HEADS = 4
MEM_DH = D_MODEL // MEM_HEADS
N_MEM = 256

ROPE_LANE0 = MLA_NOPE

COL_MISC = 0
COL_Q = COL_MISC + LANES
COL_K = COL_Q + GLA_KW
COL_V = COL_K + GLA_KW
COL_Z = COL_V + GLA_WIDTH
COL_CQ = COL_Z + GLA_WIDTH
COL_CKV = COL_CQ + MLA_Q_RANK
COL_MZ = COL_CKV + MLA_KV_RANK
D_IN_PAD = COL_MZ + MLA_WIDTH
COL_MLA = COL_CQ
COL_LA = COL_MLA
GBUF_W = COL_LA + GLA_KW
PROJ_PIECE = 256

VMEM_LIMIT = 52 * 1024 * 1024

BF16 = jnp.bfloat16
F32 = jnp.float32


def _rms(x, g):
    ms = jnp.mean(x * x, axis=-1, keepdims=True)
    return x * lax.rsqrt(ms + EPS) * g


def _dot(a, b):
    return jnp.dot(a, b, preferred_element_type=F32)


def _dot_nt(a, b):
    return lax.dot_general(a, b, (((1,), (1,)), ((), ())), preferred_element_type=F32)


def _dot_tn(a, b):
    return lax.dot_general(a, b, (((0,), (0,)), ((), ())), preferred_element_type=F32)


def _rope_group(x, cos, sin):
    lane = lax.broadcasted_iota(jnp.int32, x.shape, 1)
    partner = jnp.where(lane < ROPE_LANE0 + HALF_ROPE,
                        pltpu.roll(x, LANES - HALF_ROPE, 1),
                        pltpu.roll(x, HALF_ROPE, 1))
    return x * cos + partner * sin


def _mem_kv_body(mem_ref, g_ref, wk_ref, wv_ref, mk_ref, mv_ref):
    m = _rms(mem_ref[...], g_ref[...]).astype(BF16)
    mk_ref[...] = _dot(m, wk_ref[...])
    mv_ref[...] = _dot(m, wv_ref[...])


def _mem_kv(mem2d, g, wk, wv, tm):
    rows = mem2d.shape[0]
    wspec = pl.BlockSpec((D_MODEL, D_MODEL), lambda i: (0, 0))
    ospec = pl.BlockSpec((tm, D_MODEL), lambda i: (i, 0))
    return pl.pallas_call(
        _mem_kv_body,
        grid=(rows // tm,),
        in_specs=[pl.BlockSpec((tm, D_MODEL), lambda i: (i, 0)),
                  pl.BlockSpec((1, D_MODEL), lambda i: (0, 0)), wspec, wspec],
        out_specs=[ospec, ospec],
        out_shape=[jax.ShapeDtypeStruct((rows, D_MODEL), F32)] * 2,
        compiler_params=pltpu.CompilerParams(dimension_semantics=("arbitrary",),
                                             vmem_limit_bytes=VMEM_LIMIT),
        name="mem_kv",
    )(mem2d, g, wk, wv)


class _InGla:
    def __init__(self, tabs, wts, amask_ref):
        self.cos_ref, self.sin_ref, self.kx_ref = tabs
        (self.gmix_ref, self.win_ref, self.wg2_ref, self.bg_ref, self.ggla_ref, self.gqa_ref,
         self.wqb_ref, self.gkva_ref, self.wk_ref, self.wvt_ref) = wts
        self.amask_ref = amask_ref
        ri = lax.broadcasted_iota(jnp.int32, (CHUNK, CHUNK), 0)
        ci = lax.broadcasted_iota(jnp.int32, (CHUNK, CHUNK), 1)
        tril = (ci <= ri).astype(BF16)
        self.tril3 = jnp.concatenate([tril, tril, tril], axis=1)
        klane = lax.broadcasted_iota(jnp.int32, (CHUNK, GLA_KW), 1)
        self.head_of_lane = klane // GLA_DK

    def prep_pieces(self, x_ref, rows, gbuf, proj_ref, outs, vt_dst):
        mz_ref, q_ref, k_ref, ckv_ref, kr_ref = outs
        loc = {}

        def norm():
            loc["h"] = _rms(x_ref[rows, :], self.gmix_ref[...]).astype(BF16)

        def proj(dst, d0, c0, c1):
            def run():
                dst[:, d0:d0 + c1 - c0] = _dot(loc["h"], self.win_ref[:, c0:c1])
            return run

        def queries():
            cos, sin = self.cos_ref[rows, :], self.sin_ref[rows, :]
            cq = proj_ref[:, COL_CQ - COL_MLA:COL_CQ - COL_MLA + MLA_Q_RANK]
            qf = _dot(_rms(cq, self.gqa_ref[...]).astype(BF16), self.wqb_ref[...])
            for hh in range(MLA_HEADS):
                qh = _rope_group(qf[:, hh * HEAD_PAD:(hh + 1) * HEAD_PAD], cos, sin)
                q_ref[hh, rows, :] = (qh * Q_PRESCALE).astype(BF16)

        def keys_values():
            cos, sin = self.cos_ref[rows, :], self.sin_ref[rows, :]
            ckv = _rms(proj_ref[:, COL_CKV - COL_MLA:COL_CKV - COL_MLA + MLA_KV_RANK],
                       self.gkva_ref[...])
            ckv_ref[rows, :] = ckv
            ckv_b = ckv.astype(BF16)
            misc = gbuf[:, COL_MISC:COL_MISC + LANES]
            lane = lax.broadcasted_iota(jnp.int32, misc.shape, 1)
            kr = jnp.where(lane >= ROPE_LANE0, _rope_group(misc, cos, sin), 0.0)
            kr_ref[rows, :] = kr[:, ROPE_LANE0:ROPE_LANE0 + MLA_ROPE]
            kr = kr + self.kx_ref[rows, :]
            kn = _dot(ckv_b, self.wk_ref[...])
            vt = _dot_nt(self.wvt_ref[...], ckv_b)
            for hh in range(MLA_HEADS):
                k_ref[hh, rows, :] = (kn[:, hh * HEAD_PAD:(hh + 1) * HEAD_PAD] + kr).astype(BF16)
                vt_dst(hh, vt[hh * MLA_DV:(hh + 1) * MLA_DV, :].astype(BF16))

        def gates():
            mz = proj_ref[:, COL_MZ - COL_MLA:COL_MZ - COL_MLA + MLA_WIDTH]
            mz_ref[rows, :] = (mz * jax.nn.sigmoid(mz)).astype(BF16)
            misc = gbuf[:, COL_MISC:COL_MISC + LANES]
            z = _dot(misc.astype(BF16), self.wg2_ref[...]) + self.bg_ref[...]
            gbuf[:, COL_LA:COL_LA + GLA_KW] = ((jnp.minimum(z, 0.0)
                                                - jnp.log1p(jnp.exp(-jnp.abs(z))))
                                               * (1.0 / GLA_GATE_NORM))

        cuts = list(range(0, COL_MLA, PROJ_PIECE)) + [COL_MLA]
        to_gbuf = [proj(gbuf, c0, c0, c1) for c0, c1 in zip(cuts[:-1], cuts[1:])]
        cuts = list(range(COL_MLA, D_IN_PAD, PROJ_PIECE)) + [D_IN_PAD]
        to_proj = [proj(proj_ref, c0 - COL_MLA, c0, c1) for c0, c1 in zip(cuts[:-1], cuts[1:])]
        return ([norm] + to_proj + to_gbuf[:1] + [queries, keys_values] + to_gbuf[1:2] + [gates]
                + to_gbuf[2:])

    def chunk_stages(self, c, gbuf, ogla_ref, state):
        rows = slice(c * CHUNK, (c + 1) * CHUNK)
        loc = {}

        def per_head(a):
            return jnp.concatenate(
                [jnp.where(self.head_of_lane == hh, a, 0.0) for hh in range(GLA_HEADS)], axis=0)

        def decay():
            la = gbuf[rows, COL_LA:COL_LA + GLA_KW]
            la_hi = la.astype(BF16)
            r1 = la - la_hi.astype(F32)
            la_mid = r1.astype(BF16)
            la_lo = (r1 - la_mid.astype(F32)).astype(BF16)
            loc["b"] = _dot(self.tril3, jnp.concatenate([la_hi, la_mid, la_lo], axis=0))

        def inner():
            b = loc["b"]
            b_last = b[CHUNK - 1:CHUNK, :]
            q = gbuf[rows, COL_Q:COL_Q + GLA_KW] * (GLA_DK ** -0.5)
            k = gbuf[rows, COL_K:COL_K + GLA_KW]
            loc["sdec"] = jnp.exp(b_last)
            loc["qs"] = per_head(q * jnp.exp(b)).astype(BF16)
            kt = jnp.concatenate([(k * jnp.exp(-b)).astype(BF16)] * GLA_HEADS, axis=0)
            loc["kd2"] = per_head(k * jnp.exp(b_last - b)).astype(BF16)
            loc["a"] = _dot_nt(loc["qs"], kt)

        def outer():
            a = jnp.where(self.amask_ref[...] != 0.0, loc["a"], 0.0).astype(BF16)
            v = gbuf[rows, COL_V:COL_V + GLA_WIDTH]
            vs = jnp.concatenate([v[:, hh * GLA_DV:(hh + 1) * GLA_DV] for hh in range(GLA_HEADS)],
                                 axis=0).astype(BF16)
            st = state["st"]
            loc["o"] = _dot_nt(loc["qs"], st.astype(BF16)) + _dot(a, vs)
            state["st"] = st * loc["sdec"] + _dot_tn(vs, loc["kd2"])

        def emit():
            gz = gbuf[rows, COL_Z:COL_Z + GLA_WIDTH]
            gate = gz * jax.nn.sigmoid(gz)
            on = _rms(loc["o"], self.ggla_ref[...])
            for hh in range(GLA_HEADS):
                cols = slice(hh * GLA_DV, (hh + 1) * GLA_DV)
                ogla_ref[rows, cols] = (on[hh * CHUNK:(hh + 1) * CHUNK, :]
                                        * gate[:, cols]).astype(BF16)

        return [decay, inner, outer, emit]


def _skewed(stage_lists):
    depth = len(stage_lists[0])
    order = []
    for tau in range(len(stage_lists) + depth - 1):
        for s in range(depth):
            c = tau - s
            if 0 <= c < len(stage_lists):
                order.append(stage_lists[c][s])
    return order


def _interleave(main, fill):
    done = 0
    for n, thunk in enumerate(main):
        thunk()
        want = (n + 1) * len(fill) // len(main)
        for piece in fill[done:want]:
            piece()
        done = want


def _in_gla_lag_body(x_ref, s0_ref, cos_ref, sin_ref, kx_ref, amask_ref, *refs):
    wts, refs = refs[:N_IN_WTS], refs[N_IN_WTS:]
    (ogla_a_ref, ogla_b_ref, mz_ref, q_ref, k_ref, vt_ref, ckv_ref, kr_ref, sfin_ref,
     proj_ref, gbuf_a, gbuf_b, st_ref) = refs
    hb = gbuf_a.shape[0]
    t = pl.program_id(1)
    last = pl.num_programs(1) - 1
    parts = _InGla((cos_ref, sin_ref, kx_ref), wts, amask_ref)
    outs = (mz_ref, q_ref, k_ref, ckv_ref, kr_ref)

    @pl.when(t == 0)
    def _():
        st_ref[...] = s0_ref[0].T
        gbuf_b[...] = jnp.zeros(gbuf_b.shape, F32)

    def bracket(gbuf_prev, ogla_ref, half, gbuf_next):
        state = {"st": st_ref[...]}
        rows = slice(half * hb, (half + 1) * hb)
        stages = _skewed([parts.chunk_stages(c, gbuf_prev, ogla_ref, state)
                          for c in range(hb // CHUNK)])
        pieces = parts.prep_pieces(
            x_ref, rows, gbuf_next, proj_ref, outs,
            lambda hh, val: vt_ref.__setitem__((hh, half), val))
        _interleave(stages, pieces)
        st_ref[...] = state["st"]

    @pl.when(t < last)
    def _():
        bracket(gbuf_b, ogla_b_ref, 0, gbuf_a)

    @pl.when(t < last)
    def _():
        bracket(gbuf_a, ogla_a_ref, 1, gbuf_b)

    @pl.when(t == last)
    def _():
        state = {"st": st_ref[...]}
        for stage in _skewed([parts.chunk_stages(c, gbuf_b, ogla_b_ref, state)
                              for c in range(hb // CHUNK)]):
            stage()
        sfin_ref[0] = state["st"].T


def _in_gla_batched_body(x_ref, s0_ref, cos_ref, sin_ref, kx_ref, amask_ref, *refs):
    wts, refs = refs[:N_IN_WTS], refs[N_IN_WTS:]
    (ogla_ref, mz_ref, q_ref, k_ref, vt_ref, ckv_ref, kr_ref, sfin_ref, proj_ref, gbuf) = refs
    tm = x_ref.shape[0]
    parts = _InGla((cos_ref, sin_ref, kx_ref), wts, amask_ref)
    for piece in parts.prep_pieces(
            x_ref, slice(0, tm), gbuf, proj_ref, (mz_ref, q_ref, k_ref, ckv_ref, kr_ref),
            lambda hh, val: vt_ref.__setitem__((hh, 0), val)):
        piece()
    states = [{"st": s0_ref[c].T} for c in range(tm // CHUNK)]
    for stage in _skewed([parts.chunk_stages(c, gbuf, ogla_ref, states[c])
                          for c in range(tm // CHUNK)]):
        stage()
    for c in range(tm // CHUNK):
        sfin_ref[c] = states[c]["st"].T


def _in_gla(x, s0, cos, sin, kx, amask, wts, *, hb, carry):
    g, tlen, _ = x.shape
    tm = 2 * hb if carry else hb
    nt = tlen // tm
    ns = 1 if carry else tm // CHUNK
    clamp = (lambda t: jnp.minimum(t, nt - 1)) if carry else (lambda t: t)
    smap = (lambda b, t: (b, 0, 0)) if carry else (lambda b, t: (t, 0, 0))
    tab_tiles = cos.shape[0] // tm
    tmap = (lambda b, t: (clamp(t), 0)) if tab_tiles > 1 else (lambda b, t: (0, 0))

    def const(a):
        return pl.BlockSpec(a.shape, lambda b, t: (0,) * a.ndim, pipeline_mode=pl.Buffered(1))

    row = lambda w: pl.BlockSpec((None, tm, w), lambda b, t: (b, clamp(t), 0))
    heads = pl.BlockSpec((None, MLA_HEADS, tm, HEAD_PAD), lambda b, t: (b, 0, clamp(t), 0))
    tab = pl.BlockSpec((tm, LANES), tmap)
    in_specs = [row(D_MODEL), pl.BlockSpec((ns, GLA_KW, GLA_DV), smap), tab, tab, tab,
                const(amask)] + [const(w) for w in wts]
    vt_spec = pl.BlockSpec((None, MLA_HEADS, tm // hb, MLA_DV, hb),
                           lambda b, t: (b, 0, clamp(t), 0, 0))
    common_specs = [row(MLA_WIDTH), heads, heads, vt_spec, row(MLA_KV_RANK), row(MLA_ROPE),
                    pl.BlockSpec((ns, GLA_KW, GLA_DV), smap)]
    hshape = jax.ShapeDtypeStruct((g, MLA_HEADS, tlen, HEAD_PAD), BF16)
    common_shapes = [
        jax.ShapeDtypeStruct((g, tlen, MLA_WIDTH), BF16), hshape, hshape,
        jax.ShapeDtypeStruct((g, MLA_HEADS, tlen // hb, MLA_DV, hb), BF16),
        jax.ShapeDtypeStruct((g, tlen, MLA_KV_RANK), F32),
        jax.ShapeDtypeStruct((g, tlen, MLA_ROPE), F32),
        jax.ShapeDtypeStruct(s0.shape, F32),
    ]
    gbuf = pltpu.VMEM((hb, GBUF_W), F32)
    proj = pltpu.VMEM((hb, D_IN_PAD - COL_MLA), F32)
    if carry:
        blk = lambda imap: pl.BlockSpec((None, None, hb, GLA_WIDTH), imap)
        ogla_specs = [blk(lambda b, t: (b, clamp(t), 0, 0)),
                      blk(lambda b, t: (b, jnp.maximum(t - 1, 0), 0, 0))]
        ogla_shapes = [jax.ShapeDtypeStruct((g, nt, hb, GLA_WIDTH), BF16)] * 2
        body, grid = _in_gla_lag_body, (g, nt + 1)
        scratch = [proj, gbuf, gbuf, pltpu.VMEM((GLA_DV, GLA_KW), F32)]
    else:
        ogla_specs = [row(GLA_WIDTH)]
        ogla_shapes = [jax.ShapeDtypeStruct((g, tlen, GLA_WIDTH), BF16)]
        body, grid = _in_gla_batched_body, (g, nt)
        scratch = [proj, gbuf]
    return pl.pallas_call(
        body,
        grid=grid,
        in_specs=in_specs,
        out_specs=ogla_specs + common_specs,
        out_shape=ogla_shapes + common_shapes,
        scratch_shapes=scratch,
        compiler_params=pltpu.CompilerParams(dimension_semantics=("arbitrary", "arbitrary"),
                                             vmem_limit_bytes=VMEM_LIMIT),
        name="in_gla_lagged" if carry else "in_gla_batched",
    )(x, s0, cos, sin, kx, amask, *wts)


def _latent_attn_body(q_ref, ckvn_ref, krn_ref, ckvc_ref, krc_ref, mz_ref, wk_ref, wvt_ref,
                      sel_ref, o_ref, *, past):
    ts = q_ref.shape[1]
    nk = past + ts
    ckv = jnp.concatenate([ckvc_ref[...], ckvn_ref[...]], axis=0).astype(BF16)
    kr = jnp.concatenate([krc_ref[...], krn_ref[...]], axis=0).astype(BF16)
    krp = _dot(kr, sel_ref[...]).astype(BF16)
    keys = jnp.concatenate([ckv, krp], axis=1)
    qcat = jnp.concatenate(
        [jnp.concatenate([_dot_nt(q_ref[hh], wk_ref[:, hh * HEAD_PAD:(hh + 1) * HEAD_PAD])
                          .astype(BF16), q_ref[hh]], axis=1) for hh in range(MLA_HEADS)],
        axis=0)
    s = _dot_nt(keys, qcat)
    kchunk = lax.broadcasted_iota(jnp.int32, (nk, 1), 0) // CHUNK
    qpos = past + lax.broadcasted_iota(jnp.int32, (1, MLA_HEADS * ts), 1) % ts
    s = jnp.where(kchunk <= qpos // CHUNK, s, NEG_INF)
    p = jnp.exp2(s - jnp.max(s, axis=0, keepdims=True)).astype(BF16)
    vals = jnp.concatenate([ckv, jnp.ones((nk, LANES), BF16)], axis=1)
    ol = _dot_tn(p, vals)
    olat = (ol[:, :MLA_KV_RANK] / ol[:, MLA_KV_RANK:]).astype(BF16)
    for hh in range(MLA_HEADS):
        cols = slice(hh * MLA_DV, (hh + 1) * MLA_DV)
        o = _dot_nt(olat[hh * ts:(hh + 1) * ts, :], wvt_ref[cols, :])
        o_ref[:, cols] = (o * mz_ref[:, cols].astype(F32)).astype(BF16)


def _latent_attention(q, ckv_new, kr_new, ckv_cache, kr_cache, mz, wk, wvt, sel):
    bsz, past, _ = ckv_cache.shape
    ts = q.shape[2]
    per_b = lambda r, w: pl.BlockSpec((None, r, w), lambda b: (b, 0, 0))
    const = lambda a: pl.BlockSpec(a.shape, lambda b: (0,) * a.ndim)
    return pl.pallas_call(
        functools.partial(_latent_attn_body, past=past),
        grid=(bsz,),
        in_specs=[pl.BlockSpec((MLA_HEADS, None, ts, HEAD_PAD), lambda b: (0, b, 0, 0)),
                  per_b(ts, MLA_KV_RANK), per_b(ts, MLA_ROPE),
                  per_b(past, MLA_KV_RANK), per_b(past, MLA_ROPE), per_b(ts, MLA_WIDTH),
                  const(wk), const(wvt), const(sel)],
        out_specs=per_b(ts, MLA_WIDTH),
        out_shape=jax.ShapeDtypeStruct(mz.shape, BF16),
        compiler_params=pltpu.CompilerParams(dimension_semantics=("arbitrary",),
                                             vmem_limit_bytes=VMEM_LIMIT),
        name="latent_attn",
    )(q, ckv_new, kr_new, ckv_cache, kr_cache, mz, wk, wvt, sel)


def _attn_pipe_body(q_ref, k_ref, vt_ref, mz_ref, o_ref, q2_ref, s0_ref, s1_ref, c0_ref, c1_ref,
                    p0_ref, p1_ref, a0_ref, a1_ref, m_ref, l_ref, acc_ref):
    tq = s0_ref.shape[1]
    tk = k_ref.shape[1]
    nsub = tq // SUBQ
    ndiag = tq // tk
    nq = q_ref.shape[0] // tq
    all_subs = tuple(range(nsub))
    diag_subs = [tuple(u for u in all_subs if (u + 1) * SUBQ > d * tk) for d in range(ndiag)]
    slots = ((s0_ref, c0_ref, p0_ref, a0_ref), (s1_ref, c1_ref, p1_ref, a1_ref))

    def cols(u):
        return slice(u * SUBQ, (u + 1) * SUBQ)

    def scores(j, slot, subs, diagonal):
        s_ref, c_ref, _, _ = slot
        kt = k_ref[j]
        for u in subs:
            s = _dot_nt(kt, q2_ref[diagonal, cols(u), :])
            s_ref[:, cols(u)] = s
            c_ref[:, cols(u)] = jnp.max(s, axis=0, keepdims=True)

    def softmax(slot, subs):
        s_ref, c_ref, p_ref, a_ref = slot
        for u in subs:
            m_old = m_ref[:, cols(u)]
            m_new = jnp.maximum(m_old, c_ref[:, cols(u)])
            a_ref[:, cols(u)] = jnp.exp2(m_old - m_new)
            p_ref[:, cols(u)] = jnp.exp2((s_ref[:, cols(u)] - m_new).astype(BF16))
            m_ref[:, cols(u)] = m_new

    ones_rows = jnp.ones((ONES_ROWS, tk), BF16)

    def values(j, slot, subs):
        _, _, p_ref, a_ref = slot
        vt1 = jnp.concatenate([vt_ref[j], ones_rows], axis=0)
        for u in subs:
            pv = _dot(vt1, p_ref[:, cols(u)])
            alpha = a_ref[:, cols(u)]
            acc_ref[:, cols(u)] = alpha * acc_ref[:, cols(u)] + pv[:MLA_DV, :]
            l_ref[:, cols(u)] = alpha * l_ref[:, cols(u)] + pv[MLA_DV:MLA_DV + 1, :]

    def start_tile(i):
        q = q_ref[i * tq:(i + 1) * tq, :]
        lane = lax.broadcasted_iota(jnp.int32, q.shape, 1) - MASK_LANE0
        qchunk = lax.broadcasted_iota(jnp.int32, q.shape, 0) // CHUNK
        q2_ref[0] = q
        q2_ref[1] = jnp.where(lane > qchunk, NEG_INF, q.astype(F32)).astype(BF16)
        scores(0, slots[0], all_subs, 1 if i == 0 else 0)

    def pair(u, n_plain):
        scores(2 * u + 1, slots[1], all_subs, 0)
        softmax(slots[0], all_subs)
        values(2 * u - 1, slots[1], all_subs)
        scores(2 * u + 2, slots[0], all_subs, jnp.where(2 * u + 2 == n_plain, 1, 0))
        softmax(slots[1], all_subs)
        values(2 * u, slots[0], all_subs)

    start_tile(0)
    for i in range(nq):
        n_plain = ndiag * i
        m_ref[...] = jnp.full(m_ref.shape, NEG_INF, F32)
        l_ref[...] = jnp.zeros(l_ref.shape, F32)
        acc_ref[...] = jnp.zeros(acc_ref.shape, F32)
        if n_plain > 0:
            scores(1, slots[1], all_subs, 0)
            softmax(slots[0], all_subs)
            scores(2, slots[0], all_subs, 1 if n_plain == 2 else 0)
            softmax(slots[1], all_subs)
            values(0, slots[0], all_subs)
            lax.fori_loop(1, n_plain // 2, lambda u, c: (pair(u, n_plain), c)[1], 0)
        for d in range(ndiag):
            if d + 1 < ndiag:
                scores(n_plain + d + 1, slots[(d + 1) % 2], diag_subs[d + 1], 1)
            softmax(slots[d % 2], diag_subs[d])
            if n_plain + d > 0:
                values(n_plain + d - 1, slots[(d + 1) % 2],
                       all_subs if d == 0 else diag_subs[d - 1])
        values(n_plain + ndiag - 1, slots[(ndiag - 1) % 2], diag_subs[ndiag - 1])
        if i + 1 < nq:
            start_tile(i + 1)
        rows = slice(i * tq, (i + 1) * tq)
        o = (acc_ref[...] / l_ref[...]).T
        o_ref[rows, :] = (o * mz_ref[rows, :].astype(F32)).astype(BF16)


def _attention_pipelined(q, k, vt, mz):
    bsz, nh, tlen, _ = q.shape
    nkt, tk = k.shape[2], k.shape[3]
    tq = Q_TILE_CHUNKS * CHUNK
    assert tq % (2 * tk) == 0 and tlen % tq == 0
    kmap = lambda b, h: (b, h, 0, 0, 0)
    omap = lambda b, h: (b, 0, h)
    stat = pltpu.VMEM((1, tq), F32)
    return pl.pallas_call(
        _attn_pipe_body,
        grid=(bsz, nh),
        in_specs=[pl.BlockSpec((None, None, tlen, HEAD_PAD), lambda b, h: (b, h, 0, 0)),
                  pl.BlockSpec((None, None, nkt, tk, HEAD_PAD), kmap),
                  pl.BlockSpec((None, None, nkt, MLA_DV, tk), kmap),
                  pl.BlockSpec((None, tlen, MLA_DV), omap)],
        out_specs=pl.BlockSpec((None, tlen, MLA_DV), omap),
        out_shape=jax.ShapeDtypeStruct(mz.shape, BF16),
        scratch_shapes=[pltpu.VMEM((2, tq, HEAD_PAD), BF16),
                        pltpu.VMEM((tk, tq), F32), pltpu.VMEM((tk, tq), F32), stat, stat,
                        pltpu.VMEM((tk, tq), BF16), pltpu.VMEM((tk, tq), BF16), stat, stat,
                        stat, stat, pltpu.VMEM((MLA_DV, tq), F32)],
        compiler_params=pltpu.CompilerParams(
            dimension_semantics=("arbitrary", "arbitrary"),
            vmem_limit_bytes=VMEM_LIMIT),
        name="mla_attn_pipe",
    )(q, k, vt, mz)


def _out_mem_body(*refs, n_ogla):
    ogla_refs, refs = refs[:n_ogla], refs[n_ogla:]
    (x_ref, omla_ref, mk_ref, mv_ref, wout_ref, gq_ref, wmq_ref, wmo_ref, gfin_ref,
     y_ref, o_scr) = refs
    nb = mk_ref.shape[0]
    tm = x_ref.shape[0]
    rb = tm // nb
    hb = ogla_refs[0].shape[0]
    sb = min(OUT_ROW_BLOCK, tm)
    mb = min(sb, rb)

    def block_stages(r):
        rows = slice(r * sb, (r + 1) * sb)
        part, off = divmod(r * sb, hb)
        loc = {}

        def mix():
            ogla = ogla_refs[part][off:off + sb, :]
            loc["x1"] = (x_ref[rows, :] + _dot(ogla, wout_ref[0:GLA_WIDTH, :])
                         + _dot(omla_ref[rows, :], wout_ref[GLA_WIDTH:GLA_WIDTH + MLA_WIDTH, :]))

        def query():
            h = _rms(loc["x1"], gq_ref[...]).astype(BF16)
            loc["qm"] = (_dot(h, wmq_ref[...]) * (MEM_DH ** -0.5)).astype(BF16)

        def head(hh):
            def run():
                cols = slice(hh * MEM_DH, (hh + 1) * MEM_DH)
                for i in range(sb // mb):
                    bb = (r * sb + i * mb) // rb
                    s = _dot_nt(loc["qm"][i * mb:(i + 1) * mb, cols],
                                mk_ref[bb, :, cols].astype(BF16))
                    p = jnp.exp(s - jnp.max(s, axis=1, keepdims=True))
                    p = p / jnp.sum(p, axis=1, keepdims=True)
                    o_scr[r * sb + i * mb:r * sb + (i + 1) * mb, cols] = _dot(
                        p.astype(BF16), mv_ref[bb, :, cols].astype(BF16)).astype(BF16)
            return run

        def batch(i):
            def run():
                bb = (r * sb + i * mb) // rb
                qb = loc["qm"][i * mb:(i + 1) * mb, :]
                qs = jnp.concatenate([qb[:, hh * MEM_DH:(hh + 1) * MEM_DH]
                                      for hh in range(MEM_HEADS)], axis=0)
                mk2 = mk_ref[bb].reshape(N_MEM * MEM_HEADS, MEM_DH).astype(BF16)
                mv2 = mv_ref[bb].reshape(N_MEM * MEM_HEADS, MEM_DH).astype(BF16)
                s = _dot_nt(qs, mk2)
                row_head = lax.broadcasted_iota(jnp.int32, s.shape, 0) // mb
                col_head = lax.broadcasted_iota(jnp.int32, s.shape, 1) % MEM_HEADS
                s = jnp.where(row_head == col_head, s, NEG_INF)
                p = jnp.exp(s - jnp.max(s, axis=1, keepdims=True))
                p = p / jnp.sum(p, axis=1, keepdims=True)
                o = _dot(p.astype(BF16), mv2).astype(BF16)
                for hh in range(MEM_HEADS):
                    o_scr[r * sb + i * mb:r * sb + (i + 1) * mb,
                          hh * MEM_DH:(hh + 1) * MEM_DH] = o[hh * mb:(hh + 1) * mb, :]
            return run

        def out():
            x2 = loc["x1"] + _dot(o_scr[rows, :], wmo_ref[...])
            y_ref[rows, :] = _rms(x2, gfin_ref[...])

        if len(mk_ref.shape) == 4:
            attend = [batch(i) for i in range(sb // mb)]
        else:
            attend = [head(hh) for hh in range(MEM_HEADS)]
        return [mix, query] + attend + [out]

    for stage in _skewed([block_stages(r) for r in range(tm // sb)]):
        stage()


def _out_mem(x, ogla_parts, omla, mk, mv, wts, *, nb):
    g, tlen, _ = x.shape
    hb = ogla_parts[0].shape[2]
    tm = hb * len(ogla_parts)
    nt = tlen // tm
    tail = (0,) * (mk.ndim - 1)
    mmap = (lambda b, t: (b,) + tail) if nb == 1 else (lambda b, t: (t,) + tail)
    row = lambda w: pl.BlockSpec((None, tm, w), lambda b, t: (b, t, 0))
    part = pl.BlockSpec((None, None, hb, GLA_WIDTH), lambda b, t: (b, t, 0, 0))
    mspec = pl.BlockSpec((nb,) + mk.shape[1:], mmap)
    sq = pl.BlockSpec((D_MODEL, D_MODEL), lambda b, t: (0, 0))
    vec = pl.BlockSpec((1, D_MODEL), lambda b, t: (0, 0))
    return pl.pallas_call(
        functools.partial(_out_mem_body, n_ogla=len(ogla_parts)),
        grid=(g, nt),
        in_specs=[part] * len(ogla_parts) + [row(D_MODEL), row(MLA_WIDTH), mspec, mspec,
                                              sq, vec, sq, sq, vec],
        out_specs=row(D_MODEL),
        out_shape=jax.ShapeDtypeStruct(x.shape, F32),
        scratch_shapes=[pltpu.VMEM((tm, D_MODEL), BF16)],
        compiler_params=pltpu.CompilerParams(dimension_semantics=("arbitrary", "arbitrary"),
                                             vmem_limit_bytes=VMEM_LIMIT),
        name="out_mem",
    )(*ogla_parts, x, omla, mk, mv, *wts)


def _prep_w_in(w_in):
    splits = np.cumsum([GLA_KW, GLA_KW, GLA_WIDTH, GLA_GATE_RANK, GLA_WIDTH, MLA_Q_RANK,
                        MLA_KV_RANK, MLA_ROPE])
    gq, gk, gv, glr, gz, cq, ckv, kr, mz = jnp.split(w_in, splits.tolist(), axis=1)
    zeros = lambda n: jnp.zeros((D_MODEL, n), w_in.dtype)
    misc = jnp.concatenate([glr, zeros(ROPE_LANE0 - GLA_GATE_RANK), kr,
                            zeros(LANES - ROPE_LANE0 - MLA_ROPE)], axis=1)
    return jnp.concatenate([misc, gq, gk, gv, gz, cq, ckv, mz], axis=1).astype(BF16)


def _pad_heads(w, width):
    kdim = w.shape[0]
    w = w.reshape(kdim, MLA_HEADS, width)
    w = jnp.pad(w, ((0, 0), (0, 0), (0, HEAD_PAD - width)))
    return w.reshape(kdim, MLA_HEADS * HEAD_PAD)


def _rope_tables(pos):
    inv = ROPE_THETA ** (-jnp.arange(HALF_ROPE, dtype=F32) / HALF_ROPE)
    ang = pos.astype(F32)[:, None] * inv[None, :]
    c, s = jnp.cos(ang), jnp.sin(ang)
    n = pos.shape[0]
    cos = jnp.concatenate([jnp.ones((n, ROPE_LANE0), F32), c, c,
                           jnp.zeros((n, LANES - ROPE_LANE0 - MLA_ROPE), F32)], axis=1)
    sin = jnp.concatenate([jnp.zeros((n, ROPE_LANE0), F32), -s, s,
                           jnp.zeros((n, LANES - ROPE_LANE0 - MLA_ROPE), F32)], axis=1)
    return cos, sin


def _chunk_onehot(pos):
    cidx = (np.asarray(pos) // CHUNK) % Q_TILE_CHUNKS
    tab = np.zeros((len(cidx), LANES), np.float32)
    tab[np.arange(len(cidx)), MASK_LANE0 + cidx] = 1.0
    return jnp.asarray(tab)


def _gla_mask():
    r = np.arange(GLA_HEADS * CHUNK)
    same_head = (r[:, None] // CHUNK) == (r[None, :] // CHUNK)
    causal = (r[None, :] % CHUNK) <= (r[:, None] % CHUNK)
    return jnp.asarray((same_head & causal).astype(np.float32))


def kernel(x_prompt, x_sample, mem_prompt, cache_mla_ckv, cache_mla_krope, state_gla, cache_mem_k, cache_mem_v, g_mix, w_in, w_gla_g2, b_gla_g, g_gla_out, g_qa, w_qb, g_kva, w_kvb, w_out, g_mem_q, g_mem_kv, w_mq, w_mk, w_mv, w_mo, g_final):
    bp, tp, _ = x_prompt.shape
    bs, ts, _ = x_sample.shape
    past = cache_mla_ckv.shape[2]
    assert ts == CHUNK and g_mix.shape[0] == 1
    l = 0
    row = lambda g: g.reshape(1, -1)

    w_in_p = _prep_w_in(w_in[l])
    wg2_p = jnp.pad(w_gla_g2[l], ((0, LANES - GLA_GATE_RANK), (0, 0))).astype(BF16)
    wqb_p = _pad_heads(w_qb[l], MLA_NOPE + MLA_ROPE).astype(BF16)
    wkv = w_kvb[l].reshape(MLA_KV_RANK, MLA_HEADS, MLA_NOPE + MLA_DV)
    wk_p = _pad_heads(wkv[:, :, :MLA_NOPE].reshape(MLA_KV_RANK, -1), MLA_NOPE).astype(BF16)
    wvt_p = wkv[:, :, MLA_NOPE:].reshape(MLA_KV_RANK, MLA_WIDTH).T.astype(BF16)
    in_wts = (row(g_mix[l]), w_in_p, wg2_p, row(b_gla_g[l]), row(g_gla_out[l]), row(g_qa[l]),
              wqb_p, row(g_kva[l]), wk_p, wvt_p)
    assert len(in_wts) == N_IN_WTS
    out_wts = (w_out[l].astype(BF16), row(g_mem_q[l]), w_mq[l].astype(BF16),
               w_mo[l].astype(BF16), row(g_final))
    amask = _gla_mask()
    sel = jnp.zeros((MLA_ROPE, HEAD_PAD), BF16).at[
        jnp.arange(MLA_ROPE), ROPE_LANE0 + jnp.arange(MLA_ROPE)].set(1.0)

    tm = ROW_BLOCK
    sample_rows = bs * ts
    bpt = tm // ts

    mk_p, mv_p = _mem_kv(mem_prompt.reshape(bp * N_MEM, D_MODEL), row(g_mem_kv[l]),
                         w_mk[l].astype(BF16), w_mv[l].astype(BF16), tm)
    cos_p, sin_p = _rope_tables(jnp.arange(tp, dtype=jnp.int32))
    s0_p = jnp.zeros((bp, GLA_KW, GLA_DV), F32)
    ogla_a, ogla_b, mz_p, q_p, k_p, vt_p, ckv_p, kr_p, st_p = _in_gla(
        x_prompt, s0_p, cos_p, sin_p, _chunk_onehot(np.arange(tp)), amask, in_wts,
        hb=tm, carry=True)
    omla_p = _attention_pipelined(
        q_p, k_p.reshape(bp, MLA_HEADS, tp // tm, tm, HEAD_PAD), vt_p, mz_p)
    y_prompt = _out_mem(x_prompt, (ogla_a, ogla_b), omla_p, mk_p.reshape(bp, N_MEM, D_MODEL),
                        mv_p.reshape(bp, N_MEM, D_MODEL), out_wts, nb=1)

    xs = x_sample.reshape(1, sample_rows, D_MODEL)
    cos_s, sin_s = _rope_tables(past + jnp.arange(ts, dtype=jnp.int32))
    cos_s, sin_s = jnp.tile(cos_s, (bpt, 1)), jnp.tile(sin_s, (bpt, 1))
    s0_s = state_gla[l].reshape(bs, GLA_KW, GLA_DV)
    ogla_s, mz_s, q_s, _, _, ckv_s, kr_s, st_s = _in_gla(
        xs, s0_s, cos_s, sin_s, jnp.zeros((tm, LANES), F32), amask, in_wts, hb=tm, carry=False)
    omla_s = _latent_attention(
        q_s.reshape(MLA_HEADS, bs, ts, HEAD_PAD), ckv_s.reshape(bs, ts, MLA_KV_RANK),
        kr_s.reshape(bs, ts, MLA_ROPE), cache_mla_ckv[l], cache_mla_krope[l],
        mz_s.reshape(bs, ts, MLA_WIDTH), wk_p, wvt_p, sel)
    omla_s = omla_s.reshape(1, sample_rows, MLA_WIDTH)
    nb_s = 4
    y_sample = _out_mem(xs, (ogla_s.reshape(1, bs // nb_s, nb_s * ts, GLA_WIDTH),), omla_s,
                        cache_mem_k[l], cache_mem_v[l], out_wts, nb=nb_s)

    mem_shape = (1, bp, N_MEM, MEM_HEADS, MEM_DH)
    st_shape = (GLA_HEADS, GLA_DK, GLA_DV)
    return (y_prompt, y_sample.reshape(bs, ts, D_MODEL),
            ckv_p[None], kr_p[None], st_p.reshape(1, bp, *st_shape),
            mk_p.reshape(mem_shape), mv_p.reshape(mem_shape),
            ckv_s.reshape(1, bs, ts, MLA_KV_RANK), kr_s.reshape(1, bs, ts, MLA_ROPE),
            st_s.reshape(1, bs, *st_shape))
```

```python
import functools

import jax
import jax.numpy as jnp
import numpy as np
from jax import lax
from jax.experimental import pallas as pl
from jax.experimental.pallas import tpu as pltpu

D_MODEL = 1024
CHUNK = 64
EPS = 1e-6
NEG_INF = -1e30

GLA_HEADS = 4
GLA_DK = 64
GLA_DV = 128
GLA_GATE_RANK = 16
GLA_GATE_NORM = 16.0
GLA_KW = GLA_HEADS * GLA_DK
GLA_WIDTH = GLA_HEADS * GLA_DV

MLA_HEADS = 4
MLA_Q_RANK = 256
MLA_KV_RANK = 128
MLA_NOPE = 64
MLA_ROPE = 32
MLA_DV = 128
MLA_WIDTH = MLA_HEADS * MLA_DV
MLA_SCALE = (MLA_NOPE + MLA_ROPE) ** -0.5
ROPE_THETA = 10000.0
HALF_ROPE = MLA_ROPE // 2
Q_PRESCALE = MLA_SCALE * float(np.log2(np.e))
SUBQ = 256
LANES = 128
HEAD_PAD = LANES
MASK_LANE0 = MLA_NOPE + MLA_ROPE
Q_TILE_CHUNKS = HEAD_PAD - MASK_LANE0
ONES_ROWS = 16
ROW_BLOCK = 512
OUT_ROW_BLOCK = 256
N_IN_WTS = 10

MEM_HEADS = 4
MEM_DH = D_MODEL // MEM_HEADS
N_MEM = 256

ROPE_LANE0 = MLA_NOPE

COL_MISC = 0
COL_Q = COL_MISC + LANES
COL_K = COL_Q + GLA_KW
COL_V = COL_K + GLA_KW
COL_Z = COL_V + GLA_WIDTH
COL_CQ = COL_Z + GLA_WIDTH
COL_CKV = COL_CQ + MLA_Q_RANK
COL_MZ = COL_CKV + MLA_KV_RANK
D_IN_PAD = COL_MZ + MLA_WIDTH
COL_MLA = COL_CQ
COL_LA = COL_MLA
GBUF_W = COL_LA + GLA_KW
PROJ_PIECE = 256

VMEM_LIMIT = 52 * 1024 * 1024

BF16 = jnp.bfloat16
F32 = jnp.float32


def _rms(x, g):
    ms = jnp.mean(x * x, axis=-1, keepdims=True)
    return x * lax.rsqrt(ms + EPS) * g


def _dot(a, b):
    return jnp.dot(a, b, preferred_element_type=F32)


def _dot_nt(a, b):
    return lax.dot_general(a, b, (((1,), (1,)), ((), ())), preferred_element_type=F32)


def _dot_tn(a, b):
    return lax.dot_general(a, b, (((0,), (0,)), ((), ())), preferred_element_type=F32)


def _rope_group(x, cos, sin):
    lane = lax.broadcasted_iota(jnp.int32, x.shape, 1)
    partner = jnp.where(lane < ROPE_LANE0 + HALF_ROPE,
                        pltpu.roll(x, LANES - HALF_ROPE, 1),
                        pltpu.roll(x, HALF_ROPE, 1))
    return x * cos + partner * sin


def _mem_kv_body(mem_ref, g_ref, wk_ref, wv_ref, mk_ref, mv_ref):
    m = _rms(mem_ref[...], g_ref[...]).astype(BF16)
    mk_ref[...] = _dot(m, wk_ref[...])
    mv_ref[...] = _dot(m, wv_ref[...])


def _mem_kv(mem2d, g, wk, wv, tm):
    rows = mem2d.shape[0]
    wspec = pl.BlockSpec((D_MODEL, D_MODEL), lambda i: (0, 0))
    ospec = pl.BlockSpec((tm, D_MODEL), lambda i: (i, 0))
    return pl.pallas_call(
        _mem_kv_body,
        grid=(rows // tm,),
        in_specs=[pl.BlockSpec((tm, D_MODEL), lambda i: (i, 0)),
                  pl.BlockSpec((1, D_MODEL), lambda i: (0, 0)), wspec, wspec],
        out_specs=[ospec, ospec],
        out_shape=[jax.ShapeDtypeStruct((rows, D_MODEL), F32)] * 2,
        compiler_params=pltpu.CompilerParams(dimension_semantics=("arbitrary",),
                                             vmem_limit_bytes=VMEM_LIMIT),
        name="mem_kv",
    )(mem2d, g, wk, wv)


class _InGla:
    def __init__(self, tabs, wts, amask_ref):
        self.cos_ref, self.sin_ref, self.kx_ref = tabs
        (self.gmix_ref, self.win_ref, self.wg2_ref, self.bg_ref, self.ggla_ref, self.gqa_ref,
         self.wqb_ref, self.gkva_ref, self.wk_ref, self.wvt_ref) = wts
        self.amask_ref = amask_ref
        ri = lax.broadcasted_iota(jnp.int32, (CHUNK, CHUNK), 0)
        ci = lax.broadcasted_iota(jnp.int32, (CHUNK, CHUNK), 1)
        tril = (ci <= ri).astype(BF16)
        self.tril3 = jnp.concatenate([tril, tril, tril], axis=1)
        klane = lax.broadcasted_iota(jnp.int32, (CHUNK, GLA_KW), 1)
        self.head_of_lane = klane // GLA_DK

    def prep_pieces(self, x_ref, rows, gbuf, proj_ref, outs, vt_dst):
        mz_ref, q_ref, k_ref, ckv_ref, kr_ref = outs
        loc = {}

        def norm():
            loc["h"] = _rms(x_ref[rows, :], self.gmix_ref[...]).astype(BF16)

        def proj(dst, d0, c0, c1):
            def run():
                dst[:, d0:d0 + c1 - c0] = _dot(loc["h"], self.win_ref[:, c0:c1])
            return run

        def queries():
            cos, sin = self.cos_ref[rows, :], self.sin_ref[rows, :]
            cq = proj_ref[:, COL_CQ - COL_MLA:COL_CQ - COL_MLA + MLA_Q_RANK]
            qf = _dot(_rms(cq, self.gqa_ref[...]).astype(BF16), self.wqb_ref[...])
            for hh in range(MLA_HEADS):
                qh = _rope_group(qf[:, hh * HEAD_PAD:(hh + 1) * HEAD_PAD], cos, sin)
                q_ref[hh, rows, :] = (qh * Q_PRESCALE).astype(BF16)

        def keys_values():
            cos, sin = self.cos_ref[rows, :], self.sin_ref[rows, :]
            ckv = _rms(proj_ref[:, COL_CKV - COL_MLA:COL_CKV - COL_MLA + MLA_KV_RANK],
                       self.gkva_ref[...])
            ckv_ref[rows, :] = ckv
            ckv_b = ckv.astype(BF16)
            misc = gbuf[:, COL_MISC:COL_MISC + LANES]
            lane = lax.broadcasted_iota(jnp.int32, misc.shape, 1)
            kr = jnp.where(lane >= ROPE_LANE0, _rope_group(misc, cos, sin), 0.0)
            kr_ref[rows, :] = kr[:, ROPE_LANE0:ROPE_LANE0 + MLA_ROPE]
            kr = kr + self.kx_ref[rows, :]
            kn = _dot(ckv_b, self.wk_ref[...])
            vt = _dot_nt(self.wvt_ref[...], ckv_b)
            for hh in range(MLA_HEADS):
                k_ref[hh, rows, :] = (kn[:, hh * HEAD_PAD:(hh + 1) * HEAD_PAD] + kr).astype(BF16)
                vt_dst(hh, vt[hh * MLA_DV:(hh + 1) * MLA_DV, :].astype(BF16))

        def gates():
            mz = proj_ref[:, COL_MZ - COL_MLA:COL_MZ - COL_MLA + MLA_WIDTH]
            mz_ref[rows, :] = (mz * jax.nn.sigmoid(mz)).astype(BF16)
            misc = gbuf[:, COL_MISC:COL_MISC + LANES]
            z = _dot(misc.astype(BF16), self.wg2_ref[...]) + self.bg_ref[...]
            gbuf[:, COL_LA:COL_LA + GLA_KW] = ((jnp.minimum(z, 0.0)
                                                - jnp.log1p(jnp.exp(-jnp.abs(z))))
                                               * (1.0 / GLA_GATE_NORM))

        cuts = list(range(0, COL_MLA, PROJ_PIECE)) + [COL_MLA]
        to_gbuf = [proj(gbuf, c0, c0, c1) for c0, c1 in zip(cuts[:-1], cuts[1:])]
        cuts = list(range(COL_MLA, D_IN_PAD, PROJ_PIECE)) + [D_IN_PAD]
        to_proj = [proj(proj_ref, c0 - COL_MLA, c0, c1) for c0, c1 in zip(cuts[:-1], cuts[1:])]
        return ([norm] + to_proj + to_gbuf[:1] + [queries, keys_values] + to_gbuf[1:2] + [gates]
                + to_gbuf[2:])

    def chunk_stages(self, c, gbuf, ogla_ref, state):
        rows = slice(c * CHUNK, (c + 1) * CHUNK)
        loc = {}

        def per_head(a):
            return jnp.concatenate(
                [jnp.where(self.head_of_lane == hh, a, 0.0) for hh in range(GLA_HEADS)], axis=0)

        def decay():
            la = gbuf[rows, COL_LA:COL_LA + GLA_KW]
            la_hi = la.astype(BF16)
            r1 = la - la_hi.astype(F32)
            la_mid = r1.astype(BF16)
            la_lo = (r1 - la_mid.astype(F32)).astype(BF16)
            loc["b"] = _dot(self.tril3, jnp.concatenate([la_hi, la_mid, la_lo], axis=0))

        def inner():
            b = loc["b"]
            b_last = b[CHUNK - 1:CHUNK, :]
            q = gbuf[rows, COL_Q:COL_Q + GLA_KW] * (GLA_DK ** -0.5)
            k = gbuf[rows, COL_K:COL_K + GLA_KW]
            loc["sdec"] = jnp.exp(b_last)
            loc["qs"] = per_head(q * jnp.exp(b)).astype(BF16)
            kt = jnp.concatenate([(k * jnp.exp(-b)).astype(BF16)] * GLA_HEADS, axis=0)
            loc["kd2"] = per_head(k * jnp.exp(b_last - b)).astype(BF16)
            loc["a"] = _dot_nt(loc["qs"], kt)

        def outer():
            a = jnp.where(self.amask_ref[...] != 0.0, loc["a"], 0.0).astype(BF16)
            v = gbuf[rows, COL_V:COL_V + GLA_WIDTH]
            vs = jnp.concatenate([v[:, hh * GLA_DV:(hh + 1) * GLA_DV] for hh in range(GLA_HEADS)],
                                 axis=0).astype(BF16)
            st = state["st"]
            loc["o"] = _dot_nt(loc["qs"], st.astype(BF16)) + _dot(a, vs)
            state["st"] = st * loc["sdec"] + _dot_tn(vs, loc["kd2"])

        def emit():
            gz = gbuf[rows, COL_Z:COL_Z + GLA_WIDTH]
            gate = gz * jax.nn.sigmoid(gz)
            on = _rms(loc["o"], self.ggla_ref[...])
            for hh in range(GLA_HEADS):
                cols = slice(hh * GLA_DV, (hh + 1) * GLA_DV)
                ogla_ref[rows, cols] = (on[hh * CHUNK:(hh + 1) * CHUNK, :]
                                        * gate[:, cols]).astype(BF16)

        return [decay, inner, outer, emit]


def _skewed(stage_lists):
    depth = len(stage_lists[0])
    order = []
    for tau in range(len(stage_lists) + depth - 1):
        for s in range(depth):
            c = tau - s
            if 0 <= c < len(stage_lists):
                order.append(stage_lists[c][s])
    return order


def _interleave(main, fill):
    done = 0
    for n, thunk in enumerate(main):
        thunk()
        want = (n + 1) * len(fill) // len(main)
        for piece in fill[done:want]:
            piece()
        done = want


def _in_gla_lag_body(x_ref, s0_ref, cos_ref, sin_ref, kx_ref, amask_ref, *refs):
    wts, refs = refs[:N_IN_WTS], refs[N_IN_WTS:]
    (ogla_a_ref, ogla_b_ref, mz_ref, q_ref, k_ref, vt_ref, ckv_ref, kr_ref, sfin_ref,
     proj_ref, gbuf_a, gbuf_b, st_ref) = refs
    hb = gbuf_a.shape[0]
    t = pl.program_id(1)
    last = pl.num_programs(1) - 1
    parts = _InGla((cos_ref, sin_ref, kx_ref), wts, amask_ref)
    outs = (mz_ref, q_ref, k_ref, ckv_ref, kr_ref)

    @pl.when(t == 0)
    def _():
        st_ref[...] = s0_ref[0].T
        gbuf_b[...] = jnp.zeros(gbuf_b.shape, F32)

    def bracket(gbuf_prev, ogla_ref, half, gbuf_next):
        state = {"st": st_ref[...]}
        rows = slice(half * hb, (half + 1) * hb)
        stages = _skewed([parts.chunk_stages(c, gbuf_prev, ogla_ref, state)
                          for c in range(hb // CHUNK)])
        pieces = parts.prep_pieces(
            x_ref, rows, gbuf_next, proj_ref, outs,
            lambda hh, val: vt_ref.__setitem__((hh, half), val))
        _interleave(stages, pieces)
        st_ref[...] = state["st"]

    @pl.when(t < last)
    def _():
        bracket(gbuf_b, ogla_b_ref, 0, gbuf_a)

    @pl.when(t < last)
    def _():
        bracket(gbuf_a, ogla_a_ref, 1, gbuf_b)

    @pl.when(t == last)
    def _():
        state = {"st": st_ref[...]}
        for stage in _skewed([parts.chunk_stages(c, gbuf_b, ogla_b_ref, state)
                              for c in range(hb // CHUNK)]):
            stage()
        sfin_ref[0] = state["st"].T


def _in_gla_batched_body(x_ref, s0_ref, cos_ref, sin_ref, kx_ref, amask_ref, *refs):
    wts, refs = refs[:N_IN_WTS], refs[N_IN_WTS:]
    (ogla_ref, mz_ref, q_ref, k_ref, vt_ref, ckv_ref, kr_ref, sfin_ref, proj_ref, gbuf) = refs
    tm = x_ref.shape[0]
    parts = _InGla((cos_ref, sin_ref, kx_ref), wts, amask_ref)
    for piece in parts.prep_pieces(
            x_ref, slice(0, tm), gbuf, proj_ref, (mz_ref, q_ref, k_ref, ckv_ref, kr_ref),
            lambda hh, val: vt_ref.__setitem__((hh, 0), val)):
        piece()
    states = [{"st": s0_ref[c].T} for c in range(tm // CHUNK)]
    for stage in _skewed([parts.chunk_stages(c, gbuf, ogla_ref, states[c])
                          for c in range(tm // CHUNK)]):
        stage()
    for c in range(tm // CHUNK):
        sfin_ref[c] = states[c]["st"].T


def _in_gla(x, s0, cos, sin, kx, amask, wts, *, hb, carry):
    g, tlen, _ = x.shape
    tm = 2 * hb if carry else hb
    nt = tlen // tm
    ns = 1 if carry else tm // CHUNK
    clamp = (lambda t: jnp.minimum(t, nt - 1)) if carry else (lambda t: t)
    smap = (lambda b, t: (b, 0, 0)) if carry else (lambda b, t: (t, 0, 0))
    tab_tiles = cos.shape[0] // tm
    tmap = (lambda b, t: (clamp(t), 0)) if tab_tiles > 1 else (lambda b, t: (0, 0))

    def const(a):
        return pl.BlockSpec(a.shape, lambda b, t: (0,) * a.ndim, pipeline_mode=pl.Buffered(1))

    row = lambda w: pl.BlockSpec((None, tm, w), lambda b, t: (b, clamp(t), 0))
    heads = pl.BlockSpec((None, MLA_HEADS, tm, HEAD_PAD), lambda b, t: (b, 0, clamp(t), 0))
    tab = pl.BlockSpec((tm, LANES), tmap)
    in_specs = [row(D_MODEL), pl.BlockSpec((ns, GLA_KW, GLA_DV), smap), tab, tab, tab,
                const(amask)] + [const(w) for w in wts]
    vt_spec = pl.BlockSpec((None, MLA_HEADS, tm // hb, MLA_DV, hb),
                           lambda b, t: (b, 0, clamp(t), 0, 0))
    common_specs = [row(MLA_WIDTH), heads, heads, vt_spec, row(MLA_KV_RANK), row(MLA_ROPE),
                    pl.BlockSpec((ns, GLA_KW, GLA_DV), smap)]
    hshape = jax.ShapeDtypeStruct((g, MLA_HEADS, tlen, HEAD_PAD), BF16)
    common_shapes = [
        jax.ShapeDtypeStruct((g, tlen, MLA_WIDTH), BF16), hshape, hshape,
        jax.ShapeDtypeStruct((g, MLA_HEADS, tlen // hb, MLA_DV, hb), BF16),
        jax.ShapeDtypeStruct((g, tlen, MLA_KV_RANK), F32),
        jax.ShapeDtypeStruct((g, tlen, MLA_ROPE), F32),
        jax.ShapeDtypeStruct(s0.shape, F32),
    ]
    gbuf = pltpu.VMEM((hb, GBUF_W), F32)
    proj = pltpu.VMEM((hb, D_IN_PAD - COL_MLA), F32)
    if carry:
        blk = lambda imap: pl.BlockSpec((None, None, hb, GLA_WIDTH), imap)
        ogla_specs = [blk(lambda b, t: (b, clamp(t), 0, 0)),
                      blk(lambda b, t: (b, jnp.maximum(t - 1, 0), 0, 0))]
        ogla_shapes = [jax.ShapeDtypeStruct((g, nt, hb, GLA_WIDTH), BF16)] * 2
        body, grid = _in_gla_lag_body, (g, nt + 1)
        scratch = [proj, gbuf, gbuf, pltpu.VMEM((GLA_DV, GLA_KW), F32)]
    else:
        ogla_specs = [row(GLA_WIDTH)]
        ogla_shapes = [jax.ShapeDtypeStruct((g, tlen, GLA_WIDTH), BF16)]
        body, grid = _in_gla_batched_body, (g, nt)
        scratch = [proj, gbuf]
    return pl.pallas_call(
        body,
        grid=grid,
        in_specs=in_specs,
        out_specs=ogla_specs + common_specs,
        out_shape=ogla_shapes + common_shapes,
        scratch_shapes=scratch,
        compiler_params=pltpu.CompilerParams(dimension_semantics=("arbitrary", "arbitrary"),
                                             vmem_limit_bytes=VMEM_LIMIT),
        name="in_gla_lagged" if carry else "in_gla_batched",
    )(x, s0, cos, sin, kx, amask, *wts)


def _latent_attn_body(q_ref, ckvn_ref, krn_ref, ckvc_ref, krc_ref, mz_ref, wk_ref, wvt_ref,
                      sel_ref, o_ref, *, past):
    ts = q_ref.shape[1]
    nk = past + ts
    ckv = jnp.concatenate([ckvc_ref[...], ckvn_ref[...]], axis=0).astype(BF16)
    kr = jnp.concatenate([krc_ref[...], krn_ref[...]], axis=0).astype(BF16)
    krp = _dot(kr, sel_ref[...]).astype(BF16)
    keys = jnp.concatenate([ckv, krp], axis=1)
    qcat = jnp.concatenate(
        [jnp.concatenate([_dot_nt(q_ref[hh], wk_ref[:, hh * HEAD_PAD:(hh + 1) * HEAD_PAD])
                          .astype(BF16), q_ref[hh]], axis=1) for hh in range(MLA_HEADS)],
        axis=0)
    s = _dot_nt(keys, qcat)
    kchunk = lax.broadcasted_iota(jnp.int32, (nk, 1), 0) // CHUNK
    qpos = past + lax.broadcasted_iota(jnp.int32, (1, MLA_HEADS * ts), 1) % ts
    s = jnp.where(kchunk <= qpos // CHUNK, s, NEG_INF)
    p = jnp.exp2(s - jnp.max(s, axis=0, keepdims=True)).astype(BF16)
    vals = jnp.concatenate([ckv, jnp.ones((nk, LANES), BF16)], axis=1)
    ol = _dot_tn(p, vals)
    olat = (ol[:, :MLA_KV_RANK] / ol[:, MLA_KV_RANK:]).astype(BF16)
    for hh in range(MLA_HEADS):
        cols = slice(hh * MLA_DV, (hh + 1) * MLA_DV)
        o = _dot_nt(olat[hh * ts:(hh + 1) * ts, :], wvt_ref[cols, :])
        o_ref[:, cols] = (o * mz_ref[:, cols].astype(F32)).astype(BF16)


def _latent_attention(q, ckv_new, kr_new, ckv_cache, kr_cache, mz, wk, wvt, sel):
    bsz, past, _ = ckv_cache.shape
    ts = q.shape[2]
    per_b = lambda r, w: pl.BlockSpec((None, r, w), lambda b: (b, 0, 0))
    const = lambda a: pl.BlockSpec(a.shape, lambda b: (0,) * a.ndim)
    return pl.pallas_call(
        functools.partial(_latent_attn_body, past=past),
        grid=(bsz,),
        in_specs=[pl.BlockSpec((MLA_HEADS, None, ts, HEAD_PAD), lambda b: (0, b, 0, 0)),
                  per_b(ts, MLA_KV_RANK), per_b(ts, MLA_ROPE),
                  per_b(past, MLA_KV_RANK), per_b(past, MLA_ROPE), per_b(ts, MLA_WIDTH),
                  const(wk), const(wvt), const(sel)],
        out_specs=per_b(ts, MLA_WIDTH),
        out_shape=jax.ShapeDtypeStruct(mz.shape, BF16),
        compiler_params=pltpu.CompilerParams(dimension_semantics=("arbitrary",),
                                             vmem_limit_bytes=VMEM_LIMIT),
        name="latent_attn",
    )(q, ckv_new, kr_new, ckv_cache, kr_cache, mz, wk, wvt, sel)


def _attn_pipe_body(q_ref, k_ref, vt_ref, mz_ref, o_ref, q2_ref, s0_ref, s1_ref, c0_ref, c1_ref,
                    p0_ref, p1_ref, a0_ref, a1_ref, b0_ref, b1_ref, m_ref, l_ref, acc_ref):
    tq = s0_ref.shape[1]
    tk = k_ref.shape[1]
    nsub = tq // SUBQ
    ndiag = tq // tk
    nq = q_ref.shape[0] // tq
    all_subs = tuple(range(nsub))
    diag_subs = [tuple(u for u in all_subs if (u + 1) * SUBQ > d * tk) for d in range(ndiag)]
    slots = ((s0_ref, c0_ref, p0_ref, a0_ref), (s1_ref, c1_ref, p1_ref, a1_ref))

    def cols(u):
        return slice(u * SUBQ, (u + 1) * SUBQ)

    def scores(j, slot, subs, diagonal):
        s_ref, c_ref, _, _ = slot
        kt = k_ref[j]
        for u in subs:
            s = _dot_nt(kt, q2_ref[diagonal, cols(u), :])
            s_ref[:, cols(u)] = s
            c_ref[:, cols(u)] = jnp.max(s, axis=0, keepdims=True)

    def softmax(slot, subs):
        s_ref, c_ref, p_ref, a_ref = slot
        b_ref = b0_ref if slot is slots[0] else b1_ref
        for u in subs:
            m_old = m_ref[:, cols(u)]
            m_new = jnp.maximum(m_old, c_ref[:, cols(u)])
            a_ref[:, cols(u)] = jnp.exp2(m_old - m_new)
            p = jnp.exp2((s_ref[:, cols(u)] - m_new).astype(BF16))
            p_ref[:, cols(u)] = p
            b_ref[:, cols(u)] = jnp.sum(p.astype(F32), axis=0, keepdims=True)
            m_ref[:, cols(u)] = m_new

    def values(j, slot, subs):
        _, _, p_ref, a_ref = slot
        b_ref = b0_ref if slot is slots[0] else b1_ref
        vt = vt_ref[j]
        for u in subs:
            alpha = a_ref[:, cols(u)]
            acc_ref[:, cols(u)] = alpha * acc_ref[:, cols(u)] + _dot(vt, p_ref[:, cols(u)])
            l_ref[:, cols(u)] = alpha * l_ref[:, cols(u)] + b_ref[:, cols(u)]

    def start_tile(i):
        q = q_ref[i * tq:(i + 1) * tq, :]
        lane = lax.broadcasted_iota(jnp.int32, q.shape, 1) - MASK_LANE0
        qchunk = lax.broadcasted_iota(jnp.int32, q.shape, 0) // CHUNK
        q2_ref[0] = q
        q2_ref[1] = jnp.where(lane > qchunk, NEG_INF, q.astype(F32)).astype(BF16)
        scores(0, slots[0], all_subs, 1 if i == 0 else 0)

    def pair(u, n_plain):
        scores(2 * u + 1, slots[1], all_subs, 0)
        softmax(slots[0], all_subs)
        values(2 * u - 1, slots[1], all_subs)
        scores(2 * u + 2, slots[0], all_subs, jnp.where(2 * u + 2 == n_plain, 1, 0))
        softmax(slots[1], all_subs)
        values(2 * u, slots[0], all_subs)

    start_tile(0)
    for i in range(nq):
        n_plain = ndiag * i
        m_ref[...] = jnp.full(m_ref.shape, NEG_INF, F32)
        l_ref[...] = jnp.zeros(l_ref.shape, F32)
        acc_ref[...] = jnp.zeros(acc_ref.shape, F32)
        if n_plain > 0:
            scores(1, slots[1], all_subs, 0)
            softmax(slots[0], all_subs)
            scores(2, slots[0], all_subs, 1 if n_plain == 2 else 0)
            softmax(slots[1], all_subs)
            values(0, slots[0], all_subs)
            lax.fori_loop(1, n_plain // 2, lambda u, c: (pair(u, n_plain), c)[1], 0)
        for d in range(ndiag):
            if d + 1 < ndiag:
                scores(n_plain + d + 1, slots[(d + 1) % 2], diag_subs[d + 1], 1)
            softmax(slots[d % 2], diag_subs[d])
            if n_plain + d > 0:
                values(n_plain + d - 1, slots[(d + 1) % 2],
                       all_subs if d == 0 else diag_subs[d - 1])
        values(n_plain + ndiag - 1, slots[(ndiag - 1) % 2], diag_subs[ndiag - 1])
        if i + 1 < nq:
            start_tile(i + 1)
        rows = slice(i * tq, (i + 1) * tq)
        o = (acc_ref[...] / l_ref[...]).T
        o_ref[rows, :] = (o * mz_ref[rows, :].astype(F32)).astype(BF16)


def _attention_pipelined(q, k, vt, mz):
    bsz, nh, tlen, _ = q.shape
    nkt, tk = k.shape[2], k.shape[3]
    tq = Q_TILE_CHUNKS * CHUNK
    assert tq % (2 * tk) == 0 and tlen % tq == 0
    kmap = lambda b, h: (b, h, 0, 0, 0)
    omap = lambda b, h: (b, 0, h)
    stat = pltpu.VMEM((1, tq), F32)
    return pl.pallas_call(
        _attn_pipe_body,
        grid=(bsz, nh),
        in_specs=[pl.BlockSpec((None, None, tlen, HEAD_PAD), lambda b, h: (b, h, 0, 0)),
                  pl.BlockSpec((None, None, nkt, tk, HEAD_PAD), kmap),
                  pl.BlockSpec((None, None, nkt, MLA_DV, tk), kmap),
                  pl.BlockSpec((None, tlen, MLA_DV), omap)],
        out_specs=pl.BlockSpec((None, tlen, MLA_DV), omap),
        out_shape=jax.ShapeDtypeStruct(mz.shape, BF16),
        scratch_shapes=[pltpu.VMEM((2, tq, HEAD_PAD), BF16),
                        pltpu.VMEM((tk, tq), F32), pltpu.VMEM((tk, tq), F32), stat, stat,
                        pltpu.VMEM((tk, tq), BF16), pltpu.VMEM((tk, tq), BF16), stat, stat,
                        stat, stat, stat, stat, pltpu.VMEM((MLA_DV, tq), F32)],
        compiler_params=pltpu.CompilerParams(
            dimension_semantics=("arbitrary", "arbitrary"),
            vmem_limit_bytes=VMEM_LIMIT),
        name="mla_attn_pipe",
    )(q, k, vt, mz)


def _out_mem_body(*refs, n_ogla):
    ogla_refs, refs = refs[:n_ogla], refs[n_ogla:]
    (x_ref, omla_ref, mk_ref, mv_ref, wout_ref, gq_ref, wmq_ref, wmo_ref, gfin_ref,
     y_ref, o_scr) = refs
    nb = mk_ref.shape[0]
    tm = x_ref.shape[0]
    rb = tm // nb
    hb = ogla_refs[0].shape[0]
    sb = min(OUT_ROW_BLOCK, tm)
    mb = min(sb, rb)

    def block_stages(r):
        rows = slice(r * sb, (r + 1) * sb)
        part, off = divmod(r * sb, hb)
        loc = {}

        def mix():
            ogla = ogla_refs[part][off:off + sb, :]
            loc["x1"] = (x_ref[rows, :] + _dot(ogla, wout_ref[0:GLA_WIDTH, :])
                         + _dot(omla_ref[rows, :], wout_ref[GLA_WIDTH:GLA_WIDTH + MLA_WIDTH, :]))

        def query():
            h = _rms(loc["x1"], gq_ref[...]).astype(BF16)
            loc["qm"] = (_dot(h, wmq_ref[...]) * (MEM_DH ** -0.5)).astype(BF16)

        def head(hh):
            def run():
                cols = slice(hh * MEM_DH, (hh + 1) * MEM_DH)
                for i in range(sb // mb):
                    bb = (r * sb + i * mb) // rb
                    s = _dot_nt(loc["qm"][i * mb:(i + 1) * mb, cols],
                                mk_ref[bb, :, cols].astype(BF16))
                    p = jnp.exp(s - jnp.max(s, axis=1, keepdims=True))
                    p = p / jnp.sum(p, axis=1, keepdims=True)
                    o_scr[r * sb + i * mb:r * sb + (i + 1) * mb, cols] = _dot(
                        p.astype(BF16), mv_ref[bb, :, cols].astype(BF16)).astype(BF16)
            return run

        def batch(i):
            def run():
                bb = (r * sb + i * mb) // rb
                qb = loc["qm"][i * mb:(i + 1) * mb, :]
                qs = jnp.concatenate([qb[:, hh * MEM_DH:(hh + 1) * MEM_DH]
                                      for hh in range(MEM_HEADS)], axis=0)
                mk2 = mk_ref[bb].reshape(N_MEM * MEM_HEADS, MEM_DH).astype(BF16)
                mv2 = mv_ref[bb].reshape(N_MEM * MEM_HEADS, MEM_DH).astype(BF16)
                s = _dot_nt(qs, mk2)
                row_head = lax.broadcasted_iota(jnp.int32, s.shape, 0) // mb
                col_head = lax.broadcasted_iota(jnp.int32, s.shape, 1) % MEM_HEADS
                s = jnp.where(row_head == col_head, s, NEG_INF)
                p = jnp.exp(s - jnp.max(s, axis=1, keepdims=True))
                p = p / jnp.sum(p, axis=1, keepdims=True)
                o = _dot(p.astype(BF16), mv2).astype(BF16)
                for hh in range(MEM_HEADS):
                    o_scr[r * sb + i * mb:r * sb + (i + 1) * mb,
                          hh * MEM_DH:(hh + 1) * MEM_DH] = o[hh * mb:(hh + 1) * mb, :]
            return run

        def out():
            x2 = loc["x1"] + _dot(o_scr[rows, :], wmo_ref[...])
            y_ref[rows, :] = _rms(x2, gfin_ref[...])

        if len(mk_ref.shape) == 4:
            attend = [batch(i) for i in range(sb // mb)]
        else:
            attend = [head(hh) for hh in range(MEM_HEADS)]
        return [mix, query] + attend + [out]

    for stage in _skewed([block_stages(r) for r in range(tm // sb)]):
        stage()


def _out_mem(x, ogla_parts, omla, mk, mv, wts, *, nb):
    g, tlen, _ = x.shape
    hb = ogla_parts[0].shape[2]
    tm = hb * len(ogla_parts)
    nt = tlen // tm
    tail = (0,) * (mk.ndim - 1)
    mmap = (lambda b, t: (b,) + tail) if nb == 1 else (lambda b, t: (t,) + tail)
    row = lambda w: pl.BlockSpec((None, tm, w), lambda b, t: (b, t, 0))
    part = pl.BlockSpec((None, None, hb, GLA_WIDTH), lambda b, t: (b, t, 0, 0))
    mspec = pl.BlockSpec((nb,) + mk.shape[1:], mmap)
    sq = pl.BlockSpec((D_MODEL, D_MODEL), lambda b, t: (0, 0))
    vec = pl.BlockSpec((1, D_MODEL), lambda b, t: (0, 0))
    return pl.pallas_call(
        functools.partial(_out_mem_body, n_ogla=len(ogla_parts)),
        grid=(g, nt),
        in_specs=[part] * len(ogla_parts) + [row(D_MODEL), row(MLA_WIDTH), mspec, mspec,
                                              sq, vec, sq, sq, vec],
        out_specs=row(D_MODEL),
        out_shape=jax.ShapeDtypeStruct(x.shape, F32),
        scratch_shapes=[pltpu.VMEM((tm, D_MODEL), BF16)],
        compiler_params=pltpu.CompilerParams(dimension_semantics=("arbitrary", "arbitrary"),
                                             vmem_limit_bytes=VMEM_LIMIT),
        name="out_mem",
    )(*ogla_parts, x, omla, mk, mv, *wts)


def _prep_w_in(w_in):
    splits = np.cumsum([GLA_KW, GLA_KW, GLA_WIDTH, GLA_GATE_RANK, GLA_WIDTH, MLA_Q_RANK,
                        MLA_KV_RANK, MLA_ROPE])
    gq, gk, gv, glr, gz, cq, ckv, kr, mz = jnp.split(w_in, splits.tolist(), axis=1)
    zeros = lambda n: jnp.zeros((D_MODEL, n), w_in.dtype)
    misc = jnp.concatenate([glr, zeros(ROPE_LANE0 - GLA_GATE_RANK), kr,
                            zeros(LANES - ROPE_LANE0 - MLA_ROPE)], axis=1)
    return jnp.concatenate([misc, gq, gk, gv, gz, cq, ckv, mz], axis=1).astype(BF16)


def _pad_heads(w, width):
    kdim = w.shape[0]
    w = w.reshape(kdim, MLA_HEADS, width)
    w = jnp.pad(w, ((0, 0), (0, 0), (0, HEAD_PAD - width)))
    return w.reshape(kdim, MLA_HEADS * HEAD_PAD)


def _rope_tables(pos):
    inv = ROPE_THETA ** (-jnp.arange(HALF_ROPE, dtype=F32) / HALF_ROPE)
    ang = pos.astype(F32)[:, None] * inv[None, :]
    c, s = jnp.cos(ang), jnp.sin(ang)
    n = pos.shape[0]
    cos = jnp.concatenate([jnp.ones((n, ROPE_LANE0), F32), c, c,
                           jnp.zeros((n, LANES - ROPE_LANE0 - MLA_ROPE), F32)], axis=1)
    sin = jnp.concatenate([jnp.zeros((n, ROPE_LANE0), F32), -s, s,
                           jnp.zeros((n, LANES - ROPE_LANE0 - MLA_ROPE), F32)], axis=1)
    return cos, sin


def _chunk_onehot(pos):
    cidx = (np.asarray(pos) // CHUNK) % Q_TILE_CHUNKS
    tab = np.zeros((len(cidx), LANES), np.float32)
    tab[np.arange(len(cidx)), MASK_LANE0 + cidx] = 1.0
    return jnp.asarray(tab)


def _gla_mask():
    r = np.arange(GLA_HEADS * CHUNK)
    same_head = (r[:, None] // CHUNK) == (r[None, :] // CHUNK)
    causal = (r[None, :] % CHUNK) <= (r[:, None] % CHUNK)
    return jnp.asarray((same_head & causal).astype(np.float32))


def kernel(x_prompt, x_sample, mem_prompt, cache_mla_ckv, cache_mla_krope, state_gla, cache_mem_k, cache_mem_v, g_mix, w_in, w_gla_g2, b_gla_g, g_gla_out, g_qa, w_qb, g_kva, w_kvb, w_out, g_mem_q, g_mem_kv, w_mq, w_mk, w_mv, w_mo, g_final):
    bp, tp, _ = x_prompt.shape
    bs, ts, _ = x_sample.shape
    past = cache_mla_ckv.shape[2]
    assert ts == CHUNK and g_mix.shape[0] == 1
    l = 0
    row = lambda g: g.reshape(1, -1)

    w_in_p = _prep_w_in(w_in[l])
    wg2_p = jnp.pad(w_gla_g2[l], ((0, LANES - GLA_GATE_RANK), (0, 0))).astype(BF16)
    wqb_p = _pad_heads(w_qb[l], MLA_NOPE + MLA_ROPE).astype(BF16)
    wkv = w_kvb[l].reshape(MLA_KV_RANK, MLA_HEADS, MLA_NOPE + MLA_DV)
    wk_p = _pad_heads(wkv[:, :, :MLA_NOPE].reshape(MLA_KV_RANK, -1), MLA_NOPE).astype(BF16)
    wvt_p = wkv[:, :, MLA_NOPE:].reshape(MLA_KV_RANK, MLA_WIDTH).T.astype(BF16)
    in_wts = (row(g_mix[l]), w_in_p, wg2_p, row(b_gla_g[l]), row(g_gla_out[l]), row(g_qa[l]),
              wqb_p, row(g_kva[l]), wk_p, wvt_p)
    assert len(in_wts) == N_IN_WTS
    out_wts = (w_out[l].astype(BF16), row(g_mem_q[l]), w_mq[l].astype(BF16),
               w_mo[l].astype(BF16), row(g_final))
    amask = _gla_mask()
    sel = jnp.zeros((MLA_ROPE, HEAD_PAD), BF16).at[
        jnp.arange(MLA_ROPE), ROPE_LANE0 + jnp.arange(MLA_ROPE)].set(1.0)

    tm = ROW_BLOCK
    sample_rows = bs * ts
    bpt = tm // ts

    mk_p, mv_p = _mem_kv(mem_prompt.reshape(bp * N_MEM, D_MODEL), row(g_mem_kv[l]),
                         w_mk[l].astype(BF16), w_mv[l].astype(BF16), tm)
    cos_p, sin_p = _rope_tables(jnp.arange(tp, dtype=jnp.int32))
    s0_p = jnp.zeros((bp, GLA_KW, GLA_DV), F32)
    ogla_a, ogla_b, mz_p, q_p, k_p, vt_p, ckv_p, kr_p, st_p = _in_gla(
        x_prompt, s0_p, cos_p, sin_p, _chunk_onehot(np.arange(tp)), amask, in_wts,
        hb=tm, carry=True)
    omla_p = _attention_pipelined(
        q_p, k_p.reshape(bp, MLA_HEADS, tp // tm, tm, HEAD_PAD), vt_p, mz_p)
    y_prompt = _out_mem(x_prompt, (ogla_a, ogla_b), omla_p, mk_p.reshape(bp, N_MEM, D_MODEL),
                        mv_p.reshape(bp, N_MEM, D_MODEL), out_wts, nb=1)

    xs = x_sample.reshape(1, sample_rows, D_MODEL)
    cos_s, sin_s = _rope_tables(past + jnp.arange(ts, dtype=jnp.int32))
    cos_s, sin_s = jnp.tile(cos_s, (bpt, 1)), jnp.tile(sin_s, (bpt, 1))
    s0_s = state_gla[l].reshape(bs, GLA_KW, GLA_DV)
    ogla_s, mz_s, q_s, _, _, ckv_s, kr_s, st_s = _in_gla(
        xs, s0_s, cos_s, sin_s, jnp.zeros((tm, LANES), F32), amask, in_wts, hb=tm, carry=False)
    omla_s = _latent_attention(
        q_s.reshape(MLA_HEADS, bs, ts, HEAD_PAD), ckv_s.reshape(bs, ts, MLA_KV_RANK),
        kr_s.reshape(bs, ts, MLA_ROPE), cache_mla_ckv[l], cache_mla_krope[l],
        mz_s.reshape(bs, ts, MLA_WIDTH), wk_p, wvt_p, sel)
    omla_s = omla_s.reshape(1, sample_rows, MLA_WIDTH)
    nb_s = 4
    y_sample = _out_mem(xs, (ogla_s.reshape(1, bs // nb_s, nb_s * ts, GLA_WIDTH),), omla_s,
                        cache_mem_k[l], cache_mem_v[l], out_wts, nb=nb_s)

    mem_shape = (1, bp, N_MEM, MEM_HEADS, MEM_DH)
    st_shape = (GLA_HEADS, GLA_DK, GLA_DV)
    return (y_prompt, y_sample.reshape(bs, ts, D_MODEL),
            ckv_p[None], kr_p[None], st_p.reshape(1, bp, *st_shape),
            mk_p.reshape(mem_shape), mv_p.reshape(mem_shape),
            ckv_s.reshape(1, bs, ts, MLA_KV_RANK), kr_s.reshape(1, bs, ts, MLA_ROPE),
            st_s.reshape(1, bs, *st_shape))
```
